```python
import jax
import jax.numpy as jnp
from jax import lax
import numpy as np

D_MODEL = 1024
BATCH = 8
SEQ = 8192
DEPTH = 1
DEC_BATCH = 128
DEC_SEQ = 8
PAST_LEN = 8192
PAGE_SIZE = 128

ATTN_GROUPS = ((128, 1), (512, 4), (2048, 16))
N_ATTN_GROUPS = len(ATTN_GROUPS)
HEADS_PER_GROUP = 8
HEAD_DIM = 64
A_WIDTH = HEADS_PER_GROUP * HEAD_DIM
ATTN_SCALE = HEAD_DIM ** -0.5

SSD_EXPAND = 2
D_INNER = SSD_EXPAND * D_MODEL
SSD_HEAD_DIM = 64
SSD_HEADS = D_INNER // SSD_HEAD_DIM
SSD_GROUPS = 4
SSD_HEADS_PER_GROUP = SSD_HEADS // SSD_GROUPS
SSD_STATE = 128
CONV_W = 4
CONV_DIM = D_INNER + 2 * SSD_GROUPS * SSD_STATE
SSD_CHUNK = 128

D_FF = ((8 * D_MODEL + 3 * 256 - 1) // (3 * 256)) * 256

N_IN = 3 * N_ATTN_GROUPS * A_WIDTH + D_INNER + CONV_DIM + SSD_HEADS + 2 * D_MODEL
RMS_EPS = 1e-6

kernel_name = 'hybrid_dilated_attn_ssd_step'


def rmsnorm(x, w):
    xf = x.astype(jnp.float32)
    y = xf * lax.rsqrt(jnp.mean(xf * xf, axis=-1, keepdims=True) + RMS_EPS)
    return (y * w.astype(jnp.float32)).astype(x.dtype)


def gated_group_rmsnorm(y, z, w):
    g = y.astype(jnp.float32) * jax.nn.silu(z.astype(jnp.float32))
    shp = g.shape
    g = g.reshape(shp[:-1] + (SSD_GROUPS, shp[-1] // SSD_GROUPS))
    g = g * lax.rsqrt(jnp.mean(g * g, axis=-1, keepdims=True) + RMS_EPS)
    return (g.reshape(shp) * w.astype(jnp.float32)).astype(y.dtype)


def split_in(u):
    lead = u.shape[:-1]
    heads = lambda t: t.reshape(lead + (HEADS_PER_GROUP, HEAD_DIM))
    qkv = []
    off = 0
    for _ in range(N_ATTN_GROUPS):
        q = heads(u[..., off:off + A_WIDTH])
        k = heads(u[..., off + A_WIDTH:off + 2 * A_WIDTH])
        v = heads(u[..., off + 2 * A_WIDTH:off + 3 * A_WIDTH])
        qkv.append((q, k, v))
        off += 3 * A_WIDTH
    z = u[..., off:off + D_INNER]
    off += D_INNER
    xbc = u[..., off:off + CONV_DIM]
    off += CONV_DIM
    dt_raw = u[..., off:off + SSD_HEADS]
    off += SSD_HEADS
    gate_raw = u[..., off:off + 2 * D_MODEL]
    return qkv, z, xbc, dt_raw, gate_raw


def masked_softmax_stats(s, valid):
    s = jnp.where(valid, s, -jnp.inf)
    m = jnp.max(s, axis=-1, keepdims=True)
    p = jnp.exp(s - m)
    den = jnp.sum(p, axis=-1, keepdims=True)
    return p / den, (m + jnp.log(den))[..., 0]


def dilated_attn_prompt(q, k, v, window, dil):
    b, l, h, e = q.shape
    band = window // dil
    span = dil * band
    lp = -(-l // span) * span
    nb = lp // span

    def to_blocks(t):
        t = jnp.pad(t, ((0, 0), (0, lp - l), (0, 0), (0, 0)))
        t = t.reshape(b, lp // dil, dil, h, e).transpose(0, 2, 1, 3, 4)
        return t.reshape(b, dil, nb, band, h, e)

    def with_prev(t):
        prev = jnp.pad(t, ((0, 0), (0, 0), (1, 0), (0, 0), (0, 0), (0, 0)))[:, :, :-1]
        return jnp.concatenate([prev, t], axis=3)

    qb = to_blocks(q)
    kk = with_prev(to_blocks(k))
    vv = with_prev(to_blocks(v))
    s = jnp.einsum('brnqhe,brnkhe->brnhqk', qb, kk).astype(jnp.float32) * ATTN_SCALE
    qi = jnp.arange(band)[:, None]
    ki = jnp.arange(2 * band)[None, :]
    dist = qi + band - ki
    band_ok = (dist >= 0) & (dist <= band)
    blk = jnp.arange(nb)[:, None, None]
    valid = band_ok[None] & ((blk > 0) | (ki[None] >= band))
    p, lse = masked_softmax_stats(s, valid[None, None, :, None])
    o = jnp.einsum('brnhqk,brnkhe->brnqhe', p.astype(v.dtype), vv)
    lse = lse.transpose(0, 1, 2, 4, 3)

    def from_blocks(t):
        t = t.reshape((b, dil, lp // dil) + t.shape[4:])
        t = jnp.moveaxis(t, 1, 2).reshape((b, lp) + t.shape[3:])
        return t[:, :l]

    return from_blocks(o), from_blocks(lse)


def dilated_attn_sample(q, kv_all, past_rows, window, dil):
    b, s_len, h, e = q.shape
    nk = window // dil + 1
    idx = past_rows + jnp.arange(s_len)[:, None] - dil * jnp.arange(nk)[None, :]
    valid = idx >= 0
    g = jnp.take(kv_all, jnp.maximum(idx, 0).reshape(-1), axis=1).reshape(b, s_len, nk, 2, h, e)
    s = jnp.einsum('bshe,bskhe->bshk', q, g[:, :, :, 0]).astype(jnp.float32) * ATTN_SCALE
    p, lse = masked_softmax_stats(s, valid[None, :, None, :])
    o = jnp.einsum('bshk,bskhe->bshe', p.astype(q.dtype), g[:, :, :, 1])
    return o, lse


def merge_dilation_groups(outs, lses):
    w = jax.nn.softmax(jnp.stack(lses, axis=0), axis=0)
    o = jnp.einsum('gblh,gblhe->blhe', w, jnp.stack(outs, axis=0).astype(jnp.float32))
    return o.reshape(o.shape[:2] + (A_WIDTH,))


def ssd_chunked(x, dt, a, bm, cm, s0):
    b, l = x.shape[:2]
    q = min(SSD_CHUNK, l)
    lp = -(-l // q) * q
    nc = lp // q

    def chunks(t):
        t = jnp.pad(t.astype(jnp.float32), ((0, 0), (0, lp - l)) + ((0, 0),) * (t.ndim - 2))
        return jnp.moveaxis(t.reshape((b, nc, q) + t.shape[2:]), 1, 0)

    xs = (chunks(x), chunks(dt), chunks(bm), chunks(cm))
    causal = jnp.tril(jnp.ones((q, q), dtype=bool))

    def step(s, inp):
        xc, dtc, bc, cc = inp
        cum = jnp.cumsum(dtc * a, axis=1)
        seg = cum[:, :, None] - cum[:, None, :]
        decay = jnp.exp(jnp.where(causal[None, :, :, None, None], seg, -jnp.inf))
        cb = jnp.einsum('btgn,bsgn->btsg', cc, bc)
        w = cb[..., None] * decay * dtc[:, None]
        y = jnp.einsum('btsgh,bsghp->btghp', w, xc)
        y = y + jnp.einsum('btgn,bghpn->btghp', cc, s) * jnp.exp(cum)[..., None]
        to_end = jnp.exp(cum[:, -1:] - cum) * dtc
        s = jnp.exp(cum[:, -1])[..., None, None] * s + jnp.einsum('bsgh,bsghp,bsgn->bghpn', to_end, xc, bc)
        return s, y

    s_fin, ys = lax.scan(step, s0.astype(jnp.float32), xs)
    y = jnp.moveaxis(ys, 0, 1).reshape((b, lp) + x.shape[2:])[:, :l]
    return y, s_fin


def layer(x, kv_past, conv_prev, ssm_prev, p):
    b, l, _ = x.shape
    h = rmsnorm(x, p['pre_mix_norm'])
    u = h @ p['w_in']
    qkv, z, xbc, dt_raw, gate_raw = split_in(u)

    outs, lses, new_kv = [], [], []
    for gi, (window, dil) in enumerate(ATTN_GROUPS):
        q, k, v = qkv[gi]
        kv_new = jnp.stack([k, v], axis=2)
        if kv_past is None:
            o, lse = dilated_attn_prompt(q, k, v, window, dil)
            keep = min(window, l)
            new_kv.append(kv_new[:, l - keep:])
        else:
            past = kv_past[gi].astype(kv_new.dtype)
            rows = past.shape[1]
            kv_all = jnp.concatenate([past, kv_new], axis=1)
            o, lse = dilated_attn_sample(q, kv_all, rows, window, dil)
            new_kv.append(kv_all[:, kv_all.shape[1] - rows:])
        outs.append(o)
        lses.append(lse)
    a_out = merge_dilation_groups(outs, lses).astype(x.dtype)

    conv_in = jnp.concatenate([conv_prev.astype(x.dtype), xbc], axis=1)
    conv = p['conv_b']
    for i in range(CONV_W):
        conv = conv + conv_in[:, i:i + l] * p['conv_w'][i]
    conv = jax.nn.silu(conv)
    new_conv = conv_in[:, conv_in.shape[1] - (CONV_W - 1):]
    xs = conv[..., :D_INNER].reshape(b, l, SSD_GROUPS, SSD_HEADS_PER_GROUP, SSD_HEAD_DIM)
    bm = conv[..., D_INNER:D_INNER + SSD_GROUPS * SSD_STATE].reshape(b, l, SSD_GROUPS, SSD_STATE)
    cm = conv[..., D_INNER + SSD_GROUPS * SSD_STATE:].reshape(b, l, SSD_GROUPS, SSD_STATE)
    dt = jax.nn.softplus(dt_raw.astype(jnp.float32) + p['dt_bias'].astype(jnp.float32))
    dt = dt.reshape(b, l, SSD_GROUPS, SSD_HEADS_PER_GROUP)
    a = -jnp.exp(p['a_log'].astype(jnp.float32)).reshape(SSD_GROUPS, SSD_HEADS_PER_GROUP)
    s0 = ssm_prev.reshape(b, SSD_GROUPS, SSD_HEADS_PER_GROUP, SSD_HEAD_DIM, SSD_STATE)
    y, s_new = ssd_chunked(xs, dt, a, bm, cm, s0)
    d_skip = p['d_skip'].astype(jnp.float32).reshape(SSD_GROUPS, SSD_HEADS_PER_GROUP)
    y = y + d_skip[:, :, None] * xs.astype(jnp.float32)
    y = gated_group_rmsnorm(y.reshape(b, l, D_INNER).astype(x.dtype), z, p['ssd_norm'])

    a_proj = a_out @ p['w_attn_out']
    b_proj = y @ p['w_ssd_out']
    gate_a = jax.nn.sigmoid(gate_raw[..., :D_MODEL])
    gate_b = jax.nn.sigmoid(gate_raw[..., D_MODEL:])
    mixed = (gate_a * a_proj + gate_b * b_proj) @ p['w_out']
    x = x + rmsnorm(mixed, p['post_mix_norm'])

    h2 = rmsnorm(x, p['pre_ffn_norm'])
    gu = h2 @ p['w_up']
    f = (jax.nn.silu(gu[..., :D_FF]) * gu[..., D_FF:]) @ p['w_down']
    x = x + rmsnorm(f, p['post_ffn_norm'])

    ssm_out = s_new.reshape(b, SSD_HEADS, SSD_HEAD_DIM, SSD_STATE).astype(x.dtype)
    return x, new_kv, new_conv, ssm_out


def setup_inputs(seed: int = 0) -> dict:
    key = jax.random.key(seed)
    ks = jax.random.split(key, 26)
    f32 = jnp.float32
    nrm = lambda k, shape, scale: jax.random.normal(k, shape, f32) * scale
    kv_shape = lambda w: (DEPTH, DEC_BATCH, min(w, PAST_LEN), 2, HEADS_PER_GROUP, HEAD_DIM)
    dt0 = jnp.exp(jax.random.uniform(ks[9], (DEPTH, SSD_HEADS), f32, np.log(1e-3), np.log(1e-1)))
    return {
        'x_prompt': nrm(ks[0], (BATCH, SEQ, D_MODEL), 1.0),
        'x_sample': nrm(ks[1], (DEC_BATCH, DEC_SEQ, D_MODEL), 1.0),
        'cache_kv_w128': nrm(ks[2], kv_shape(ATTN_GROUPS[0][0]), 1.0),
        'cache_kv_w512': nrm(ks[3], kv_shape(ATTN_GROUPS[1][0]), 1.0),
        'cache_kv_w2048': nrm(ks[4], kv_shape(ATTN_GROUPS[2][0]), 1.0),
        'state_conv': nrm(ks[5], (DEPTH, DEC_BATCH, CONV_W - 1, CONV_DIM), 1.0),
        'state_ssm': nrm(ks[6], (DEPTH, DEC_BATCH, SSD_HEADS, SSD_HEAD_DIM, SSD_STATE), 0.5),
        'pre_mix_norm': 1.0 + nrm(ks[7], (DEPTH, D_MODEL), 0.05),
        'w_in': nrm(ks[8], (DEPTH, D_MODEL, N_IN), D_MODEL ** -0.5),
        'conv_w': nrm(ks[10], (DEPTH, CONV_W, CONV_DIM), CONV_W ** -0.5),
        'conv_b': nrm(ks[11], (DEPTH, CONV_DIM), 0.02),
        'dt_bias': dt0 + jnp.log(-jnp.expm1(-dt0)),
        'a_log': jnp.log(jax.random.uniform(ks[12], (DEPTH, SSD_HEADS), f32, 1.0, 16.0)),
        'd_skip': 1.0 + nrm(ks[13], (DEPTH, SSD_HEADS), 0.1),
        'ssd_norm': 1.0 + nrm(ks[14], (DEPTH, D_INNER), 0.05),
        'w_attn_out': nrm(ks[15], (DEPTH, A_WIDTH, D_MODEL), A_WIDTH ** -0.5),
        'w_ssd_out': nrm(ks[16], (DEPTH, D_INNER, D_MODEL), D_INNER ** -0.5),
        'w_out': nrm(ks[17], (DEPTH, D_MODEL, D_MODEL), D_MODEL ** -0.5),
        'post_mix_norm': 1.0 + nrm(ks[18], (DEPTH, D_MODEL), 0.05),
        'pre_ffn_norm': 1.0 + nrm(ks[19], (DEPTH, D_MODEL), 0.05),
        'w_up': nrm(ks[20], (DEPTH, D_MODEL, 2 * D_FF), D_MODEL ** -0.5),
        'w_down': nrm(ks[21], (DEPTH, D_FF, D_MODEL), D_FF ** -0.5),
        'post_ffn_norm': 1.0 + nrm(ks[22], (DEPTH, D_MODEL), 0.05),
    }


def reference(x_prompt, x_sample, cache_kv_w128, cache_kv_w512, cache_kv_w2048, state_conv, state_ssm,
              pre_mix_norm, w_in, conv_w, conv_b, dt_bias, a_log, d_skip, ssd_norm,
              w_attn_out, w_ssd_out, w_out, post_mix_norm, pre_ffn_norm, w_up, w_down, post_ffn_norm):
    yp, ys = x_prompt, x_sample
    pk0, pk1, pk2, pconv, pssm = [], [], [], [], []
    sk0, sk1, sk2, sconv, sssm = [], [], [], [], []
    for li in range(DEPTH):
        p = {
            'pre_mix_norm': pre_mix_norm[li], 'w_in': w_in[li], 'conv_w': conv_w[li], 'conv_b': conv_b[li],
            'dt_bias': dt_bias[li], 'a_log': a_log[li], 'd_skip': d_skip[li], 'ssd_norm': ssd_norm[li],
            'w_attn_out': w_attn_out[li], 'w_ssd_out': w_ssd_out[li], 'w_out': w_out[li],
            'post_mix_norm': post_mix_norm[li], 'pre_ffn_norm': pre_ffn_norm[li], 'w_up': w_up[li],
            'w_down': w_down[li], 'post_ffn_norm': post_ffn_norm[li],
        }
        b_p = yp.shape[0]
        conv0 = jnp.zeros((b_p, CONV_W - 1, CONV_DIM), yp.dtype)
        ssm0 = jnp.zeros((b_p, SSD_HEADS, SSD_HEAD_DIM, SSD_STATE), jnp.float32)
        yp, kvp, cvp, smp = layer(yp, None, conv0, ssm0, p)
        past = (cache_kv_w128[li], cache_kv_w512[li], cache_kv_w2048[li])
        ys, kvs, cvs, sms = layer(ys, past, state_conv[li], state_ssm[li], p)
        pk0.append(kvp[0]); pk1.append(kvp[1]); pk2.append(kvp[2]); pconv.append(cvp); pssm.append(smp)
        sk0.append(kvs[0]); sk1.append(kvs[1]); sk2.append(kvs[2]); sconv.append(cvs); sssm.append(sms)
    kv_w128_prompt = jnp.stack(pk0, axis=0)
    kv_w512_prompt = jnp.stack(pk1, axis=0)
    kv_w2048_prompt = jnp.stack(pk2, axis=0)
    conv_prompt = jnp.stack(pconv, axis=0)
    ssm_prompt = jnp.stack(pssm, axis=0)
    kv_w128_sample = jnp.stack(sk0, axis=0)
    kv_w512_sample = jnp.stack(sk1, axis=0)
    kv_w2048_sample = jnp.stack(sk2, axis=0)
    conv_sample = jnp.stack(sconv, axis=0)
    ssm_sample = jnp.stack(sssm, axis=0)
    return (yp, ys, kv_w128_prompt, kv_w512_prompt, kv_w2048_prompt, conv_prompt, ssm_prompt,
            kv_w128_sample, kv_w512_sample, kv_w2048_sample, conv_sample, ssm_sample)
```

```python
import functools

import jax
import jax.numpy as jnp
from jax import lax
from jax.experimental import pallas as pl
from jax.experimental.pallas import tpu as pltpu

F32 = jnp.float32
BF16 = jnp.bfloat16

D_MODEL = 1024
ATTN_GROUPS = ((128, 1), (512, 4), (2048, 16))
BAND = 128
HEADS = 8
HEAD_DIM = 64
A_WIDTH = HEADS * HEAD_DIM
ATTN_SCALE = HEAD_DIM ** -0.5
ATTN_TILE = 2048

D_INNER = 2048
SSD_HEADS = 32
SSD_HEAD_DIM = 64
SSD_GROUPS = 4
SSD_GROUP_W = D_INNER // SSD_GROUPS
SSD_STATE = 128
CONV_W = 4
CHUNK = 128
D_FF = 2816
RMS_EPS = 1e-6
LANES = 128
HALO = 8

Z_OFF = 0
GATE_OFF = 2048
X_OFF = 4096
QKV_OFF = 6144
B_OFF = 10752
C_OFF = 11264
N_MAIN = 11776

VMEM_LIMIT = 56 * 1024 * 1024


def _cparams(sem):
    return pltpu.CompilerParams(dimension_semantics=sem, vmem_limit_bytes=VMEM_LIMIT)


def _rms(x, w):
    return x * lax.rsqrt(jnp.mean(x * x, axis=-1, keepdims=True) + RMS_EPS) * w


def _silu(x):
    return x * (1.0 / (1.0 + jnp.exp(-x)))


def _sigmoid(x):
    return 1.0 / (1.0 + jnp.exp(-x))


def _dot(a, b):
    return jnp.dot(a, b, preferred_element_type=F32)


def _dot_nt(a, b):
    return lax.dot_general(a, b, (((1,), (1,)), ((), ())), preferred_element_type=F32)


def _dot_tn(a, b):
    return lax.dot_general(a, b, (((0,), (0,)), ((), ())), preferred_element_type=F32)


def _expand01(a, e):
    hi = a.astype(BF16)
    mid = (a - hi.astype(F32)).astype(BF16)
    return _dot(hi, e) + _dot(mid, e)


def _inproj_kernel(x_ref, nw_ref, w_ref, wdt_ref, u_ref, dt_ref, h_ref):
    @pl.when(pl.program_id(1) == 0)
    def _():
        hb = _rms(x_ref[...], nw_ref[...]).astype(BF16)
        h_ref[...] = hb
        dt_ref[...] = _dot(hb, wdt_ref[...])

    u_ref[...] = _dot(h_ref[...], w_ref[...])


def _inproj(x, nw, w_main, w_dt, tm, tn):
    t = x.shape[0]
    return pl.pallas_call(
        _inproj_kernel,
        grid=(t // tm, N_MAIN // tn),
        in_specs=[
            pl.BlockSpec((tm, D_MODEL), lambda i, j: (i, 0)),
            pl.BlockSpec((1, D_MODEL), lambda i, j: (0, 0)),
            pl.BlockSpec((D_MODEL, tn), lambda i, j: (0, j)),
            pl.BlockSpec((D_MODEL, LANES), lambda i, j: (0, 0)),
        ],
        out_specs=[
            pl.BlockSpec((tm, tn), lambda i, j: (i, j)),
            pl.BlockSpec((tm, LANES), lambda i, j: (i, 0)),
        ],
        out_shape=[
            jax.ShapeDtypeStruct((t, N_MAIN), F32),
            jax.ShapeDtypeStruct((t, LANES), F32),
        ],
        scratch_shapes=[pltpu.VMEM((tm, D_MODEL), BF16)],
        compiler_params=_cparams(("parallel", "arbitrary")),
        name="inproj",
    )(x, nw, w_main, w_dt)


def _attn_prompt_kernel(q0, k0, v0, q1, k1, v1, q2, k2, v2, o_ref,
                        kd0, vd0, kd1, vd1, kd2, vd2, lse_ref, acc_ref):
    n = pl.program_id(2)
    q_refs, k_refs, v_refs = (q0, q1, q2), (k0, k1, k2), (v0, v1, v2)
    kds, vds = (kd0, kd1, kd2), (vd0, vd1, vd2)
    lane = lax.broadcasted_iota(jnp.int32, (BAND, LANES), 1)
    row = lax.broadcasted_iota(jnp.int32, (BAND, LANES), 0)
    head0 = lane < HEAD_DIM
    cur_ok = lane <= row
    rel = lane - row

    for g, (_, dil) in enumerate(ATTN_GROUPS):
        span = BAND * dil
        nsub = ATTN_TILE // span
        nblk = nsub * dil
        q_ref, k_ref, v_ref, kd, vd = q_refs[g], k_refs[g], v_refs[g], kds[g], vds[g]
        carry = dil * BAND

        @pl.when(n == 0)
        def _():
            kd[0:carry, :] = jnp.zeros((carry, LANES), BF16)
            vd[0:carry, :] = jnp.zeros((carry, LANES), BF16)

        @pl.when(n > 0)
        def _():
            kd[0:carry, :] = kd[nblk * BAND:nblk * BAND + carry, :]
            vd[0:carry, :] = vd[nblk * BAND:nblk * BAND + carry, :]

        def rows_of(i):
            if dil == 1:
                return pl.ds(pl.multiple_of(i * BAND, BAND), BAND)
            j = i // dil
            r = i - j * dil
            return pl.ds(j * span + r, BAND, stride=dil)

        def deinterleave(i, c):
            rows = rows_of(i)
            slot = pl.ds(pl.multiple_of((i + dil) * BAND, BAND), BAND)
            kd[slot, :] = k_ref[rows, :].astype(BF16)
            vd[slot, :] = v_ref[rows, :].astype(BF16)
            return c

        lax.fori_loop(0, nblk, deinterleave, 0)

        def block(i, c):
            rows = rows_of(i)
            q = q_ref[rows, :] * ATTN_SCALE
            pslot = pl.ds(pl.multiple_of(i * BAND, BAND), BAND)
            cslot = pl.ds(pl.multiple_of((i + dil) * BAND, BAND), BAND)
            kp, kc, vp, vc = kd[pslot, :], kd[cslot, :], vd[pslot, :], vd[cslot, :]
            has_prev = jnp.logical_or(i >= dil, n > 0)
            prev_ok = rel >= jnp.where(has_prev, 0, 2 * BAND)
            outs, lses = [], []
            for hh in range(2):
                hmask = head0 if hh == 0 else jnp.logical_not(head0)
                qm = jnp.where(hmask, q, 0.0).astype(BF16)
                sp = jnp.where(prev_ok, _dot_nt(qm, kp), -jnp.inf)
                sc = jnp.where(cur_ok, _dot_nt(qm, kc), -jnp.inf)
                m = jnp.maximum(jnp.max(sp, axis=1, keepdims=True), jnp.max(sc, axis=1, keepdims=True))
                pp = jnp.exp(sp - m)
                pc = jnp.exp(sc - m)
                den = jnp.sum(pp, axis=1, keepdims=True) + jnp.sum(pc, axis=1, keepdims=True)
                o = _dot(pp.astype(BF16), vp) + _dot(pc.astype(BF16), vc)
                outs.append(o * (1.0 / den))
                lses.append(m + jnp.log(den))
            o_blk = jnp.where(head0, outs[0], outs[1])
            lse_blk = jnp.where(head0, lses[0], lses[1])
            if g == 0:
                lse_ref[rows, :] = lse_blk
                acc_ref[rows, :] = o_blk
            else:
                lse_run = lse_ref[rows, :]
                acc_run = acc_ref[rows, :]
                mx = jnp.maximum(lse_run, lse_blk)
                ea = jnp.exp(lse_run - mx)
                eb = jnp.exp(lse_blk - mx)
                tot = ea + eb
                inv = 1.0 / tot
                merged = (ea * inv) * acc_run + (eb * inv) * o_blk
                if g == len(ATTN_GROUPS) - 1:
                    o_ref[rows, :] = merged
                else:
                    lse_ref[rows, :] = mx + jnp.log(tot)
                    acc_ref[rows, :] = merged
            return c

        lax.fori_loop(0, nblk, block, 0)


def _attn_prompt(u, batch, seq):
    nt = seq // ATTN_TILE
    hp = A_WIDTH // LANES
    qkv_blk = QKV_OFF // LANES

    def spec(g, which):
        col = qkv_blk + (3 * g + which) * hp
        return pl.BlockSpec((ATTN_TILE, LANES), lambda b, h, n, col=col: (b * nt + n, col + h))

    in_specs = [spec(g, w) for g in range(3) for w in range(3)]
    scratch = []
    for _, dil in ATTN_GROUPS:
        rows = ATTN_TILE + dil * BAND
        scratch += [pltpu.VMEM((rows, LANES), BF16), pltpu.VMEM((rows, LANES), BF16)]
    scratch += [pltpu.VMEM((ATTN_TILE, LANES), F32), pltpu.VMEM((ATTN_TILE, LANES), F32)]
    return pl.pallas_call(
        _attn_prompt_kernel,
        grid=(batch, hp, nt),
        in_specs=in_specs,
        out_specs=pl.BlockSpec((ATTN_TILE, LANES), lambda b, h, n: (b * nt + n, h)),
        out_shape=jax.ShapeDtypeStruct((batch * seq, A_WIDTH), F32),
        scratch_shapes=scratch,
        compiler_params=_cparams(("parallel", "parallel", "arbitrary")),
        name="attn_prompt",
    )(*([u] * 9))


def _conv_silu(pad_ref, w_ref, b_ref, rows):
    acc = b_ref[...] + pad_ref[HALO - (CONV_W - 1):HALO - (CONV_W - 1) + rows, :] * w_ref[0:1, :]
    for i in range(1, CONV_W):
        off = HALO - (CONV_W - 1) + i
        acc = acc + pad_ref[off:off + rows, :] * w_ref[i:i + 1, :]
    return _silu(acc)


def _softplus(x):
    return jnp.maximum(x, 0.0) + jnp.log1p(jnp.exp(-jnp.abs(x)))


def _ssd_prompt_kernel(x_ref, b_ref, c_ref, dt_ref, cwx, cwb, cwc, cbx, cbb, cbc,
                       dtb_ref, alog_ref, dexp_ref, e_ref, y_ref, st_ref,
                       xpad, bpad, cpad, state):
    c = pl.program_id(1)
    nc = pl.num_programs(1)

    @pl.when(c == 0)
    def _():
        xpad[0:HALO, :] = jnp.zeros((HALO, D_INNER), F32)
        bpad[0:HALO, :] = jnp.zeros((HALO, SSD_GROUP_W), F32)
        cpad[0:HALO, :] = jnp.zeros((HALO, SSD_GROUP_W), F32)
        state[...] = jnp.zeros(state.shape, F32)

    @pl.when(c > 0)
    def _():
        xpad[0:HALO, :] = xpad[CHUNK:CHUNK + HALO, :]
        bpad[0:HALO, :] = bpad[CHUNK:CHUNK + HALO, :]
        cpad[0:HALO, :] = cpad[CHUNK:CHUNK + HALO, :]

    xpad[HALO:HALO + CHUNK, :] = x_ref[...]
    bpad[HALO:HALO + CHUNK, :] = b_ref[...]
    cpad[HALO:HALO + CHUNK, :] = c_ref[...]
    xc = _conv_silu(xpad, cwx, cbx, CHUNK)
    bm = _conv_silu(bpad, cwb, cbb, CHUNK)
    cm = _conv_silu(cpad, cwc, cbc, CHUNK)

    dt = _softplus(dt_ref[...] + dtb_ref[...])
    a = -jnp.exp(alog_ref[...])
    dta = dt * a
    ti = lax.broadcasted_iota(jnp.int32, (CHUNK, CHUNK), 0)
    si = lax.broadcasted_iota(jnp.int32, (CHUNK, CHUNK), 1)
    causal = si <= ti
    tril = jnp.where(causal, 1.0, 0.0).astype(F32)
    cum = jnp.dot(tril, dta, preferred_element_type=F32, precision=lax.Precision.HIGHEST)
    cum_t = cum.T
    dt_t = dt.T
    cum_last = cum[CHUNK - 1:CHUNK, :]
    expcum = jnp.exp(cum)
    to_end = jnp.exp(cum_last - cum) * dt
    ex = _expand01(jnp.concatenate([expcum, to_end], axis=0), e_ref[...])
    expcum_x = ex[0:CHUNK, :]
    to_end_x = ex[CHUNK:2 * CHUNK, :]
    dec_x = expcum_x[CHUNK - 1:CHUNK, :]

    lane = lax.broadcasted_iota(jnp.int32, (CHUNK, LANES), 1)
    head0 = lane < SSD_HEAD_DIM
    hpg = SSD_HEADS // SSD_GROUPS
    for g in range(SSD_GROUPS):
        gs = slice(g * SSD_GROUP_W, (g + 1) * SSD_GROUP_W)
        ns = slice(g * SSD_STATE, (g + 1) * SSD_STATE)
        bg = bm[:, ns].astype(BF16)
        cg = cm[:, ns].astype(BF16)
        cb = _dot_nt(cg, bg)
        for pr in range(hpg // 2):
            ws = []
            for hh in range(2):
                h = g * hpg + 2 * pr + hh
                colb = jnp.broadcast_to(cum[:, h:h + 1], (CHUNK, CHUNK))
                rowb = jnp.broadcast_to(cum_t[h:h + 1, :], (CHUNK, CHUNK))
                dtrow = jnp.broadcast_to(dt_t[h:h + 1, :], (CHUNK, CHUNK))
                decay = jnp.exp(jnp.where(causal, colb - rowb, -jnp.inf))
                ws.append((cb * decay * dtrow).astype(BF16))
            w2 = jnp.concatenate(ws, axis=1)
            lo = g * SSD_GROUP_W + pr * LANES
            xp = xc[:, lo:lo + LANES]
            x2 = jnp.concatenate([jnp.where(head0, xp, 0.0), jnp.where(head0, 0.0, xp)], axis=0).astype(BF16)
            y_ref[:, lo:lo + LANES] = _dot(w2, x2)
        st = state[g]
        y_state = _dot(cg, st.astype(BF16))
        y_ref[:, gs] = (y_ref[:, gs] + y_state * expcum_x[:, gs] + dexp_ref[:, gs] * xc[:, gs])
        xs = (xc[:, gs] * to_end_x[:, gs]).astype(BF16)
        state[g] = dec_x[:, gs] * st + _dot_tn(bg, xs)

    @pl.when(c == nc - 1)
    def _():
        for g in range(SSD_GROUPS):
            st_ref[0, g * hpg:(g + 1) * hpg, :, :] = state[g].T.reshape(hpg, SSD_HEAD_DIM, SSD_STATE)


def _ssd_prompt(u, dt_raw, conv_w, conv_b, dt_bias, a_log, d_exp, e_mat, batch, seq):
    nc = seq // CHUNK
    cwx, cwb, cwc = conv_w[:, :D_INNER], conv_w[:, D_INNER:D_INNER + SSD_GROUP_W], conv_w[:, D_INNER + SSD_GROUP_W:]
    cbx, cbb, cbc = conv_b[:, :D_INNER], conv_b[:, D_INNER:D_INNER + SSD_GROUP_W], conv_b[:, D_INNER + SSD_GROUP_W:]
    full = lambda shape: pl.BlockSpec(shape, lambda b, c: (0,) * len(shape))
    return pl.pallas_call(
        _ssd_prompt_kernel,
        grid=(batch, nc),
        in_specs=[
            pl.BlockSpec((CHUNK, D_INNER), lambda b, c: (b * nc + c, X_OFF // D_INNER)),
            pl.BlockSpec((CHUNK, SSD_GROUP_W), lambda b, c: (b * nc + c, B_OFF // SSD_GROUP_W)),
            pl.BlockSpec((CHUNK, SSD_GROUP_W), lambda b, c: (b * nc + c, C_OFF // SSD_GROUP_W)),
            pl.BlockSpec((CHUNK, LANES), lambda b, c: (b * nc + c, 0)),
            full((CONV_W, D_INNER)), full((CONV_W, SSD_GROUP_W)), full((CONV_W, SSD_GROUP_W)),
            full((1, D_INNER)), full((1, SSD_GROUP_W)), full((1, SSD_GROUP_W)),
            full((1, LANES)), full((1, LANES)), full((1, D_INNER)), full((LANES, D_INNER)),
        ],
        out_specs=[
            pl.BlockSpec((CHUNK, D_INNER), lambda b, c: (b * nc + c, 0)),
            pl.BlockSpec((1, SSD_HEADS, SSD_HEAD_DIM, SSD_STATE), lambda b, c: (b, 0, 0, 0)),
        ],
        out_shape=[
            jax.ShapeDtypeStruct((batch * seq, D_INNER), F32),
            jax.ShapeDtypeStruct((batch, SSD_HEADS, SSD_HEAD_DIM, SSD_STATE), F32),
        ],
        scratch_shapes=[
            pltpu.VMEM((HALO + CHUNK, D_INNER), F32),
            pltpu.VMEM((HALO + CHUNK, SSD_GROUP_W), F32),
            pltpu.VMEM((HALO + CHUNK, SSD_GROUP_W), F32),
            pltpu.VMEM((SSD_GROUPS, SSD_STATE, SSD_GROUP_W), F32),
        ],
        compiler_params=_cparams(("parallel", "arbitrary")),
        name="ssd_prompt",
    )(u, u, u, dt_raw, cwx, cwb, cwc, cbx, cbb, cbc, dt_bias, a_log, d_exp, e_mat)


def _mix_kernel(a_ref, y_ref, z_ref, ga_ref, gb_ref, x_ref, wa_ref, ws_ref, wo_ref,
                sn_ref, pn_ref, o_ref):
    a_proj = _dot(a_ref[...].astype(BF16), wa_ref[...])
    g = y_ref[...] * _silu(z_ref[...])
    parts = []
    for i in range(SSD_GROUPS):
        gi = g[:, i * SSD_GROUP_W:(i + 1) * SSD_GROUP_W]
        parts.append(gi * lax.rsqrt(jnp.mean(gi * gi, axis=-1, keepdims=True) + RMS_EPS))
    yn = (jnp.concatenate(parts, axis=1) * sn_ref[...]).astype(BF16)
    b_proj = _dot(yn, ws_ref[...])
    mixed_in = _sigmoid(ga_ref[...]) * a_proj + _sigmoid(gb_ref[...]) * b_proj
    mixed = _dot(mixed_in.astype(BF16), wo_ref[...])
    o_ref[...] = x_ref[...] + _rms(mixed, pn_ref[...])


def _mix(a_out, y, u, x, w_attn_out, w_ssd_out, w_out, ssd_norm, post_mix_norm, tm):
    t = x.shape[0]
    const = lambda shape: pl.BlockSpec(shape, lambda i: (0, 0), pipeline_mode=pl.Buffered(1))
    return pl.pallas_call(
        _mix_kernel,
        grid=(t // tm,),
        in_specs=[
            pl.BlockSpec((tm, A_WIDTH), lambda i: (i, 0)),
            pl.BlockSpec((tm, D_INNER), lambda i: (i, 0)),
            pl.BlockSpec((tm, D_INNER), lambda i: (i, Z_OFF // D_INNER)),
            pl.BlockSpec((tm, D_MODEL), lambda i: (i, GATE_OFF // D_MODEL)),
            pl.BlockSpec((tm, D_MODEL), lambda i: (i, GATE_OFF // D_MODEL + 1)),
            pl.BlockSpec((tm, D_MODEL), lambda i: (i, 0)),
            const((A_WIDTH, D_MODEL)), const((D_INNER, D_MODEL)), const((D_MODEL, D_MODEL)),
            const((1, D_INNER)), const((1, D_MODEL)),
        ],
        out_specs=pl.BlockSpec((tm, D_MODEL), lambda i: (i, 0)),
        out_shape=jax.ShapeDtypeStruct((t, D_MODEL), F32),
        compiler_params=_cparams(("parallel",)),
        name="mix",
    )(a_out, y, u, u, u, x, w_attn_out, w_ssd_out, w_out, ssd_norm, post_mix_norm)


def _ffn_kernel(x_ref, wu_ref, wd_ref, n1_ref, n2_ref, o_ref):
    x = x_ref[...]
    h = _rms(x, n1_ref[...]).astype(BF16)
    gu = _dot(h, wu_ref[...])
    act = (_silu(gu[:, :D_FF]) * gu[:, D_FF:]).astype(BF16)
    f = _dot(act, wd_ref[...])
    o_ref[...] = x + _rms(f, n2_ref[...])


def _ffn(x, w_up, w_down, pre_norm, post_norm, tm):
    t = x.shape[0]
    const = lambda shape: pl.BlockSpec(shape, lambda i: (0, 0), pipeline_mode=pl.Buffered(1))
    return pl.pallas_call(
        _ffn_kernel,
        grid=(t // tm,),
        in_specs=[
            pl.BlockSpec((tm, D_MODEL), lambda i: (i, 0)),
            const((D_MODEL, 2 * D_FF)), const((D_FF, D_MODEL)),
            const((1, D_MODEL)), const((1, D_MODEL)),
        ],
        out_specs=pl.BlockSpec((tm, D_MODEL), lambda i: (i, 0)),
        out_shape=jax.ShapeDtypeStruct((t, D_MODEL), F32),
        compiler_params=_cparams(("parallel",)),
        name="ffn",
    )(x, w_up, w_down, pre_norm, post_norm)


def _attn_sample_kernel(q_ref, knt_ref, vnt_ref, c0, c1, c2, ao_ref, o0, o1, o2):
    n_new = q_ref.shape[3]
    caches, outs = (c0, c1, c2), (o0, o1, o2)
    m_run = l_run = o_run = None
    for g, (win, dil) in enumerate(ATTN_GROUPS):
        q = (q_ref[0, 0, g] * ATTN_SCALE).astype(BF16)
        k_t = caches[g][0, 0, 0]
        v_t = caches[g][0, 1, 0]
        knt = knt_ref[0, 0, g]
        vnt = vnt_ref[0, 0, g]
        qi = lax.broadcasted_iota(jnp.int32, (n_new, win), 0)
        ci = lax.broadcasted_iota(jnp.int32, (n_new, win), 1)
        ok_c = jnp.logical_and(ci >= qi, ((ci - qi) & (dil - 1)) == 0)
        qn = lax.broadcasted_iota(jnp.int32, (n_new, LANES), 0)
        nn = lax.broadcasted_iota(jnp.int32, (n_new, LANES), 1) - (LANES - n_new)
        ok_n = jnp.logical_and(jnp.logical_and(nn >= 0, nn <= qn), ((qn - nn) & (dil - 1)) == 0)
        s_c = jnp.where(ok_c, _dot(q, k_t.astype(BF16)), -jnp.inf)
        s_n = jnp.where(ok_n, _dot(q, knt.astype(BF16)), -jnp.inf)
        m = jnp.maximum(jnp.max(s_c, axis=1, keepdims=True), jnp.max(s_n, axis=1, keepdims=True))
        p_c = jnp.exp(s_c - m)
        p_n = jnp.exp(s_n - m)
        den = jnp.sum(p_c, axis=1, keepdims=True) + jnp.sum(p_n, axis=1, keepdims=True)
        o = _dot_nt(p_c.astype(BF16), v_t.astype(BF16)) + _dot_nt(p_n.astype(BF16), vnt.astype(BF16))
        if g == 0:
            m_run, l_run, o_run = m, den, o
        else:
            m_new = jnp.maximum(m_run, m)
            ea = jnp.exp(m_run - m_new)
            eb = jnp.exp(m - m_new)
            l_run = ea * l_run + eb * den
            o_run = ea * o_run + eb * o
            m_run = m_new
        lane = lax.broadcasted_iota(jnp.int32, (HEAD_DIM, LANES), 1)
        keep = lane < LANES - n_new
        for which, (src, new) in enumerate(((k_t, knt), (v_t, vnt))):
            rolled = pltpu.roll(src, win - n_new, 1)
            if win > LANES:
                outs[g][0, which, 0, :, 0:win - LANES] = rolled[:, 0:win - LANES]
            outs[g][0, which, 0, :, win - LANES:win] = jnp.where(keep, rolled[:, win - LANES:win], new)
    ao_ref[0, 0] = o_run * (1.0 / l_run)


def _attn_sample(q_s, knt, vnt, caches):
    bsz = q_s.shape[0]
    n_new = q_s.shape[3]
    cache_spec = lambda win: pl.BlockSpec((1, 2, 1, HEAD_DIM, win), lambda b, h: (b, 0, h, 0, 0))
    return pl.pallas_call(
        _attn_sample_kernel,
        grid=(bsz, HEADS),
        in_specs=[
            pl.BlockSpec((1, 1, 3, n_new, HEAD_DIM), lambda b, h: (b, h, 0, 0, 0)),
            pl.BlockSpec((1, 1, 3, HEAD_DIM, LANES), lambda b, h: (b, h, 0, 0, 0)),
            pl.BlockSpec((1, 1, 3, HEAD_DIM, LANES), lambda b, h: (b, h, 0, 0, 0)),
        ] + [cache_spec(w) for w, _ in ATTN_GROUPS],
        out_specs=[pl.BlockSpec((1, 1, n_new, HEAD_DIM), lambda b, h: (b, h, 0, 0))]
        + [cache_spec(w) for w, _ in ATTN_GROUPS],
        out_shape=[jax.ShapeDtypeStruct((bsz, HEADS, n_new, HEAD_DIM), F32)]
        + [jax.ShapeDtypeStruct(c.shape, F32) for c in caches],
        compiler_params=_cparams(("parallel", "parallel")),
        name="attn_sample",
    )(q_s, knt, vnt, *caches)


def _ssd_sample_kernel(x_ref, b_ref, c_ref, dt_ref, sx_ref, sb_ref, sc_ref, st_in,
                       cwx, cwb, cwc, cbx, cbb, cbc, dtb_ref, alog_ref, dexp_ref, e_ref,
                       y_ref, st_out, xpad, bpad, cpad):
    n = x_ref.shape[0]
    keep = CONV_W - 1
    for pad, new, old in ((xpad, x_ref, sx_ref), (bpad, b_ref, sb_ref), (cpad, c_ref, sc_ref)):
        pad[0:HALO, :] = jnp.zeros((HALO, pad.shape[1]), F32)
        pad[HALO - keep:HALO, :] = old[0]
        pad[HALO:HALO + n, :] = new[...]
    xc = _conv_silu(xpad, cwx, cbx, n)
    bm = _conv_silu(bpad, cwb, cbb, n)
    cm = _conv_silu(cpad, cwc, cbc, n)

    dt = _softplus(dt_ref[...] + dtb_ref[...])
    a = -jnp.exp(alog_ref[...])
    dta = dt * a
    ti = lax.broadcasted_iota(jnp.int32, (n, n), 0)
    si = lax.broadcasted_iota(jnp.int32, (n, n), 1)
    tril = jnp.where(si <= ti, 1.0, 0.0).astype(F32)
    cum = jnp.dot(tril, dta, preferred_element_type=F32, precision=lax.Precision.HIGHEST)
    cum_last = cum[n - 1:n, :]
    expcum = jnp.exp(cum)
    to_end = jnp.exp(cum_last - cum) * dt

    hpg = SSD_HEADS // SSD_GROUPS
    lane = lax.broadcasted_iota(jnp.int32, (n, LANES), 1)
    trow = lax.broadcasted_iota(jnp.int32, (n, LANES), 0)
    cbs = [_dot_nt(cm[:, g * SSD_STATE:(g + 1) * SSD_STATE].astype(BF16),
                   bm[:, g * SSD_STATE:(g + 1) * SSD_STATE].astype(BF16)) for g in range(SSD_GROUPS)]
    coefs = []
    for s in range(n):
        cbh = jnp.zeros((n, LANES), F32)
        for g in range(SSD_GROUPS):
            in_g = jnp.logical_and(lane >= g * hpg, lane < (g + 1) * hpg)
            cbh = jnp.where(in_g, jnp.broadcast_to(cbs[g][:, s:s + 1], (n, LANES)), cbh)
        decay = jnp.exp(jnp.where(trow >= s, cum - cum[s:s + 1, :], -jnp.inf))
        coefs.append(cbh * decay * dt[s:s + 1, :])
    ex = _expand01(jnp.concatenate(coefs + [expcum, to_end], axis=0), e_ref[...])
    y = dexp_ref[...] * xc
    for s in range(n):
        y = y + ex[s * n:(s + 1) * n, :] * xc[s:s + 1, :]
    expcum_x = ex[n * n:n * n + n, :]
    to_end_x = ex[n * n + n:n * n + 2 * n, :]
    xs = xc * to_end_x
    dec_t = jnp.broadcast_to(expcum[n - 1:n, :], (LANES, LANES)).T
    ys = []
    for g in range(SSD_GROUPS):
        gs = slice(g * SSD_GROUP_W, (g + 1) * SSD_GROUP_W)
        ns = slice(g * SSD_STATE, (g + 1) * SSD_STATE)
        st = st_in[0, g * hpg:(g + 1) * hpg].reshape(SSD_GROUP_W, SSD_STATE)
        ys.append(_dot_nt(cm[:, ns].astype(BF16), st.astype(BF16)))
        upd = _dot_tn(xs[:, gs].astype(BF16), bm[:, ns].astype(BF16))
        for hh in range(hpg):
            h = g * hpg + hh
            rows = slice(hh * SSD_HEAD_DIM, (hh + 1) * SSD_HEAD_DIM)
            st_out[0, h] = dec_t[h:h + 1, :] * st[rows, :] + upd[rows, :]
    y_ref[...] = y + jnp.concatenate(ys, axis=1) * expcum_x


def _ssd_sample(u, dt_raw, conv_state, ssm_state, conv_w, conv_b, dt_bias, a_log, d_exp, e_mat, n_new):
    bsz = ssm_state.shape[0]
    keep = CONV_W - 1
    cwx, cwb, cwc = conv_w[:, :D_INNER], conv_w[:, D_INNER:D_INNER + SSD_GROUP_W], conv_w[:, D_INNER + SSD_GROUP_W:]
    cbx, cbb, cbc = conv_b[:, :D_INNER], conv_b[:, D_INNER:D_INNER + SSD_GROUP_W], conv_b[:, D_INNER + SSD_GROUP_W:]
    full = lambda shape: pl.BlockSpec(shape, lambda b: (0,) * len(shape))
    st_spec = pl.BlockSpec((1, SSD_HEADS, SSD_HEAD_DIM, SSD_STATE), lambda b: (b, 0, 0, 0))
    return pl.pallas_call(
        _ssd_sample_kernel,
        grid=(bsz,),
        in_specs=[
            pl.BlockSpec((n_new, D_INNER), lambda b: (b, X_OFF // D_INNER)),
            pl.BlockSpec((n_new, SSD_GROUP_W), lambda b: (b, B_OFF // SSD_GROUP_W)),
            pl.BlockSpec((n_new, SSD_GROUP_W), lambda b: (b, C_OFF // SSD_GROUP_W)),
            pl.BlockSpec((n_new, LANES), lambda b: (b, 0)),
            pl.BlockSpec((1, keep, D_INNER), lambda b: (b, 0, 0)),
            pl.BlockSpec((1, keep, SSD_GROUP_W), lambda b: (b, 0, D_INNER // SSD_GROUP_W)),
            pl.BlockSpec((1, keep, SSD_GROUP_W), lambda b: (b, 0, D_INNER // SSD_GROUP_W + 1)),
            st_spec,
            full((CONV_W, D_INNER)), full((CONV_W, SSD_GROUP_W)), full((CONV_W, SSD_GROUP_W)),
            full((1, D_INNER)), full((1, SSD_GROUP_W)), full((1, SSD_GROUP_W)),
            full((1, LANES)), full((1, LANES)), full((1, D_INNER)), full((LANES, D_INNER)),
        ],
        out_specs=[pl.BlockSpec((n_new, D_INNER), lambda b: (b, 0)), st_spec],
        out_shape=[
            jax.ShapeDtypeStruct((bsz * n_new, D_INNER), F32),
            jax.ShapeDtypeStruct(ssm_state.shape, F32),
        ],
        scratch_shapes=[
            pltpu.VMEM((HALO + n_new, D_INNER), F32),
            pltpu.VMEM((HALO + n_new, SSD_GROUP_W), F32),
            pltpu.VMEM((HALO + n_new, SSD_GROUP_W), F32),
        ],
        compiler_params=_cparams(("parallel",)),
        name="ssd_sample",
    )(u, u, u, dt_raw, conv_state, conv_state, conv_state, ssm_state,
      cwx, cwb, cwc, cbx, cbb, cbc, dt_bias, a_log, d_exp, e_mat)


def _prep_weights(w_in, dt_bias, d_skip):
    q_end = 3 * len(ATTN_GROUPS) * A_WIDTH
    z_end = q_end + D_INNER
    x_end = z_end + D_INNER
    b_end = x_end + SSD_GROUP_W
    c_end = b_end + SSD_GROUP_W
    dt_end = c_end + SSD_HEADS
    w_main = jnp.concatenate(
        [w_in[:, q_end:z_end], w_in[:, dt_end:], w_in[:, z_end:x_end], w_in[:, :q_end],
         w_in[:, x_end:b_end], w_in[:, b_end:c_end]], axis=1).astype(BF16)
    w_dt = jnp.pad(w_in[:, c_end:dt_end], ((0, 0), (0, LANES - SSD_HEADS))).astype(BF16)
    dtb = jnp.pad(dt_bias, (0, LANES - SSD_HEADS)).reshape(1, LANES)
    d_exp = jnp.repeat(d_skip, SSD_HEAD_DIM).reshape(1, D_INNER)
    return w_main, w_dt, dtb, d_exp


def _tail_kv(u, batch, seq, g, win):
    keep = min(win, seq)
    u3 = u.reshape(batch, seq, N_MAIN)
    off = QKV_OFF + 3 * g * A_WIDTH
    k = u3[:, seq - keep:, off + A_WIDTH:off + 2 * A_WIDTH]
    v = u3[:, seq - keep:, off + 2 * A_WIDTH:off + 3 * A_WIDTH]
    return jnp.stack([k, v], axis=2).reshape(batch, keep, 2, HEADS, HEAD_DIM)


def _tail_conv(u, batch, seq):
    u3 = u.reshape(batch, seq, N_MAIN)
    rows = u3[:, seq - (CONV_W - 1):]
    return jnp.concatenate([rows[..., X_OFF:X_OFF + D_INNER], rows[..., B_OFF:B_OFF + SSD_GROUP_W],
                            rows[..., C_OFF:C_OFF + SSD_GROUP_W]], axis=-1)


def kernel(x_prompt, x_sample, cache_kv_w128, cache_kv_w512, cache_kv_w2048, state_conv, state_ssm,
           pre_mix_norm, w_in, conv_w, conv_b, dt_bias, a_log, d_skip, ssd_norm, w_attn_out, w_ssd_out,
           w_out, post_mix_norm, pre_ffn_norm, w_up, w_down, post_ffn_norm):
    batch, seq, _ = x_prompt.shape
    dbatch, dseq, _ = x_sample.shape
    assert w_in.shape[0] == 1, "single-layer trunk"
    assert seq % ATTN_TILE == 0 and dseq == HALO

    w_main, w_dt, dtb, d_exp = _prep_weights(w_in[0], dt_bias[0], d_skip[0])
    alog = jnp.pad(a_log[0], (0, LANES - SSD_HEADS)).reshape(1, LANES)
    e_mat = (jnp.arange(LANES)[:, None] == (jnp.arange(D_INNER)[None, :] // SSD_HEAD_DIM)).astype(BF16)
    nw = pre_mix_norm[0].reshape(1, D_MODEL)
    cw, cb = conv_w[0], conv_b[0].reshape(1, -1)
    wa, ws, wo = w_attn_out[0].astype(BF16), w_ssd_out[0].astype(BF16), w_out[0].astype(BF16)
    wu, wd = w_up[0].astype(BF16), w_down[0].astype(BF16)
    sn = ssd_norm[0].reshape(1, D_INNER)
    pmn, pfn, qfn = (post_mix_norm[0].reshape(1, -1), pre_ffn_norm[0].reshape(1, -1),
                     post_ffn_norm[0].reshape(1, -1))

    xp = x_prompt.reshape(batch * seq, D_MODEL)
    u_p, dt_p = _inproj(xp, nw, w_main, w_dt, 1024, 512)
    a_p = _attn_prompt(u_p, batch, seq)
    y_p, ssm_p = _ssd_prompt(u_p, dt_p, cw, cb, dtb, alog, d_exp, e_mat, batch, seq)
    x1_p = _mix(a_p, y_p, u_p, xp, wa, ws, wo, sn, pmn, 256)
    out_p = _ffn(x1_p, wu, wd, pfn, qfn, 256).reshape(batch, seq, D_MODEL)
    kv_p = [_tail_kv(u_p, batch, seq, g, w)[None] for g, (w, _) in enumerate(ATTN_GROUPS)]
    conv_p = _tail_conv(u_p, batch, seq)[None]

    xs = x_sample.reshape(dbatch * dseq, D_MODEL)
    u_s, dt_s = _inproj(xs, nw, w_main, w_dt, min(1024, dbatch * dseq), 512)
    qkv = u_s[:, QKV_OFF:QKV_OFF + 9 * A_WIDTH].reshape(dbatch, dseq, 3, 3, HEADS, HEAD_DIM)
    q_s = qkv[:, :, :, 0].transpose(0, 3, 2, 1, 4)
    pad_new = lambda t: jnp.pad(t.transpose(0, 3, 2, 4, 1), ((0, 0),) * 4 + ((LANES - dseq, 0),))
    knt, vnt = pad_new(qkv[:, :, :, 1]), pad_new(qkv[:, :, :, 2])
    caches = [c[0].transpose(0, 2, 3, 4, 1) for c in (cache_kv_w128, cache_kv_w512, cache_kv_w2048)]
    ao, nc0, nc1, nc2 = _attn_sample(q_s, knt, vnt, caches)
    a_s = ao.transpose(0, 2, 1, 3).reshape(dbatch * dseq, A_WIDTH)
    kv_s = [c.transpose(0, 4, 1, 2, 3)[None] for c in (nc0, nc1, nc2)]
    y_s, ssm_s = _ssd_sample(u_s, dt_s, state_conv[0], state_ssm[0], cw, cb, dtb, alog, d_exp, e_mat, dseq)
    tm_s = min(256, dbatch * dseq)
    x1_s = _mix(a_s, y_s, u_s, xs, wa, ws, wo, sn, pmn, tm_s)
    out_s = _ffn(x1_s, wu, wd, pfn, qfn, tm_s).reshape(dbatch, dseq, D_MODEL)
    conv_s = _tail_conv(u_s, dbatch, dseq)[None]

    return (out_p, out_s, kv_p[0], kv_p[1], kv_p[2], conv_p, ssm_p[None],
            kv_s[0], kv_s[1], kv_s[2], conv_s, ssm_s[None])
```

```python
import functools

import jax
import jax.numpy as jnp
from jax import lax
from jax.experimental import pallas as pl
from jax.experimental.pallas import tpu as pltpu

F32 = jnp.float32
BF16 = jnp.bfloat16

D_MODEL = 1024
ATTN_GROUPS = ((128, 1), (512, 4), (2048, 16))
BAND = 128
HEADS = 8
HEAD_DIM = 64
A_WIDTH = HEADS * HEAD_DIM
ATTN_SCALE = HEAD_DIM ** -0.5
LOG2E = 1.4426950408889634
ATTN_TILE = 2048
BLOCK_UNROLL = 4

D_INNER = 2048
SSD_HEADS = 32
SSD_HEAD_DIM = 64
SSD_GROUPS = 4
SSD_GROUP_W = D_INNER // SSD_GROUPS
SSD_STATE = 128
CONV_W = 4
CHUNK = 128
D_FF = 2816
RMS_EPS = 1e-6
LANES = 128
HALO = 8

Z_OFF = 0
GATE_OFF = 2048
X_OFF = 4096
B_OFF = 6144
C_OFF = 6656
N_REST = 7168
N_QKV = 3 * len(ATTN_GROUPS) * A_WIDTH
N_MAIN = N_REST + N_QKV

VMEM_LIMIT = 56 * 1024 * 1024


def _cparams(sem):
    return pltpu.CompilerParams(dimension_semantics=sem, vmem_limit_bytes=VMEM_LIMIT)


def _rms(x, w):
    return x * lax.rsqrt(jnp.mean(x * x, axis=-1, keepdims=True) + RMS_EPS) * w


def _silu(x):
    h = 0.5 * x
    return h + h * jnp.tanh(h)


def _sigmoid(x):
    return 0.5 + 0.5 * jnp.tanh(0.5 * x)


def _dot(a, b):
    return jnp.dot(a, b, preferred_element_type=F32)


def _dot_nt(a, b):
    return lax.dot_general(a, b, (((1,), (1,)), ((), ())), preferred_element_type=F32)


def _dot_tn(a, b):
    return lax.dot_general(a, b, (((0,), (0,)), ((), ())), preferred_element_type=F32)


def _expand01(a, e):
    hi = a.astype(BF16)
    mid = (a - hi.astype(F32)).astype(BF16)
    return _dot(hi, e) + _dot(mid, e)


def _inproj_kernel(x_ref, nw_ref, w_ref, wdt_ref, r_ref, q_ref, dt_ref, h_ref, *, rest_tiles):
    j = pl.program_id(1)

    @pl.when(j == 0)
    def _():
        hb = _rms(x_ref[...], nw_ref[...]).astype(BF16)
        h_ref[...] = hb
        dt_ref[...] = _dot(hb, wdt_ref[...])

    @pl.when(j < rest_tiles)
    def _():
        r_ref[...] = _dot(h_ref[...], w_ref[...]).astype(BF16)

    @pl.when(j >= rest_tiles)
    def _():
        q_ref[...] = _dot(h_ref[...], w_ref[...])


def _inproj(x, nw, w_main, w_dt, tm, tn):
    t = x.shape[0]
    rest_tiles = N_REST // tn
    return pl.pallas_call(
        functools.partial(_inproj_kernel, rest_tiles=rest_tiles),
        grid=(t // tm, N_MAIN // tn),
        in_specs=[
            pl.BlockSpec((tm, D_MODEL), lambda i, j: (i, 0)),
            pl.BlockSpec((1, D_MODEL), lambda i, j: (0, 0)),
            pl.BlockSpec((D_MODEL, tn), lambda i, j: (0, j)),
            pl.BlockSpec((D_MODEL, LANES), lambda i, j: (0, 0)),
        ],
        out_specs=[
            pl.BlockSpec((tm, tn), lambda i, j: (i, jnp.minimum(j, rest_tiles - 1))),
            pl.BlockSpec((tm, tn), lambda i, j: (i, jnp.maximum(j - rest_tiles, 0))),
            pl.BlockSpec((tm, LANES), lambda i, j: (i, 0)),
        ],
        out_shape=[
            jax.ShapeDtypeStruct((t, N_REST), BF16),
            jax.ShapeDtypeStruct((t, N_QKV), F32),
            jax.ShapeDtypeStruct((t, LANES), F32),
        ],
        scratch_shapes=[pltpu.VMEM((tm, D_MODEL), BF16)],
        compiler_params=_cparams(("parallel", "arbitrary")),
        name="inproj",
    )(x, nw, w_main, w_dt)


def _unroll_for(trips):
    return max(u for u in range(1, BLOCK_UNROLL + 1) if trips % u == 0)


def _attn_prompt_kernel(q0, k0, v0, q1, k1, v1, q2, k2, v2, o_ref,
                        kd0, vt0, kd1, vt1, kd2, vt2, lse_ref, acc_ref):
    n = pl.program_id(2)
    q_refs, k_refs, v_refs = (q0, q1, q2), (k0, k1, k2), (v0, v1, v2)
    kds, vts = (kd0, kd1, kd2), (vt0, vt1, vt2)
    krow = lax.broadcasted_iota(jnp.int32, (BAND, LANES), 0)
    qcol = lax.broadcasted_iota(jnp.int32, (BAND, LANES), 1)
    upper = krow > qcol
    diag = krow == qcol
    head0 = qcol < HEAD_DIM
    half = BAND // 2

    for g, (_, dil) in enumerate(ATTN_GROUPS):
        span = BAND * dil
        nsub = ATTN_TILE // span
        nblk = nsub * dil
        q_ref, k_ref, v_ref, kd, vt = q_refs[g], k_refs[g], v_refs[g], kds[g], vts[g]

        @pl.when(n == 0)
        def _():
            for r in range(dil):
                kd[r * (nsub + 1)] = jnp.zeros((BAND, LANES), BF16)
                vt[r * (nsub + 1)] = jnp.zeros((LANES, BAND), BF16)

        @pl.when(n > 0)
        def _():
            for r in range(dil):
                kd[r * (nsub + 1)] = kd[r * (nsub + 1) + nsub]
                vt[r * (nsub + 1)] = vt[r * (nsub + 1) + nsub]

        def split(i):
            if dil == 1:
                return i, 0
            j = i // dil
            return j, i - j * dil

        def rows_of(i):
            j, r = split(i)
            if dil == 1:
                return pl.ds(pl.multiple_of(i * BAND, BAND), BAND)
            return pl.ds(j * span + r, BAND, stride=dil)

        def deinterleave(i, c):
            j, r = split(i)
            slot = r * (nsub + 1) + j + 1
            rows = rows_of(i)
            kd[slot] = k_ref[rows, :].astype(BF16)
            vt[slot] = v_ref[rows, :].T.astype(BF16)
            return c

        lax.fori_loop(0, nblk, deinterleave, 0)

        def block(i, c, first):
            j, r = split(i)
            slot = r * (nsub + 1) + j
            rows = rows_of(i)
            q = q_ref[rows, :] * (ATTN_SCALE * LOG2E)
            q2 = jnp.concatenate([jnp.where(head0, q, 0.0), jnp.where(head0, 0.0, q)], axis=0).astype(BF16)
            kk = kd[pl.ds(slot, 2)].reshape(2 * BAND, LANES)
            st = _dot_nt(kk, q2)
            vprev = vt[slot]
            vcur = vt[slot + 1]
            ots, lts = [], []
            for hh in range(2):
                sp = st[0:BAND, hh * BAND:(hh + 1) * BAND]
                sc = st[BAND:2 * BAND, hh * BAND:(hh + 1) * BAND]
                if first:
                    sp = jnp.where(n > 0, sp, -jnp.inf)
                comb = jnp.where(upper, sp, sc)
                dg = jnp.sum(jnp.where(diag, sp, 0.0), axis=0, keepdims=True)
                m = jnp.maximum(jnp.max(comb, axis=0, keepdims=True), dg)
                p = jnp.exp2(comb - m)
                pd = jnp.exp2(dg - m)
                den = jnp.sum(p, axis=0, keepdims=True) + pd
                pprev = jnp.where(upper, p, jnp.where(diag, pd, 0.0)).astype(BF16)
                pcur = jnp.where(upper, 0.0, p).astype(BF16)
                pt = jnp.concatenate([pprev, pcur], axis=0)
                vsl = slice(hh * half, (hh + 1) * half)
                vtt = jnp.concatenate([vprev[vsl, :], vcur[vsl, :]], axis=1)
                ot = _dot(vtt, pt) * (1.0 / den)
                ots.append(ot)
                lts.append(jnp.broadcast_to(m + jnp.log2(den), (half, BAND)))
            o_blk = jnp.concatenate(ots, axis=0).T
            lse_blk = jnp.concatenate(lts, axis=0).T
            if g == 0:
                lse_ref[rows, :] = lse_blk
                acc_ref[rows, :] = o_blk
            else:
                lse_run = lse_ref[rows, :]
                acc_run = acc_ref[rows, :]
                mx = jnp.maximum(lse_run, lse_blk)
                ea = jnp.exp2(lse_run - mx)
                eb = jnp.exp2(lse_blk - mx)
                tot = ea + eb
                inv = 1.0 / tot
                merged = (ea * inv) * acc_run + (eb * inv) * o_blk
                if g == len(ATTN_GROUPS) - 1:
                    o_ref[rows, :] = merged
                else:
                    lse_ref[rows, :] = mx + jnp.log2(tot)
                    acc_ref[rows, :] = merged
            return c

        lax.fori_loop(0, dil, functools.partial(block, first=True), 0, unroll=_unroll_for(dil))
        if nblk > dil:
            lax.fori_loop(dil, nblk, functools.partial(block, first=False), 0, unroll=_unroll_for(nblk - dil))


def _attn_prompt(u, batch, seq):
    nt = seq // ATTN_TILE
    hp = A_WIDTH // LANES
    qkv_blk = 0

    def spec(g, which):
        col = qkv_blk + (3 * g + which) * hp
        return pl.BlockSpec((ATTN_TILE, LANES), lambda b, h, n, col=col: (b * nt + n, col + h))

    in_specs = [spec(g, w) for g in range(3) for w in range(3)]
    scratch = []
    for _, dil in ATTN_GROUPS:
        nslots = dil * (ATTN_TILE // (BAND * dil) + 1)
        scratch += [pltpu.VMEM((nslots, BAND, LANES), BF16), pltpu.VMEM((nslots, LANES, BAND), BF16)]
    scratch += [pltpu.VMEM((ATTN_TILE, LANES), F32), pltpu.VMEM((ATTN_TILE, LANES), F32)]
    return pl.pallas_call(
        _attn_prompt_kernel,
        grid=(batch, hp, nt),
        in_specs=in_specs,
        out_specs=pl.BlockSpec((ATTN_TILE, LANES), lambda b, h, n: (b * nt + n, h)),
        out_shape=jax.ShapeDtypeStruct((batch * seq, A_WIDTH), F32),
        scratch_shapes=scratch,
        compiler_params=_cparams(("parallel", "parallel", "arbitrary")),
        name="attn_prompt",
    )(*([u] * 9))


def _conv_silu(pad_ref, w_ref, b_ref, rows):
    acc = b_ref[...] + pad_ref[HALO - (CONV_W - 1):HALO - (CONV_W - 1) + rows, :] * w_ref[0:1, :]
    for i in range(1, CONV_W):
        off = HALO - (CONV_W - 1) + i
        acc = acc + pad_ref[off:off + rows, :] * w_ref[i:i + 1, :]
    return _silu(acc)


def _conv_silu_chunk(x_ref, tail_ref, w_ref, b_ref):
    x = x_ref[...].astype(F32)
    ext = jnp.concatenate([tail_ref[...], x], axis=0)
    tail_ref[...] = x[x.shape[0] - HALO:, :]
    acc = b_ref[...] + x * w_ref[CONV_W - 1:CONV_W, :]
    for k in range(1, CONV_W):
        acc = acc + pltpu.roll(ext, k, 0)[HALO:, :] * w_ref[CONV_W - 1 - k:CONV_W - k, :]
    return _silu(acc)


def _softplus(x):
    return jnp.maximum(x, 0.0) + jnp.log1p(jnp.exp(-jnp.abs(x)))


def _ssd_prompt_kernel(x_ref, b_ref, c_ref, dt_ref, cwx, cwb, cwc, cbx, cbb, cbc,
                       dtb_ref, alog_ref, dexp_ref, e_ref, y_ref, st_ref,
                       xtail, btail, ctail, state):
    c = pl.program_id(1)
    nc = pl.num_programs(1)

    @pl.when(c == 0)
    def _():
        xtail[...] = jnp.zeros(xtail.shape, F32)
        btail[...] = jnp.zeros(btail.shape, F32)
        ctail[...] = jnp.zeros(ctail.shape, F32)
        state[...] = jnp.zeros(state.shape, F32)

    xc = _conv_silu_chunk(x_ref, xtail, cwx, cbx)
    bm = _conv_silu_chunk(b_ref, btail, cwb, cbb)
    cm = _conv_silu_chunk(c_ref, ctail, cwc, cbc)

    dt = _softplus(dt_ref[...] + dtb_ref[...])
    a = -jnp.exp(alog_ref[...])
    dta = dt * a
    ti = lax.broadcasted_iota(jnp.int32, (CHUNK, CHUNK), 0)
    si = lax.broadcasted_iota(jnp.int32, (CHUNK, CHUNK), 1)
    causal = si <= ti
    tril = jnp.where(causal, 1.0, 0.0).astype(F32)
    cum = jnp.dot(tril, dta, preferred_element_type=F32, precision=lax.Precision.HIGHEST)
    cum2 = cum * LOG2E
    cum2_t = cum2.T
    cum2_last = cum2[CHUNK - 1:CHUNK, :]
    expcum = jnp.exp2(cum2)
    to_end = jnp.exp2(cum2_last - cum2) * dt
    ex = _expand01(jnp.concatenate([expcum, to_end, dt], axis=0), e_ref[...])
    expcum_x = ex[0:CHUNK, :]
    to_end_x = ex[CHUNK:2 * CHUNK, :]
    dt_x = ex[2 * CHUNK:3 * CHUNK, :]
    dec_x = expcum_x[CHUNK - 1:CHUNK, :]

    lane = lax.broadcasted_iota(jnp.int32, (CHUNK, LANES), 1)
    head0 = lane < SSD_HEAD_DIM
    hpg = SSD_HEADS // SSD_GROUPS
    for g in range(SSD_GROUPS):
        gs = slice(g * SSD_GROUP_W, (g + 1) * SSD_GROUP_W)
        ns = slice(g * SSD_STATE, (g + 1) * SSD_STATE)
        bg = bm[:, ns].astype(BF16)
        cg = cm[:, ns].astype(BF16)
        cb = _dot_nt(cg, bg)
        xg = xc[:, gs]
        xdt = xg * dt_x[:, gs]
        ys = []
        for pr in range(hpg // 2):
            ws = []
            for hh in range(2):
                h = g * hpg + 2 * pr + hh
                colb = jnp.broadcast_to(cum2[:, h:h + 1], (CHUNK, CHUNK))
                rowb = jnp.broadcast_to(cum2_t[h:h + 1, :], (CHUNK, CHUNK))
                decay = jnp.exp2(jnp.where(causal, colb - rowb, -jnp.inf))
                ws.append((cb * decay).astype(BF16))
            w2 = jnp.concatenate(ws, axis=1)
            xp = xdt[:, pr * LANES:(pr + 1) * LANES]
            x2 = jnp.concatenate([jnp.where(head0, xp, 0.0), jnp.where(head0, 0.0, xp)], axis=0).astype(BF16)
            ys.append(_dot(w2, x2))
        st = state[g]
        y_state = _dot(cg, st.astype(BF16))
        y_ref[:, gs] = jnp.concatenate(ys, axis=1) + y_state * expcum_x[:, gs] + dexp_ref[:, gs] * xg
        xs = (xg * to_end_x[:, gs]).astype(BF16)
        state[g] = dec_x[:, gs] * st + _dot_tn(bg, xs)

    @pl.when(c == nc - 1)
    def _():
        for g in range(SSD_GROUPS):
            st_ref[0, g * hpg:(g + 1) * hpg, :, :] = state[g].T.reshape(hpg, SSD_HEAD_DIM, SSD_STATE)


def _ssd_prompt(u, dt_raw, conv_w, conv_b, dt_bias, a_log, d_exp, e_mat, batch, seq):
    nc = seq // CHUNK
    cwx, cwb, cwc = conv_w[:, :D_INNER], conv_w[:, D_INNER:D_INNER + SSD_GROUP_W], conv_w[:, D_INNER + SSD_GROUP_W:]
    cbx, cbb, cbc = conv_b[:, :D_INNER], conv_b[:, D_INNER:D_INNER + SSD_GROUP_W], conv_b[:, D_INNER + SSD_GROUP_W:]
    full = lambda shape: pl.BlockSpec(shape, lambda b, c: (0,) * len(shape))
    return pl.pallas_call(
        _ssd_prompt_kernel,
        grid=(batch, nc),
        in_specs=[
            pl.BlockSpec((CHUNK, D_INNER), lambda b, c: (b * nc + c, X_OFF // D_INNER)),
            pl.BlockSpec((CHUNK, SSD_GROUP_W), lambda b, c: (b * nc + c, B_OFF // SSD_GROUP_W)),
            pl.BlockSpec((CHUNK, SSD_GROUP_W), lambda b, c: (b * nc + c, C_OFF // SSD_GROUP_W)),
            pl.BlockSpec((CHUNK, LANES), lambda b, c: (b * nc + c, 0)),
            full((CONV_W, D_INNER)), full((CONV_W, SSD_GROUP_W)), full((CONV_W, SSD_GROUP_W)),
            full((1, D_INNER)), full((1, SSD_GROUP_W)), full((1, SSD_GROUP_W)),
            full((1, LANES)), full((1, LANES)), full((1, D_INNER)), full((LANES, D_INNER)),
        ],
        out_specs=[
            pl.BlockSpec((CHUNK, D_INNER), lambda b, c: (b * nc + c, 0)),
            pl.BlockSpec((1, SSD_HEADS, SSD_HEAD_DIM, SSD_STATE), lambda b, c: (b, 0, 0, 0)),
        ],
        out_shape=[
            jax.ShapeDtypeStruct((batch * seq, D_INNER), F32),
            jax.ShapeDtypeStruct((batch, SSD_HEADS, SSD_HEAD_DIM, SSD_STATE), F32),
        ],
        scratch_shapes=[
            pltpu.VMEM((HALO, D_INNER), F32),
            pltpu.VMEM((HALO, SSD_GROUP_W), F32),
            pltpu.VMEM((HALO, SSD_GROUP_W), F32),
            pltpu.VMEM((SSD_GROUPS, SSD_STATE, SSD_GROUP_W), F32),
        ],
        compiler_params=_cparams(("parallel", "arbitrary")),
        name="ssd_prompt",
    )(u, u, u, dt_raw, cwx, cwb, cwc, cbx, cbb, cbc, dt_bias, a_log, d_exp, e_mat)


def _mix_kernel(a_ref, y_ref, z_ref, ga_ref, gb_ref, x_ref, wa_ref, ws_ref, wo_ref,
                sn_ref, pn_ref, o_ref):
    a_proj = _dot(a_ref[...].astype(BF16), wa_ref[...])
    g = y_ref[...] * _silu(z_ref[...].astype(F32))
    parts = []
    for i in range(SSD_GROUPS):
        gi = g[:, i * SSD_GROUP_W:(i + 1) * SSD_GROUP_W]
        parts.append(gi * lax.rsqrt(jnp.mean(gi * gi, axis=-1, keepdims=True) + RMS_EPS))
    yn = (jnp.concatenate(parts, axis=1) * sn_ref[...]).astype(BF16)
    b_proj = _dot(yn, ws_ref[...])
    mixed_in = _sigmoid(ga_ref[...].astype(F32)) * a_proj + _sigmoid(gb_ref[...].astype(F32)) * b_proj
    mixed = _dot(mixed_in.astype(BF16), wo_ref[...])
    o_ref[...] = x_ref[...] + _rms(mixed, pn_ref[...])


def _mix(a_out, y, u, x, w_attn_out, w_ssd_out, w_out, ssd_norm, post_mix_norm, tm):
    t = x.shape[0]
    const = lambda shape: pl.BlockSpec(shape, lambda i: (0, 0), pipeline_mode=pl.Buffered(1))
    return pl.pallas_call(
        _mix_kernel,
        grid=(t // tm,),
        in_specs=[
            pl.BlockSpec((tm, A_WIDTH), lambda i: (i, 0)),
            pl.BlockSpec((tm, D_INNER), lambda i: (i, 0)),
            pl.BlockSpec((tm, D_INNER), lambda i: (i, Z_OFF // D_INNER)),
            pl.BlockSpec((tm, D_MODEL), lambda i: (i, GATE_OFF // D_MODEL)),
            pl.BlockSpec((tm, D_MODEL), lambda i: (i, GATE_OFF // D_MODEL + 1)),
            pl.BlockSpec((tm, D_MODEL), lambda i: (i, 0)),
            const((A_WIDTH, D_MODEL)), const((D_INNER, D_MODEL)), const((D_MODEL, D_MODEL)),
            const((1, D_INNER)), const((1, D_MODEL)),
        ],
        out_specs=pl.BlockSpec((tm, D_MODEL), lambda i: (i, 0)),
        out_shape=jax.ShapeDtypeStruct((t, D_MODEL), F32),
        compiler_params=_cparams(("parallel",)),
        name="mix",
    )(a_out, y, u, u, u, x, w_attn_out, w_ssd_out, w_out, ssd_norm, post_mix_norm)


def _ffn_kernel(x_ref, wu_ref, wd_ref, n1_ref, n2_ref, o_ref):
    x = x_ref[...]
    h = _rms(x, n1_ref[...]).astype(BF16)
    gu = _dot(h, wu_ref[...])
    act = (_silu(gu[:, :D_FF]) * gu[:, D_FF:]).astype(BF16)
    f = _dot(act, wd_ref[...])
    o_ref[...] = x + _rms(f, n2_ref[...])


def _ffn(x, w_up, w_down, pre_norm, post_norm, tm):
    t = x.shape[0]
    const = lambda shape: pl.BlockSpec(shape, lambda i: (0, 0), pipeline_mode=pl.Buffered(1))
    return pl.pallas_call(
        _ffn_kernel,
        grid=(t // tm,),
        in_specs=[
            pl.BlockSpec((tm, D_MODEL), lambda i: (i, 0)),
            const((D_MODEL, 2 * D_FF)), const((D_FF, D_MODEL)),
            const((1, D_MODEL)), const((1, D_MODEL)),
        ],
        out_specs=pl.BlockSpec((tm, D_MODEL), lambda i: (i, 0)),
        out_shape=jax.ShapeDtypeStruct((t, D_MODEL), F32),
        compiler_params=_cparams(("parallel",)),
        name="ffn",
    )(x, w_up, w_down, pre_norm, post_norm)


def _attn_sample_kernel(q_ref, new_ref, c0, c1, c2, ao_ref, o0, o1, o2):
    n_new = q_ref.shape[3]
    new_t = new_ref[0, 0]
    caches, outs = (c0, c1, c2), (o0, o1, o2)
    m_run = l_run = o_run = None
    for g, (win, dil) in enumerate(ATTN_GROUPS):
        q = (q_ref[0, 0, g] * ATTN_SCALE).astype(BF16)
        k_t = caches[g][0, 0, 0]
        v_t = caches[g][0, 1, 0]
        knt = pltpu.roll(new_t, LANES - n_new - (2 * g) * n_new, 1)
        vnt = pltpu.roll(new_t, LANES - n_new - (2 * g + 1) * n_new, 1)
        qi = lax.broadcasted_iota(jnp.int32, (n_new, win), 0)
        ci = lax.broadcasted_iota(jnp.int32, (n_new, win), 1)
        ok_c = jnp.logical_and(ci >= qi, ((ci - qi) & (dil - 1)) == 0)
        qn = lax.broadcasted_iota(jnp.int32, (n_new, LANES), 0)
        nn = lax.broadcasted_iota(jnp.int32, (n_new, LANES), 1) - (LANES - n_new)
        ok_n = jnp.logical_and(jnp.logical_and(nn >= 0, nn <= qn), ((qn - nn) & (dil - 1)) == 0)
        s_c = jnp.where(ok_c, _dot(q, k_t.astype(BF16)), -jnp.inf)
        s_n = jnp.where(ok_n, _dot(q, knt.astype(BF16)), -jnp.inf)
        m = jnp.maximum(jnp.max(s_c, axis=1, keepdims=True), jnp.max(s_n, axis=1, keepdims=True))
        p_c = jnp.exp(s_c - m)
        p_n = jnp.exp(s_n - m)
        den = jnp.sum(p_c, axis=1, keepdims=True) + jnp.sum(p_n, axis=1, keepdims=True)
        o = _dot_nt(p_c.astype(BF16), v_t.astype(BF16)) + _dot_nt(p_n.astype(BF16), vnt.astype(BF16))
        if g == 0:
            m_run, l_run, o_run = m, den, o
        else:
            m_new = jnp.maximum(m_run, m)
            ea = jnp.exp(m_run - m_new)
            eb = jnp.exp(m - m_new)
            l_run = ea * l_run + eb * den
            o_run = ea * o_run + eb * o
            m_run = m_new
        lane = lax.broadcasted_iota(jnp.int32, (HEAD_DIM, LANES), 1)
        keep = lane < LANES - n_new
        for which, (src, new) in enumerate(((k_t, knt), (v_t, vnt))):
            rolled = pltpu.roll(src, win - n_new, 1)
            if win > LANES:
                outs[g][0, which, 0, :, 0:win - LANES] = rolled[:, 0:win - LANES]
            outs[g][0, which, 0, :, win - LANES:win] = jnp.where(keep, rolled[:, win - LANES:win], new)
    ao_ref[0, 0] = o_run * (1.0 / l_run)


def _attn_sample(q_s, new_t, caches):
    bsz = q_s.shape[0]
    n_new = q_s.shape[3]
    cache_spec = lambda win: pl.BlockSpec((1, 2, 1, HEAD_DIM, win), lambda b, h: (b, 0, h, 0, 0))
    return pl.pallas_call(
        _attn_sample_kernel,
        grid=(bsz, HEADS),
        in_specs=[
            pl.BlockSpec((1, 1, 3, n_new, HEAD_DIM), lambda b, h: (b, h, 0, 0, 0)),
            pl.BlockSpec((1, 1, HEAD_DIM, LANES), lambda b, h: (b, h, 0, 0)),
        ] + [cache_spec(w) for w, _ in ATTN_GROUPS],
        out_specs=[pl.BlockSpec((1, 1, n_new, HEAD_DIM), lambda b, h: (b, h, 0, 0))]
        + [cache_spec(w) for w, _ in ATTN_GROUPS],
        out_shape=[jax.ShapeDtypeStruct((bsz, HEADS, n_new, HEAD_DIM), F32)]
        + [jax.ShapeDtypeStruct(c.shape, F32) for c in caches],
        compiler_params=_cparams(("parallel", "parallel")),
        name="attn_sample",
    )(q_s, new_t, *caches)


def _ssd_sample_kernel(x_ref, b_ref, c_ref, dt_ref, sx_ref, sb_ref, sc_ref, st_in,
                       cwx, cwb, cwc, cbx, cbb, cbc, dtb_ref, alog_ref, dexp_ref, e_ref,
                       y_ref, st_out, xpad, bpad, cpad):
    n = x_ref.shape[0]
    keep = CONV_W - 1
    for pad, new, old in ((xpad, x_ref, sx_ref), (bpad, b_ref, sb_ref), (cpad, c_ref, sc_ref)):
        pad[0:HALO, :] = jnp.zeros((HALO, pad.shape[1]), F32)
        pad[HALO - keep:HALO, :] = old[0]
        pad[HALO:HALO + n, :] = new[...]
    xc = _conv_silu(xpad, cwx, cbx, n)
    bm = _conv_silu(bpad, cwb, cbb, n)
    cm = _conv_silu(cpad, cwc, cbc, n)

    dt = _softplus(dt_ref[...] + dtb_ref[...])
    a = -jnp.exp(alog_ref[...])
    dta = dt * a
    ti = lax.broadcasted_iota(jnp.int32, (n, n), 0)
    si = lax.broadcasted_iota(jnp.int32, (n, n), 1)
    tril = jnp.where(si <= ti, 1.0, 0.0).astype(F32)
    cum = jnp.dot(tril, dta, preferred_element_type=F32, precision=lax.Precision.HIGHEST)
    cum_last = cum[n - 1:n, :]
    expcum = jnp.exp(cum)
    to_end = jnp.exp(cum_last - cum) * dt

    hpg = SSD_HEADS // SSD_GROUPS
    lane = lax.broadcasted_iota(jnp.int32, (n, LANES), 1)
    trow = lax.broadcasted_iota(jnp.int32, (n, LANES), 0)
    cbs = [_dot_nt(cm[:, g * SSD_STATE:(g + 1) * SSD_STATE].astype(BF16),
                   bm[:, g * SSD_STATE:(g + 1) * SSD_STATE].astype(BF16)) for g in range(SSD_GROUPS)]
    coefs = []
    for s in range(n):
        cbh = jnp.zeros((n, LANES), F32)
        for g in range(SSD_GROUPS):
            in_g = jnp.logical_and(lane >= g * hpg, lane < (g + 1) * hpg)
            cbh = jnp.where(in_g, jnp.broadcast_to(cbs[g][:, s:s + 1], (n, LANES)), cbh)
        decay = jnp.exp(jnp.where(trow >= s, cum - cum[s:s + 1, :], -jnp.inf))
        coefs.append(cbh * decay * dt[s:s + 1, :])
    ex = _expand01(jnp.concatenate(coefs + [expcum, to_end], axis=0), e_ref[...])
    y = dexp_ref[...] * xc
    for s in range(n):
        y = y + ex[s * n:(s + 1) * n, :] * xc[s:s + 1, :]
    expcum_x = ex[n * n:n * n + n, :]
    to_end_x = ex[n * n + n:n * n + 2 * n, :]
    xs = xc * to_end_x
    dec_t = jnp.broadcast_to(expcum[n - 1:n, :], (LANES, LANES)).T
    ys = []
    for g in range(SSD_GROUPS):
        gs = slice(g * SSD_GROUP_W, (g + 1) * SSD_GROUP_W)
        ns = slice(g * SSD_STATE, (g + 1) * SSD_STATE)
        st = st_in[0, g * hpg:(g + 1) * hpg].reshape(SSD_GROUP_W, SSD_STATE)
        ys.append(_dot_nt(cm[:, ns].astype(BF16), st.astype(BF16)))
        upd = _dot_tn(xs[:, gs].astype(BF16), bm[:, ns].astype(BF16))
        for hh in range(hpg):
            h = g * hpg + hh
            rows = slice(hh * SSD_HEAD_DIM, (hh + 1) * SSD_HEAD_DIM)
            st_out[0, h] = dec_t[h:h + 1, :] * st[rows, :] + upd[rows, :]
    y_ref[...] = y + jnp.concatenate(ys, axis=1) * expcum_x


def _ssd_sample(u, dt_raw, conv_state, ssm_state, conv_w, conv_b, dt_bias, a_log, d_exp, e_mat, n_new):
    bsz = ssm_state.shape[0]
    keep = CONV_W - 1
    cwx, cwb, cwc = conv_w[:, :D_INNER], conv_w[:, D_INNER:D_INNER + SSD_GROUP_W], conv_w[:, D_INNER + SSD_GROUP_W:]
    cbx, cbb, cbc = conv_b[:, :D_INNER], conv_b[:, D_INNER:D_INNER + SSD_GROUP_W], conv_b[:, D_INNER + SSD_GROUP_W:]
    full = lambda shape: pl.BlockSpec(shape, lambda b: (0,) * len(shape))
    st_spec = pl.BlockSpec((1, SSD_HEADS, SSD_HEAD_DIM, SSD_STATE), lambda b: (b, 0, 0, 0))
    return pl.pallas_call(
        _ssd_sample_kernel,
        grid=(bsz,),
        in_specs=[
            pl.BlockSpec((n_new, D_INNER), lambda b: (b, X_OFF // D_INNER)),
            pl.BlockSpec((n_new, SSD_GROUP_W), lambda b: (b, B_OFF // SSD_GROUP_W)),
            pl.BlockSpec((n_new, SSD_GROUP_W), lambda b: (b, C_OFF // SSD_GROUP_W)),
            pl.BlockSpec((n_new, LANES), lambda b: (b, 0)),
            pl.BlockSpec((1, keep, D_INNER), lambda b: (b, 0, 0)),
            pl.BlockSpec((1, keep, SSD_GROUP_W), lambda b: (b, 0, D_INNER // SSD_GROUP_W)),
            pl.BlockSpec((1, keep, SSD_GROUP_W), lambda b: (b, 0, D_INNER // SSD_GROUP_W + 1)),
            st_spec,
            full((CONV_W, D_INNER)), full((CONV_W, SSD_GROUP_W)), full((CONV_W, SSD_GROUP_W)),
            full((1, D_INNER)), full((1, SSD_GROUP_W)), full((1, SSD_GROUP_W)),
            full((1, LANES)), full((1, LANES)), full((1, D_INNER)), full((LANES, D_INNER)),
        ],
        out_specs=[pl.BlockSpec((n_new, D_INNER), lambda b: (b, 0)), st_spec],
        out_shape=[
            jax.ShapeDtypeStruct((bsz * n_new, D_INNER), F32),
            jax.ShapeDtypeStruct(ssm_state.shape, F32),
        ],
        scratch_shapes=[
            pltpu.VMEM((HALO + n_new, D_INNER), F32),
            pltpu.VMEM((HALO + n_new, SSD_GROUP_W), F32),
            pltpu.VMEM((HALO + n_new, SSD_GROUP_W), F32),
        ],
        compiler_params=_cparams(("parallel",)),
        name="ssd_sample",
    )(u, u, u, dt_raw, conv_state, conv_state, conv_state, ssm_state,
      cwx, cwb, cwc, cbx, cbb, cbc, dt_bias, a_log, d_exp, e_mat)


def _prep_weights(w_in, dt_bias, d_skip):
    q_end = 3 * len(ATTN_GROUPS) * A_WIDTH
    z_end = q_end + D_INNER
    x_end = z_end + D_INNER
    b_end = x_end + SSD_GROUP_W
    c_end = b_end + SSD_GROUP_W
    dt_end = c_end + SSD_HEADS
    w_main = jnp.concatenate(
        [w_in[:, q_end:z_end], w_in[:, dt_end:], w_in[:, z_end:c_end], w_in[:, :q_end]], axis=1).astype(BF16)
    w_dt = jnp.pad(w_in[:, c_end:dt_end], ((0, 0), (0, LANES - SSD_HEADS))).astype(BF16)
    dtb = jnp.pad(dt_bias, (0, LANES - SSD_HEADS)).reshape(1, LANES)
    d_exp = jnp.repeat(d_skip, SSD_HEAD_DIM).reshape(1, D_INNER)
    return w_main, w_dt, dtb, d_exp


def _tail_kv(u, batch, seq, g, win):
    keep = min(win, seq)
    u3 = u.reshape(batch, seq, N_QKV)
    off = 3 * g * A_WIDTH
    k = u3[:, seq - keep:, off + A_WIDTH:off + 2 * A_WIDTH]
    v = u3[:, seq - keep:, off + 2 * A_WIDTH:off + 3 * A_WIDTH]
    return jnp.stack([k, v], axis=2).reshape(batch, keep, 2, HEADS, HEAD_DIM)


def _tail_conv(u, batch, seq):
    u3 = u.reshape(batch, seq, N_REST)
    return u3[:, seq - (CONV_W - 1):, X_OFF:C_OFF + SSD_GROUP_W].astype(F32)


def kernel(x_prompt, x_sample, cache_kv_w128, cache_kv_w512, cache_kv_w2048, state_conv, state_ssm,
           pre_mix_norm, w_in, conv_w, conv_b, dt_bias, a_log, d_skip, ssd_norm, w_attn_out, w_ssd_out,
           w_out, post_mix_norm, pre_ffn_norm, w_up, w_down, post_ffn_norm):
    batch, seq, _ = x_prompt.shape
    dbatch, dseq, _ = x_sample.shape
    assert w_in.shape[0] == 1, "single-layer trunk"
    assert seq % ATTN_TILE == 0 and dseq == HALO

    w_main, w_dt, dtb, d_exp = _prep_weights(w_in[0], dt_bias[0], d_skip[0])
    alog = jnp.pad(a_log[0], (0, LANES - SSD_HEADS)).reshape(1, LANES)
    e_mat = (jnp.arange(LANES)[:, None] == (jnp.arange(D_INNER)[None, :] // SSD_HEAD_DIM)).astype(BF16)
    nw = pre_mix_norm[0].reshape(1, D_MODEL)
    cw, cb = conv_w[0], conv_b[0].reshape(1, -1)
    wa, ws, wo = w_attn_out[0].astype(BF16), w_ssd_out[0].astype(BF16), w_out[0].astype(BF16)
    wu, wd = w_up[0].astype(BF16), w_down[0].astype(BF16)
    sn = ssd_norm[0].reshape(1, D_INNER)
    pmn, pfn, qfn = (post_mix_norm[0].reshape(1, -1), pre_ffn_norm[0].reshape(1, -1),
                     post_ffn_norm[0].reshape(1, -1))

    xp = x_prompt.reshape(batch * seq, D_MODEL)
    ur_p, uq_p, dt_p = _inproj(xp, nw, w_main, w_dt, min(2048, batch * seq), 512)
    a_p = _attn_prompt(uq_p, batch, seq)
    y_p, ssm_p = _ssd_prompt(ur_p, dt_p, cw, cb, dtb, alog, d_exp, e_mat, batch, seq)
    x1_p = _mix(a_p, y_p, ur_p, xp, wa, ws, wo, sn, pmn, 256)
    out_p = _ffn(x1_p, wu, wd, pfn, qfn, 256).reshape(batch, seq, D_MODEL)
    kv_p = [_tail_kv(uq_p, batch, seq, g, w)[None] for g, (w, _) in enumerate(ATTN_GROUPS)]
    conv_p = _tail_conv(ur_p, batch, seq)[None]

    xs = x_sample.reshape(dbatch * dseq, D_MODEL)
    ur_s, uq_s, dt_s = _inproj(xs, nw, w_main, w_dt, min(1024, dbatch * dseq), 512)
    qkv = uq_s.reshape(dbatch, dseq, 3, 3, HEADS, HEAD_DIM)
    q_s = qkv[:, :, :, 0].transpose(0, 3, 2, 1, 4)
    new_t = qkv[:, :, :, 1:3].transpose(0, 4, 5, 2, 3, 1).reshape(dbatch, HEADS, HEAD_DIM, 6 * dseq)
    new_t = jnp.pad(new_t, ((0, 0), (0, 0), (0, 0), (0, LANES - 6 * dseq)))
    caches = [c[0].transpose(0, 2, 3, 4, 1) for c in (cache_kv_w128, cache_kv_w512, cache_kv_w2048)]
    ao, nc0, nc1, nc2 = _attn_sample(q_s, new_t, caches)
    a_s = ao.transpose(0, 2, 1, 3).reshape(dbatch * dseq, A_WIDTH)
    kv_s = [c.transpose(0, 4, 1, 2, 3)[None] for c in (nc0, nc1, nc2)]
    y_s, ssm_s = _ssd_sample(ur_s.astype(F32), dt_s, state_conv[0], state_ssm[0], cw, cb, dtb, alog, d_exp,
                             e_mat, dseq)
    tm_s = min(256, dbatch * dseq)
    x1_s = _mix(a_s, y_s, ur_s, xs, wa, ws, wo, sn, pmn, tm_s)
    out_s = _ffn(x1_s, wu, wd, pfn, qfn, tm_s).reshape(dbatch, dseq, D_MODEL)
    conv_s = _tail_conv(ur_s, dbatch, dseq)[None]

    return (out_p, out_s, kv_p[0], kv_p[1], kv_p[2], conv_p, ssm_p[None],
            kv_s[0], kv_s[1], kv_s[2], conv_s, ssm_s[None])
```

```python
import functools

import jax
import jax.numpy as jnp
from jax import lax
from jax.experimental import pallas as pl
from jax.experimental.pallas import tpu as pltpu

F32 = jnp.float32
BF16 = jnp.bfloat16

D_MODEL = 1024
ATTN_GROUPS = ((128, 1), (512, 4), (2048, 16))
BAND = 128
HEADS = 8
HEAD_DIM = 64
A_WIDTH = HEADS * HEAD_DIM
ATTN_SCALE = HEAD_DIM ** -0.5
LOG2E = 1.4426950408889634
ATTN_TILE = 2048
BLOCK_UNROLL = 4

D_INNER = 2048
SSD_HEADS = 32
SSD_HEAD_DIM = 64
SSD_GROUPS = 4
SSD_GROUP_W = D_INNER // SSD_GROUPS
SSD_STATE = 128
CONV_W = 4
CHUNK = 128
D_FF = 2816
RMS_EPS = 1e-6
LANES = 128
HALO = 8

Z_OFF = 0
GATE_OFF = 2048
X_OFF = 4096
B_OFF = 6144
C_OFF = 6656
N_REST = 7168
N_QKV = 3 * len(ATTN_GROUPS) * A_WIDTH
N_MAIN = N_REST + N_QKV

VMEM_LIMIT = 56 * 1024 * 1024


def _cparams(sem):
    return pltpu.CompilerParams(dimension_semantics=sem, vmem_limit_bytes=VMEM_LIMIT)


def _rms(x, w):
    return x * lax.rsqrt(jnp.mean(x * x, axis=-1, keepdims=True) + RMS_EPS) * w


def _silu(x):
    h = 0.5 * x
    return h + h * jnp.tanh(h)


def _sigmoid(x):
    return 0.5 + 0.5 * jnp.tanh(0.5 * x)


def _dot(a, b):
    return jnp.dot(a, b, preferred_element_type=F32)


def _dot_nt(a, b):
    return lax.dot_general(a, b, (((1,), (1,)), ((), ())), preferred_element_type=F32)


def _dot_tn(a, b):
    return lax.dot_general(a, b, (((0,), (0,)), ((), ())), preferred_element_type=F32)


def _expand01(a, e):
    hi = a.astype(BF16)
    mid = (a - hi.astype(F32)).astype(BF16)
    return _dot(hi, e) + _dot(mid, e)


def _inproj_kernel(x_ref, nw_ref, w_ref, wdt_ref, r_ref, q_ref, dt_ref, h_ref, *, rest_tiles):
    j = pl.program_id(1)

    @pl.when(j == 0)
    def _():
        hb = _rms(x_ref[...], nw_ref[...]).astype(BF16)
        h_ref[...] = hb
        dt_ref[...] = _dot(hb, wdt_ref[...])

    @pl.when(j < rest_tiles)
    def _():
        r_ref[...] = _dot(h_ref[...], w_ref[...]).astype(BF16)

    @pl.when(j >= rest_tiles)
    def _():
        q_ref[...] = _dot(h_ref[...], w_ref[...])


def _inproj(x, nw, w_main, w_dt, tm, tn):
    t = x.shape[0]
    rest_tiles = N_REST // tn
    return pl.pallas_call(
        functools.partial(_inproj_kernel, rest_tiles=rest_tiles),
        grid=(t // tm, N_MAIN // tn),
        in_specs=[
            pl.BlockSpec((tm, D_MODEL), lambda i, j: (i, 0)),
            pl.BlockSpec((1, D_MODEL), lambda i, j: (0, 0)),
            pl.BlockSpec((D_MODEL, tn), lambda i, j: (0, j)),
            pl.BlockSpec((D_MODEL, LANES), lambda i, j: (0, 0)),
        ],
        out_specs=[
            pl.BlockSpec((tm, tn), lambda i, j: (i, jnp.minimum(j, rest_tiles - 1))),
            pl.BlockSpec((tm, tn), lambda i, j: (i, jnp.maximum(j - rest_tiles, 0))),
            pl.BlockSpec((tm, LANES), lambda i, j: (i, 0)),
        ],
        out_shape=[
            jax.ShapeDtypeStruct((t, N_REST), BF16),
            jax.ShapeDtypeStruct((t, N_QKV), F32),
            jax.ShapeDtypeStruct((t, LANES), F32),
        ],
        scratch_shapes=[pltpu.VMEM((tm, D_MODEL), BF16)],
        compiler_params=_cparams(("parallel", "arbitrary")),
        name="inproj",
    )(x, nw, w_main, w_dt)


def _unroll_for(trips):
    return max(u for u in range(1, BLOCK_UNROLL + 1) if trips % u == 0)


def _attn_prompt_kernel(q0, k0, v0, q1, k1, v1, q2, k2, v2, o_ref,
                        kd0, vt0, kd1, vt1, kd2, vt2, lse_ref, acc_ref, s_buf, p_buf, stat_buf):
    n = pl.program_id(2)
    q_refs, k_refs, v_refs = (q0, q1, q2), (k0, k1, k2), (v0, v1, v2)
    kds, vts = (kd0, kd1, kd2), (vt0, vt1, vt2)
    krow = lax.broadcasted_iota(jnp.int32, (BAND, LANES), 0)
    qcol = lax.broadcasted_iota(jnp.int32, (BAND, LANES), 1)
    upper = krow > qcol
    diag = krow == qcol
    head0 = qcol < HEAD_DIM
    half = BAND // 2

    for g, (_, dil) in enumerate(ATTN_GROUPS):
        span = BAND * dil
        nsub = ATTN_TILE // span
        nblk = nsub * dil
        q_ref, k_ref, v_ref, kd, vt = q_refs[g], k_refs[g], v_refs[g], kds[g], vts[g]

        @pl.when(n == 0)
        def _():
            for r in range(dil):
                kd[r * (nsub + 1)] = jnp.zeros((BAND, LANES), BF16)
                vt[r * (nsub + 1)] = jnp.zeros((LANES, BAND), BF16)

        @pl.when(n > 0)
        def _():
            for r in range(dil):
                kd[r * (nsub + 1)] = kd[r * (nsub + 1) + nsub]
                vt[r * (nsub + 1)] = vt[r * (nsub + 1) + nsub]

        def split(i):
            if dil == 1:
                return i, 0
            j = i // dil
            return j, i - j * dil

        def rows_of(i):
            j, r = split(i)
            if dil == 1:
                return pl.ds(i * BAND if isinstance(i, int) else pl.multiple_of(i * BAND, BAND), BAND)
            return pl.ds(j * span + r, BAND, stride=dil)

        def deinterleave(i, c):
            j, r = split(i)
            slot = r * (nsub + 1) + j + 1
            rows = rows_of(i)
            kd[slot] = k_ref[rows, :].astype(BF16)
            vt[slot] = v_ref[rows, :].T.astype(BF16)
            return c

        lax.fori_loop(0, nblk, deinterleave, 0, unroll=_unroll_for(nblk))

        def qk_stage(i, sl):
            j, r = split(i)
            slot = r * (nsub + 1) + j
            q = q_ref[rows_of(i), :] * (ATTN_SCALE * LOG2E)
            q2 = jnp.concatenate([jnp.where(head0, q, 0.0), jnp.where(head0, 0.0, q)], axis=0).astype(BF16)
            kk = kd[pl.ds(slot, 2)].reshape(2 * BAND, LANES)
            s_buf[sl] = _dot_nt(kk, q2)

        def softmax_stage(sl, first):
            for hh in range(2):
                sp = s_buf[sl, 0:BAND, hh * BAND:(hh + 1) * BAND]
                sc = s_buf[sl, BAND:2 * BAND, hh * BAND:(hh + 1) * BAND]
                if first:
                    sp = jnp.where(n > 0, sp, -jnp.inf)
                comb = jnp.where(upper, sp, sc)
                dg = jnp.sum(jnp.where(diag, sp, 0.0), axis=0, keepdims=True)
                m = jnp.maximum(jnp.max(comb, axis=0, keepdims=True), dg)
                p = jnp.exp2(comb - m)
                pd = jnp.exp2(dg - m)
                den = jnp.sum(p, axis=0, keepdims=True) + pd
                p_buf[sl, hh, 0:BAND, :] = jnp.where(upper, p, jnp.where(diag, pd, 0.0)).astype(BF16)
                p_buf[sl, hh, BAND:2 * BAND, :] = jnp.where(upper, 0.0, p).astype(BF16)
                stat_buf[sl, 2 * hh:2 * hh + 1, :] = 1.0 / den
                stat_buf[sl, 2 * hh + 1:2 * hh + 2, :] = m + jnp.log2(den)

        def pv_stage(i, sl):
            j, r = split(i)
            slot = r * (nsub + 1) + j
            rows = rows_of(i)
            vprev = vt[slot]
            vcur = vt[slot + 1]
            ots, lts = [], []
            for hh in range(2):
                vsl = slice(hh * half, (hh + 1) * half)
                vtt = jnp.concatenate([vprev[vsl, :], vcur[vsl, :]], axis=1)
                ots.append(_dot(vtt, p_buf[sl, hh]) * stat_buf[sl, 2 * hh:2 * hh + 1, :])
                lts.append(jnp.broadcast_to(stat_buf[sl, 2 * hh + 1:2 * hh + 2, :], (half, BAND)))
            o_blk = jnp.concatenate(ots, axis=0).T
            lse_blk = jnp.concatenate(lts, axis=0).T
            if g == 0:
                lse_ref[rows, :] = lse_blk
                acc_ref[rows, :] = o_blk
            else:
                lse_run = lse_ref[rows, :]
                acc_run = acc_ref[rows, :]
                mx = jnp.maximum(lse_run, lse_blk)
                ea = jnp.exp2(lse_run - mx)
                eb = jnp.exp2(lse_blk - mx)
                tot = ea + eb
                inv = 1.0 / tot
                merged = (ea * inv) * acc_run + (eb * inv) * o_blk
                if g == len(ATTN_GROUPS) - 1:
                    o_ref[rows, :] = merged
                else:
                    lse_ref[rows, :] = mx + jnp.log2(tot)
                    acc_ref[rows, :] = merged

        npairs = nblk // 2
        firsts_of = lambda pair: (2 * pair < dil, 2 * pair + 1 < dil)

        def qk_pair(pair, par):
            qk_stage(2 * pair, 2 * par)
            qk_stage(2 * pair + 1, 2 * par + 1)

        def softmax_pair(par, firsts):
            softmax_stage(2 * par, firsts[0])
            softmax_stage(2 * par + 1, firsts[1])

        def pv_pair(pair, par):
            pv_stage(2 * pair, 2 * par)
            pv_stage(2 * pair + 1, 2 * par + 1)

        qk_pair(0, 0)
        softmax_pair(0, firsts_of(0))
        qk_pair(1, 1)
        all_first = dil >= nblk

        def steady(t, c, firsts):
            for par in range(2):
                pair = 2 * t + 2 + par
                pv_pair(pair - 2, par)
                softmax_pair(1 - par, firsts[par])
                qk_pair(pair, par)
            return c

        trips = (npairs - 2) // 2
        start = 0
        if any(firsts_of(1)) and not all_first:
            steady(0, 0, (firsts_of(1), firsts_of(2)))
            start = 1
        lax.fori_loop(start, trips, functools.partial(steady, firsts=((all_first,) * 2,) * 2), 0)
        pv_pair(npairs - 2, 0)
        softmax_pair(1, firsts_of(npairs - 1))
        pv_pair(npairs - 1, 1)


def _attn_prompt(u, batch, seq):
    nt = seq // ATTN_TILE
    hp = A_WIDTH // LANES
    qkv_blk = 0

    def spec(g, which):
        col = qkv_blk + (3 * g + which) * hp
        return pl.BlockSpec((ATTN_TILE, LANES), lambda b, h, n, col=col: (b * nt + n, col + h))

    in_specs = [spec(g, w) for g in range(3) for w in range(3)]
    scratch = []
    for _, dil in ATTN_GROUPS:
        nslots = dil * (ATTN_TILE // (BAND * dil) + 1)
        scratch += [pltpu.VMEM((nslots, BAND, LANES), BF16), pltpu.VMEM((nslots, LANES, BAND), BF16)]
    scratch += [pltpu.VMEM((ATTN_TILE, LANES), F32), pltpu.VMEM((ATTN_TILE, LANES), F32)]
    scratch += [pltpu.VMEM((4, 2 * BAND, 2 * BAND), F32), pltpu.VMEM((4, 2, 2 * BAND, BAND), BF16),
                pltpu.VMEM((4, HALO, BAND), F32)]
    return pl.pallas_call(
        _attn_prompt_kernel,
        grid=(batch, hp, nt),
        in_specs=in_specs,
        out_specs=pl.BlockSpec((ATTN_TILE, LANES), lambda b, h, n: (b * nt + n, h)),
        out_shape=jax.ShapeDtypeStruct((batch * seq, A_WIDTH), F32),
        scratch_shapes=scratch,
        compiler_params=_cparams(("parallel", "parallel", "arbitrary")),
        name="attn_prompt",
    )(*([u] * 9))


def _conv_silu(pad_ref, w_ref, b_ref, rows):
    acc = b_ref[...] + pad_ref[HALO - (CONV_W - 1):HALO - (CONV_W - 1) + rows, :] * w_ref[0:1, :]
    for i in range(1, CONV_W):
        off = HALO - (CONV_W - 1) + i
        acc = acc + pad_ref[off:off + rows, :] * w_ref[i:i + 1, :]
    return _silu(acc)


def _conv_silu_chunk(x_ref, tail_ref, w_ref, b_ref):
    x = x_ref[...].astype(F32)
    ext = jnp.concatenate([tail_ref[...], x], axis=0)
    tail_ref[...] = x[x.shape[0] - HALO:, :]
    acc = b_ref[...] + x * w_ref[CONV_W - 1:CONV_W, :]
    for k in range(1, CONV_W):
        acc = acc + pltpu.roll(ext, k, 0)[HALO:, :] * w_ref[CONV_W - 1 - k:CONV_W - k, :]
    return _silu(acc)


def _softplus(x):
    return jnp.maximum(x, 0.0) + jnp.log1p(jnp.exp(-jnp.abs(x)))


def _ssd_prompt_kernel(x_ref, b_ref, c_ref, dt_ref, cwx, cwb, cwc, cbx, cbb, cbc,
                       dtb_ref, alog_ref, dexp_ref, e_ref, y_ref, st_ref,
                       xtail, btail, ctail, state):
    c = pl.program_id(1)
    nc = pl.num_programs(1)

    @pl.when(c == 0)
    def _():
        xtail[...] = jnp.zeros(xtail.shape, F32)
        btail[...] = jnp.zeros(btail.shape, F32)
        ctail[...] = jnp.zeros(ctail.shape, F32)
        state[...] = jnp.zeros(state.shape, F32)

    xc = _conv_silu_chunk(x_ref, xtail, cwx, cbx)
    bm = _conv_silu_chunk(b_ref, btail, cwb, cbb)
    cm = _conv_silu_chunk(c_ref, ctail, cwc, cbc)

    dt = _softplus(dt_ref[...] + dtb_ref[...])
    a = -jnp.exp(alog_ref[...])
    dta = dt * a
    ti = lax.broadcasted_iota(jnp.int32, (CHUNK, CHUNK), 0)
    si = lax.broadcasted_iota(jnp.int32, (CHUNK, CHUNK), 1)
    causal = si <= ti
    tril = jnp.where(causal, 1.0, 0.0).astype(F32)
    cum = jnp.dot(tril, dta, preferred_element_type=F32, precision=lax.Precision.HIGHEST)
    cum2 = cum * LOG2E
    cum2_t = cum2.T
    cum2_last = cum2[CHUNK - 1:CHUNK, :]
    expcum = jnp.exp2(cum2)
    to_end = jnp.exp2(cum2_last - cum2) * dt
    ex = _expand01(jnp.concatenate([expcum, to_end, dt], axis=0), e_ref[...])
    expcum_x = ex[0:CHUNK, :]
    to_end_x = ex[CHUNK:2 * CHUNK, :]
    dt_x = ex[2 * CHUNK:3 * CHUNK, :]
    dec_x = expcum_x[CHUNK - 1:CHUNK, :]

    lane = lax.broadcasted_iota(jnp.int32, (CHUNK, LANES), 1)
    head0 = lane < SSD_HEAD_DIM
    hpg = SSD_HEADS // SSD_GROUPS
    for g in range(SSD_GROUPS):
        gs = slice(g * SSD_GROUP_W, (g + 1) * SSD_GROUP_W)
        ns = slice(g * SSD_STATE, (g + 1) * SSD_STATE)
        bg = bm[:, ns].astype(BF16)
        cg = cm[:, ns].astype(BF16)
        cb = _dot_nt(cg, bg)
        xg = xc[:, gs]
        xdt = xg * dt_x[:, gs]
        ys = []
        for pr in range(hpg // 2):
            ws = []
            for hh in range(2):
                h = g * hpg + 2 * pr + hh
                colb = jnp.broadcast_to(cum2[:, h:h + 1], (CHUNK, CHUNK))
                rowb = jnp.broadcast_to(cum2_t[h:h + 1, :], (CHUNK, CHUNK))
                decay = jnp.exp2(jnp.where(causal, colb - rowb, -jnp.inf))
                ws.append((cb * decay).astype(BF16))
            w2 = jnp.concatenate(ws, axis=1)
            xp = xdt[:, pr * LANES:(pr + 1) * LANES]
            x2 = jnp.concatenate([jnp.where(head0, xp, 0.0), jnp.where(head0, 0.0, xp)], axis=0).astype(BF16)
            ys.append(_dot(w2, x2))
        st = state[g]
        y_state = _dot(cg, st.astype(BF16))
        y_ref[:, gs] = jnp.concatenate(ys, axis=1) + y_state * expcum_x[:, gs] + dexp_ref[:, gs] * xg
        xs = (xg * to_end_x[:, gs]).astype(BF16)
        state[g] = dec_x[:, gs] * st + _dot_tn(bg, xs)

    @pl.when(c == nc - 1)
    def _():
        for g in range(SSD_GROUPS):
            st_ref[0, g * hpg:(g + 1) * hpg, :, :] = state[g].T.reshape(hpg, SSD_HEAD_DIM, SSD_STATE)


def _ssd_prompt(u, dt_raw, conv_w, conv_b, dt_bias, a_log, d_exp, e_mat, batch, seq):
    nc = seq // CHUNK
    cwx, cwb, cwc = conv_w[:, :D_INNER], conv_w[:, D_INNER:D_INNER + SSD_GROUP_W], conv_w[:, D_INNER + SSD_GROUP_W:]
    cbx, cbb, cbc = conv_b[:, :D_INNER], conv_b[:, D_INNER:D_INNER + SSD_GROUP_W], conv_b[:, D_INNER + SSD_GROUP_W:]
    full = lambda shape: pl.BlockSpec(shape, lambda b, c: (0,) * len(shape))
    return pl.pallas_call(
        _ssd_prompt_kernel,
        grid=(batch, nc),
        in_specs=[
            pl.BlockSpec((CHUNK, D_INNER), lambda b, c: (b * nc + c, X_OFF // D_INNER)),
            pl.BlockSpec((CHUNK, SSD_GROUP_W), lambda b, c: (b * nc + c, B_OFF // SSD_GROUP_W)),
            pl.BlockSpec((CHUNK, SSD_GROUP_W), lambda b, c: (b * nc + c, C_OFF // SSD_GROUP_W)),
            pl.BlockSpec((CHUNK, LANES), lambda b, c: (b * nc + c, 0)),
            full((CONV_W, D_INNER)), full((CONV_W, SSD_GROUP_W)), full((CONV_W, SSD_GROUP_W)),
            full((1, D_INNER)), full((1, SSD_GROUP_W)), full((1, SSD_GROUP_W)),
            full((1, LANES)), full((1, LANES)), full((1, D_INNER)), full((LANES, D_INNER)),
        ],
        out_specs=[
            pl.BlockSpec((CHUNK, D_INNER), lambda b, c: (b * nc + c, 0)),
            pl.BlockSpec((1, SSD_HEADS, SSD_HEAD_DIM, SSD_STATE), lambda b, c: (b, 0, 0, 0)),
        ],
        out_shape=[
            jax.ShapeDtypeStruct((batch * seq, D_INNER), F32),
            jax.ShapeDtypeStruct((batch, SSD_HEADS, SSD_HEAD_DIM, SSD_STATE), F32),
        ],
        scratch_shapes=[
            pltpu.VMEM((HALO, D_INNER), F32),
            pltpu.VMEM((HALO, SSD_GROUP_W), F32),
            pltpu.VMEM((HALO, SSD_GROUP_W), F32),
            pltpu.VMEM((SSD_GROUPS, SSD_STATE, SSD_GROUP_W), F32),
        ],
        compiler_params=_cparams(("parallel", "arbitrary")),
        name="ssd_prompt",
    )(u, u, u, dt_raw, cwx, cwb, cwc, cbx, cbb, cbc, dt_bias, a_log, d_exp, e_mat)


def _mix_ffn_kernel(a_ref, y_ref, z_ref, ga_ref, gb_ref, x_ref, wa_ref, ws_ref, wo_ref,
                    sn_ref, pn_ref, wu_ref, wd_ref, n1_ref, n2_ref, o_ref):
    a_proj = _dot(a_ref[...].astype(BF16), wa_ref[...])
    g = y_ref[...] * _silu(z_ref[...].astype(F32))
    parts = []
    for i in range(SSD_GROUPS):
        gi = g[:, i * SSD_GROUP_W:(i + 1) * SSD_GROUP_W]
        parts.append(gi * lax.rsqrt(jnp.mean(gi * gi, axis=-1, keepdims=True) + RMS_EPS))
    yn = (jnp.concatenate(parts, axis=1) * sn_ref[...]).astype(BF16)
    b_proj = _dot(yn, ws_ref[...])
    mixed_in = _sigmoid(ga_ref[...].astype(F32)) * a_proj + _sigmoid(gb_ref[...].astype(F32)) * b_proj
    mixed = _dot(mixed_in.astype(BF16), wo_ref[...])
    x1 = x_ref[...] + _rms(mixed, pn_ref[...])
    h = _rms(x1, n1_ref[...]).astype(BF16)
    gu = _dot(h, wu_ref[...])
    act = (_silu(gu[:, :D_FF]) * gu[:, D_FF:]).astype(BF16)
    f = _dot(act, wd_ref[...])
    o_ref[...] = x1 + _rms(f, n2_ref[...])


def _mix_ffn(a_out, y, u, x, w_attn_out, w_ssd_out, w_out, ssd_norm, post_mix_norm,
             w_up, w_down, pre_norm, post_norm, tm):
    t = x.shape[0]
    const = lambda shape: pl.BlockSpec(shape, lambda i: (0, 0), pipeline_mode=pl.Buffered(1))
    return pl.pallas_call(
        _mix_ffn_kernel,
        grid=(t // tm,),
        in_specs=[
            pl.BlockSpec((tm, A_WIDTH), lambda i: (i, 0)),
            pl.BlockSpec((tm, D_INNER), lambda i: (i, 0)),
            pl.BlockSpec((tm, D_INNER), lambda i: (i, Z_OFF // D_INNER)),
            pl.BlockSpec((tm, D_MODEL), lambda i: (i, GATE_OFF // D_MODEL)),
            pl.BlockSpec((tm, D_MODEL), lambda i: (i, GATE_OFF // D_MODEL + 1)),
            pl.BlockSpec((tm, D_MODEL), lambda i: (i, 0)),
            const((A_WIDTH, D_MODEL)), const((D_INNER, D_MODEL)), const((D_MODEL, D_MODEL)),
            const((1, D_INNER)), const((1, D_MODEL)),
            const((D_MODEL, 2 * D_FF)), const((D_FF, D_MODEL)),
            const((1, D_MODEL)), const((1, D_MODEL)),
        ],
        out_specs=pl.BlockSpec((tm, D_MODEL), lambda i: (i, 0)),
        out_shape=jax.ShapeDtypeStruct((t, D_MODEL), F32),
        compiler_params=_cparams(("parallel",)),
        name="mix_ffn",
    )(a_out, y, u, u, u, x, w_attn_out, w_ssd_out, w_out, ssd_norm, post_mix_norm,
      w_up, w_down, pre_norm, post_norm)


def _attn_sample_kernel(q_ref, new_ref, c0, c1, c2, ao_ref, o0, o1, o2):
    n_new = q_ref.shape[3]
    new_t = new_ref[0, 0]
    caches, outs = (c0, c1, c2), (o0, o1, o2)
    m_run = l_run = o_run = None
    for g, (win, dil) in enumerate(ATTN_GROUPS):
        q = (q_ref[0, 0, g] * ATTN_SCALE).astype(BF16)
        k_t = caches[g][0, 0, 0]
        v_t = caches[g][0, 1, 0]
        knt = pltpu.roll(new_t, LANES - n_new - (2 * g) * n_new, 1)
        vnt = pltpu.roll(new_t, LANES - n_new - (2 * g + 1) * n_new, 1)
        qi = lax.broadcasted_iota(jnp.int32, (n_new, win), 0)
        ci = lax.broadcasted_iota(jnp.int32, (n_new, win), 1)
        ok_c = jnp.logical_and(ci >= qi, ((ci - qi) & (dil - 1)) == 0)
        qn = lax.broadcasted_iota(jnp.int32, (n_new, LANES), 0)
        nn = lax.broadcasted_iota(jnp.int32, (n_new, LANES), 1) - (LANES - n_new)
        ok_n = jnp.logical_and(jnp.logical_and(nn >= 0, nn <= qn), ((qn - nn) & (dil - 1)) == 0)
        s_c = jnp.where(ok_c, _dot(q, k_t.astype(BF16)), -jnp.inf)
        s_n = jnp.where(ok_n, _dot(q, knt.astype(BF16)), -jnp.inf)
        m = jnp.maximum(jnp.max(s_c, axis=1, keepdims=True), jnp.max(s_n, axis=1, keepdims=True))
        p_c = jnp.exp(s_c - m)
        p_n = jnp.exp(s_n - m)
        den = jnp.sum(p_c, axis=1, keepdims=True) + jnp.sum(p_n, axis=1, keepdims=True)
        o = _dot_nt(p_c.astype(BF16), v_t.astype(BF16)) + _dot_nt(p_n.astype(BF16), vnt.astype(BF16))
        if g == 0:
            m_run, l_run, o_run = m, den, o
        else:
            m_new = jnp.maximum(m_run, m)
            ea = jnp.exp(m_run - m_new)
            eb = jnp.exp(m - m_new)
            l_run = ea * l_run + eb * den
            o_run = ea * o_run + eb * o
            m_run = m_new
        lane = lax.broadcasted_iota(jnp.int32, (HEAD_DIM, LANES), 1)
        keep = lane < LANES - n_new
        for which, (src, new) in enumerate(((k_t, knt), (v_t, vnt))):
            rolled = pltpu.roll(src, win - n_new, 1)
            if win > LANES:
                outs[g][0, which, 0, :, 0:win - LANES] = rolled[:, 0:win - LANES]
            outs[g][0, which, 0, :, win - LANES:win] = jnp.where(keep, rolled[:, win - LANES:win], new)
    ao_ref[0, 0] = o_run * (1.0 / l_run)


def _attn_sample(q_s, new_t, caches):
    bsz = q_s.shape[0]
    n_new = q_s.shape[3]
    cache_spec = lambda win: pl.BlockSpec((1, 2, 1, HEAD_DIM, win), lambda b, h: (b, 0, h, 0, 0))
    return pl.pallas_call(
        _attn_sample_kernel,
        grid=(bsz, HEADS),
        in_specs=[
            pl.BlockSpec((1, 1, 3, n_new, HEAD_DIM), lambda b, h: (b, h, 0, 0, 0)),
            pl.BlockSpec((1, 1, HEAD_DIM, LANES), lambda b, h: (b, h, 0, 0)),
        ] + [cache_spec(w) for w, _ in ATTN_GROUPS],
        out_specs=[pl.BlockSpec((1, 1, n_new, HEAD_DIM), lambda b, h: (b, h, 0, 0))]
        + [cache_spec(w) for w, _ in ATTN_GROUPS],
        out_shape=[jax.ShapeDtypeStruct((bsz, HEADS, n_new, HEAD_DIM), F32)]
        + [jax.ShapeDtypeStruct(c.shape, F32) for c in caches],
        compiler_params=_cparams(("parallel", "parallel")),
        name="attn_sample",
    )(q_s, new_t, *caches)


def _ssd_sample_kernel(x_ref, b_ref, c_ref, dt_ref, sx_ref, sb_ref, sc_ref, st_in,
                       cwx, cwb, cwc, cbx, cbb, cbc, dtb_ref, alog_ref, dexp_ref, e_ref,
                       y_ref, st_out, xpad, bpad, cpad):
    n = x_ref.shape[0]
    keep = CONV_W - 1
    for pad, new, old in ((xpad, x_ref, sx_ref), (bpad, b_ref, sb_ref), (cpad, c_ref, sc_ref)):
        pad[0:HALO, :] = jnp.zeros((HALO, pad.shape[1]), F32)
        pad[HALO - keep:HALO, :] = old[0]
        pad[HALO:HALO + n, :] = new[...]
    xc = _conv_silu(xpad, cwx, cbx, n)
    bm = _conv_silu(bpad, cwb, cbb, n)
    cm = _conv_silu(cpad, cwc, cbc, n)

    dt = _softplus(dt_ref[...] + dtb_ref[...])
    a = -jnp.exp(alog_ref[...])
    dta = dt * a
    ti = lax.broadcasted_iota(jnp.int32, (n, n), 0)
    si = lax.broadcasted_iota(jnp.int32, (n, n), 1)
    tril = jnp.where(si <= ti, 1.0, 0.0).astype(F32)
    cum = jnp.dot(tril, dta, preferred_element_type=F32, precision=lax.Precision.HIGHEST)
    cum_last = cum[n - 1:n, :]
    expcum = jnp.exp(cum)
    to_end = jnp.exp(cum_last - cum) * dt

    hpg = SSD_HEADS // SSD_GROUPS
    lane = lax.broadcasted_iota(jnp.int32, (n, LANES), 1)
    trow = lax.broadcasted_iota(jnp.int32, (n, LANES), 0)
    cbs = [_dot_nt(cm[:, g * SSD_STATE:(g + 1) * SSD_STATE].astype(BF16),
                   bm[:, g * SSD_STATE:(g + 1) * SSD_STATE].astype(BF16)) for g in range(SSD_GROUPS)]
    coefs = []
    for s in range(n):
        cbh = jnp.zeros((n, LANES), F32)
        for g in range(SSD_GROUPS):
            in_g = jnp.logical_and(lane >= g * hpg, lane < (g + 1) * hpg)
            cbh = jnp.where(in_g, jnp.broadcast_to(cbs[g][:, s:s + 1], (n, LANES)), cbh)
        decay = jnp.exp(jnp.where(trow >= s, cum - cum[s:s + 1, :], -jnp.inf))
        coefs.append(cbh * decay * dt[s:s + 1, :])
    ex = _expand01(jnp.concatenate(coefs + [expcum, to_end], axis=0), e_ref[...])
    y = dexp_ref[...] * xc
    for s in range(n):
        y = y + ex[s * n:(s + 1) * n, :] * xc[s:s + 1, :]
    expcum_x = ex[n * n:n * n + n, :]
    to_end_x = ex[n * n + n:n * n + 2 * n, :]
    xs = xc * to_end_x
    dec_t = jnp.broadcast_to(expcum[n - 1:n, :], (LANES, LANES)).T
    ys = []
    for g in range(SSD_GROUPS):
        gs = slice(g * SSD_GROUP_W, (g + 1) * SSD_GROUP_W)
        ns = slice(g * SSD_STATE, (g + 1) * SSD_STATE)
        st = st_in[0, g * hpg:(g + 1) * hpg].reshape(SSD_GROUP_W, SSD_STATE)
        ys.append(_dot_nt(cm[:, ns].astype(BF16), st.astype(BF16)))
        upd = _dot_tn(xs[:, gs].astype(BF16), bm[:, ns].astype(BF16))
        for hh in range(hpg):
            h = g * hpg + hh
            rows = slice(hh * SSD_HEAD_DIM, (hh + 1) * SSD_HEAD_DIM)
            st_out[0, h] = dec_t[h:h + 1, :] * st[rows, :] + upd[rows, :]
    y_ref[...] = y + jnp.concatenate(ys, axis=1) * expcum_x


def _ssd_sample(u, dt_raw, conv_state, ssm_state, conv_w, conv_b, dt_bias, a_log, d_exp, e_mat, n_new):
    bsz = ssm_state.shape[0]
    keep = CONV_W - 1
    cwx, cwb, cwc = conv_w[:, :D_INNER], conv_w[:, D_INNER:D_INNER + SSD_GROUP_W], conv_w[:, D_INNER + SSD_GROUP_W:]
    cbx, cbb, cbc = conv_b[:, :D_INNER], conv_b[:, D_INNER:D_INNER + SSD_GROUP_W], conv_b[:, D_INNER + SSD_GROUP_W:]
    full = lambda shape: pl.BlockSpec(shape, lambda b: (0,) * len(shape))
    st_spec = pl.BlockSpec((1, SSD_HEADS, SSD_HEAD_DIM, SSD_STATE), lambda b: (b, 0, 0, 0))
    return pl.pallas_call(
        _ssd_sample_kernel,
        grid=(bsz,),
        in_specs=[
            pl.BlockSpec((n_new, D_INNER), lambda b: (b, X_OFF // D_INNER)),
            pl.BlockSpec((n_new, SSD_GROUP_W), lambda b: (b, B_OFF // SSD_GROUP_W)),
            pl.BlockSpec((n_new, SSD_GROUP_W), lambda b: (b, C_OFF // SSD_GROUP_W)),
            pl.BlockSpec((n_new, LANES), lambda b: (b, 0)),
            pl.BlockSpec((1, keep, D_INNER), lambda b: (b, 0, 0)),
            pl.BlockSpec((1, keep, SSD_GROUP_W), lambda b: (b, 0, D_INNER // SSD_GROUP_W)),
            pl.BlockSpec((1, keep, SSD_GROUP_W), lambda b: (b, 0, D_INNER // SSD_GROUP_W + 1)),
            st_spec,
            full((CONV_W, D_INNER)), full((CONV_W, SSD_GROUP_W)), full((CONV_W, SSD_GROUP_W)),
            full((1, D_INNER)), full((1, SSD_GROUP_W)), full((1, SSD_GROUP_W)),
            full((1, LANES)), full((1, LANES)), full((1, D_INNER)), full((LANES, D_INNER)),
        ],
        out_specs=[pl.BlockSpec((n_new, D_INNER), lambda b: (b, 0)), st_spec],
        out_shape=[
            jax.ShapeDtypeStruct((bsz * n_new, D_INNER), F32),
            jax.ShapeDtypeStruct(ssm_state.shape, F32),
        ],
        scratch_shapes=[
            pltpu.VMEM((HALO + n_new, D_INNER), F32),
            pltpu.VMEM((HALO + n_new, SSD_GROUP_W), F32),
            pltpu.VMEM((HALO + n_new, SSD_GROUP_W), F32),
        ],
        compiler_params=_cparams(("parallel",)),
        name="ssd_sample",
    )(u, u, u, dt_raw, conv_state, conv_state, conv_state, ssm_state,
      cwx, cwb, cwc, cbx, cbb, cbc, dt_bias, a_log, d_exp, e_mat)


def _prep_weights(w_in, dt_bias, d_skip):
    q_end = 3 * len(ATTN_GROUPS) * A_WIDTH
    z_end = q_end + D_INNER
    x_end = z_end + D_INNER
    b_end = x_end + SSD_GROUP_W
    c_end = b_end + SSD_GROUP_W
    dt_end = c_end + SSD_HEADS
    w_main = jnp.concatenate(
        [w_in[:, q_end:z_end], w_in[:, dt_end:], w_in[:, z_end:c_end], w_in[:, :q_end]], axis=1).astype(BF16)
    w_dt = jnp.pad(w_in[:, c_end:dt_end], ((0, 0), (0, LANES - SSD_HEADS))).astype(BF16)
    dtb = jnp.pad(dt_bias, (0, LANES - SSD_HEADS)).reshape(1, LANES)
    d_exp = jnp.repeat(d_skip, SSD_HEAD_DIM).reshape(1, D_INNER)
    return w_main, w_dt, dtb, d_exp


def _tail_kv(u, batch, seq, g, win):
    keep = min(win, seq)
    u3 = u.reshape(batch, seq, N_QKV)
    off = 3 * g * A_WIDTH
    k = u3[:, seq - keep:, off + A_WIDTH:off + 2 * A_WIDTH]
    v = u3[:, seq - keep:, off + 2 * A_WIDTH:off + 3 * A_WIDTH]
    return jnp.stack([k, v], axis=2).reshape(batch, keep, 2, HEADS, HEAD_DIM)


def _tail_conv(u, batch, seq):
    u3 = u.reshape(batch, seq, N_REST)
    return u3[:, seq - (CONV_W - 1):, X_OFF:C_OFF + SSD_GROUP_W].astype(F32)


def kernel(x_prompt, x_sample, cache_kv_w128, cache_kv_w512, cache_kv_w2048, state_conv, state_ssm,
           pre_mix_norm, w_in, conv_w, conv_b, dt_bias, a_log, d_skip, ssd_norm, w_attn_out, w_ssd_out,
           w_out, post_mix_norm, pre_ffn_norm, w_up, w_down, post_ffn_norm):
    batch, seq, _ = x_prompt.shape
    dbatch, dseq, _ = x_sample.shape
    assert w_in.shape[0] == 1, "single-layer trunk"
    assert seq % ATTN_TILE == 0 and dseq == HALO

    w_main, w_dt, dtb, d_exp = _prep_weights(w_in[0], dt_bias[0], d_skip[0])
    alog = jnp.pad(a_log[0], (0, LANES - SSD_HEADS)).reshape(1, LANES)
    e_mat = (jnp.arange(LANES)[:, None] == (jnp.arange(D_INNER)[None, :] // SSD_HEAD_DIM)).astype(BF16)
    nw = pre_mix_norm[0].reshape(1, D_MODEL)
    cw, cb = conv_w[0], conv_b[0].reshape(1, -1)
    wa, ws, wo = w_attn_out[0].astype(BF16), w_ssd_out[0].astype(BF16), w_out[0].astype(BF16)
    wu, wd = w_up[0].astype(BF16), w_down[0].astype(BF16)
    sn = ssd_norm[0].reshape(1, D_INNER)
    pmn, pfn, qfn = (post_mix_norm[0].reshape(1, -1), pre_ffn_norm[0].reshape(1, -1),
                     post_ffn_norm[0].reshape(1, -1))

    xp = x_prompt.reshape(batch * seq, D_MODEL)
    ur_p, uq_p, dt_p = _inproj(xp, nw, w_main, w_dt, min(2048, batch * seq), 512)
    a_p = _attn_prompt(uq_p, batch, seq)
    y_p, ssm_p = _ssd_prompt(ur_p, dt_p, cw, cb, dtb, alog, d_exp, e_mat, batch, seq)
    out_p = _mix_ffn(a_p, y_p, ur_p, xp, wa, ws, wo, sn, pmn, wu, wd, pfn, qfn, 256).reshape(batch, seq, D_MODEL)
    kv_p = [_tail_kv(uq_p, batch, seq, g, w)[None] for g, (w, _) in enumerate(ATTN_GROUPS)]
    conv_p = _tail_conv(ur_p, batch, seq)[None]

    xs = x_sample.reshape(dbatch * dseq, D_MODEL)
    ur_s, uq_s, dt_s = _inproj(xs, nw, w_main, w_dt, min(1024, dbatch * dseq), 512)
    qkv = uq_s.reshape(dbatch, dseq, 3, 3, HEADS, HEAD_DIM)
    q_s = qkv[:, :, :, 0].transpose(0, 3, 2, 1, 4)
    new_t = qkv[:, :, :, 1:3].transpose(0, 4, 5, 2, 3, 1).reshape(dbatch, HEADS, HEAD_DIM, 6 * dseq)
    new_t = jnp.pad(new_t, ((0, 0), (0, 0), (0, 0), (0, LANES - 6 * dseq)))
    caches = [c[0].transpose(0, 2, 3, 4, 1) for c in (cache_kv_w128, cache_kv_w512, cache_kv_w2048)]
    ao, nc0, nc1, nc2 = _attn_sample(q_s, new_t, caches)
    a_s = ao.transpose(0, 2, 1, 3).reshape(dbatch * dseq, A_WIDTH)
    kv_s = [c.transpose(0, 4, 1, 2, 3)[None] for c in (nc0, nc1, nc2)]
    y_s, ssm_s = _ssd_sample(ur_s.astype(F32), dt_s, state_conv[0], state_ssm[0], cw, cb, dtb, alog, d_exp,
                             e_mat, dseq)
    tm_s = min(256, dbatch * dseq)
    out_s = _mix_ffn(a_s, y_s, ur_s, xs, wa, ws, wo, sn, pmn, wu, wd, pfn, qfn, tm_s).reshape(dbatch, dseq, D_MODEL)
    conv_s = _tail_conv(ur_s, dbatch, dseq)[None]

    return (out_p, out_s, kv_p[0], kv_p[1], kv_p[2], conv_p, ssm_p[None],
            kv_s[0], kv_s[1], kv_s[2], conv_s, ssm_s[None])
```

```python
import functools

import jax
import jax.numpy as jnp
from jax import lax
from jax.experimental import pallas as pl
from jax.experimental.pallas import tpu as pltpu

F32 = jnp.float32
BF16 = jnp.bfloat16

D_MODEL = 1024
ATTN_GROUPS = ((128, 1), (512, 4), (2048, 16))
BAND = 128
HEADS = 8
HEAD_DIM = 64
A_WIDTH = HEADS * HEAD_DIM
ATTN_SCALE = HEAD_DIM ** -0.5
LOG2E = 1.4426950408889634
ATTN_TILE = 2048
BLOCK_UNROLL = 4

D_INNER = 2048
SSD_HEADS = 32
SSD_HEAD_DIM = 64
SSD_GROUPS = 4
SSD_GROUP_W = D_INNER // SSD_GROUPS
SSD_STATE = 128
CONV_W = 4
CHUNK = 128
D_FF = 2816
RMS_EPS = 1e-6
LANES = 128
HALO = 8

Z_OFF = 0
GATE_OFF = 2048
X_OFF = 4096
B_OFF = 6144
C_OFF = 6656
N_REST = 7168
N_QKV = 3 * len(ATTN_GROUPS) * A_WIDTH
N_MAIN = N_REST + N_QKV

VMEM_LIMIT = 56 * 1024 * 1024


def _cparams(sem):
    return pltpu.CompilerParams(dimension_semantics=sem, vmem_limit_bytes=VMEM_LIMIT)


def _rms(x, w):
    return x * lax.rsqrt(jnp.mean(x * x, axis=-1, keepdims=True) + RMS_EPS) * w


def _silu(x):
    h = 0.5 * x
    return h + h * jnp.tanh(h)


def _sigmoid(x):
    return 0.5 + 0.5 * jnp.tanh(0.5 * x)


def _dot(a, b):
    return jnp.dot(a, b, preferred_element_type=F32)


def _dot_nt(a, b):
    return lax.dot_general(a, b, (((1,), (1,)), ((), ())), preferred_element_type=F32)


def _dot_tn(a, b):
    return lax.dot_general(a, b, (((0,), (0,)), ((), ())), preferred_element_type=F32)


def _expand01(a, e):
    hi = a.astype(BF16)
    mid = (a - hi.astype(F32)).astype(BF16)
    return _dot(hi, e) + _dot(mid, e)


def _inproj_kernel(x_ref, nw_ref, w_ref, wdt_ref, r_ref, q_ref, dt_ref, h_ref, *, rest_tiles):
    j = pl.program_id(1)

    @pl.when(j == 0)
    def _():
        hb = _rms(x_ref[...], nw_ref[...]).astype(BF16)
        h_ref[...] = hb
        dt_ref[...] = _dot(hb, wdt_ref[...])

    @pl.when(j < rest_tiles)
    def _():
        r_ref[...] = _dot(h_ref[...], w_ref[...]).astype(BF16)

    @pl.when(j >= rest_tiles)
    def _():
        q_ref[...] = _dot(h_ref[...], w_ref[...])


def _inproj(x, nw, w_main, w_dt, tm, tn):
    t = x.shape[0]
    rest_tiles = N_REST // tn
    return pl.pallas_call(
        functools.partial(_inproj_kernel, rest_tiles=rest_tiles),
        grid=(t // tm, N_MAIN // tn),
        in_specs=[
            pl.BlockSpec((tm, D_MODEL), lambda i, j: (i, 0)),
            pl.BlockSpec((1, D_MODEL), lambda i, j: (0, 0)),
            pl.BlockSpec((D_MODEL, tn), lambda i, j: (0, j)),
            pl.BlockSpec((D_MODEL, LANES), lambda i, j: (0, 0)),
        ],
        out_specs=[
            pl.BlockSpec((tm, tn), lambda i, j: (i, jnp.minimum(j, rest_tiles - 1))),
            pl.BlockSpec((tm, tn), lambda i, j: (i, jnp.maximum(j - rest_tiles, 0))),
            pl.BlockSpec((tm, LANES), lambda i, j: (i, 0)),
        ],
        out_shape=[
            jax.ShapeDtypeStruct((t, N_REST), BF16),
            jax.ShapeDtypeStruct((t, N_QKV), F32),
            jax.ShapeDtypeStruct((t, LANES), F32),
        ],
        scratch_shapes=[pltpu.VMEM((tm, D_MODEL), BF16)],
        compiler_params=_cparams(("parallel", "arbitrary")),
        name="inproj",
    )(x, nw, w_main, w_dt)


def _unroll_for(trips):
    return max(u for u in range(1, BLOCK_UNROLL + 1) if trips % u == 0)


def _attn_prompt_kernel(q0, k0, v0, q1, k1, v1, q2, k2, v2, o_ref, kvo0, kvo1, kvo2,
                        kd0, vt0, kd1, vt1, kd2, vt2, lse_ref, acc_ref, s_buf, p_buf, stat_buf):
    n = pl.program_id(2)
    q_refs, k_refs, v_refs = (q0, q1, q2), (k0, k1, k2), (v0, v1, v2)
    kds, vts = (kd0, kd1, kd2), (vt0, vt1, vt2)
    krow = lax.broadcasted_iota(jnp.int32, (BAND, LANES), 0)
    qcol = lax.broadcasted_iota(jnp.int32, (BAND, LANES), 1)
    upper = krow > qcol
    diag = krow == qcol
    head0 = qcol < HEAD_DIM
    half = BAND // 2

    for g, (_, dil) in enumerate(ATTN_GROUPS):
        span = BAND * dil
        nsub = ATTN_TILE // span
        nblk = nsub * dil
        q_ref, k_ref, v_ref, kd, vt = q_refs[g], k_refs[g], v_refs[g], kds[g], vts[g]

        @pl.when(n == 0)
        def _():
            for r in range(dil):
                kd[r * (nsub + 1)] = jnp.zeros((BAND, LANES), BF16)
                vt[r * (nsub + 1)] = jnp.zeros((LANES, BAND), BF16)

        @pl.when(n > 0)
        def _():
            for r in range(dil):
                kd[r * (nsub + 1)] = kd[r * (nsub + 1) + nsub]
                vt[r * (nsub + 1)] = vt[r * (nsub + 1) + nsub]

        def split(i):
            if dil == 1:
                return i, 0
            j = i // dil
            return j, i - j * dil

        def rows_of(i):
            j, r = split(i)
            if dil == 1:
                return pl.ds(i * BAND if isinstance(i, int) else pl.multiple_of(i * BAND, BAND), BAND)
            return pl.ds(j * span + r, BAND, stride=dil)

        def deinterleave(i, c):
            j, r = split(i)
            slot = r * (nsub + 1) + j + 1
            rows = rows_of(i)
            kd[slot] = k_ref[rows, :].astype(BF16)
            vt[slot] = v_ref[rows, :].T.astype(BF16)
            return c

        lax.fori_loop(0, nblk, deinterleave, 0, unroll=_unroll_for(nblk))

        def qk_stage(i, sl):
            j, r = split(i)
            slot = r * (nsub + 1) + j
            q = q_ref[rows_of(i), :] * (ATTN_SCALE * LOG2E)
            q2 = jnp.concatenate([jnp.where(head0, q, 0.0), jnp.where(head0, 0.0, q)], axis=0).astype(BF16)
            kk = kd[pl.ds(slot, 2)].reshape(2 * BAND, LANES)
            s_buf[sl] = _dot_nt(kk, q2)

        def softmax_stage(sl, first):
            for hh in range(2):
                sp = s_buf[sl, 0:BAND, hh * BAND:(hh + 1) * BAND]
                sc = s_buf[sl, BAND:2 * BAND, hh * BAND:(hh + 1) * BAND]
                if first:
                    sp = jnp.where(n > 0, sp, -jnp.inf)
                comb = jnp.where(upper, sp, sc)
                dg = jnp.sum(jnp.where(diag, sp, 0.0), axis=0, keepdims=True)
                m = jnp.maximum(jnp.max(comb, axis=0, keepdims=True), dg)
                p = jnp.exp2(comb - m)
                pd = jnp.exp2(dg - m)
                den = jnp.sum(p, axis=0, keepdims=True) + pd
                p_buf[sl, hh, 0:BAND, :] = jnp.where(upper, p, jnp.where(diag, pd, 0.0)).astype(BF16)
                p_buf[sl, hh, BAND:2 * BAND, :] = jnp.where(upper, 0.0, p).astype(BF16)
                stat_buf[sl, 2 * hh:2 * hh + 1, :] = 1.0 / den
                stat_buf[sl, 2 * hh + 1:2 * hh + 2, :] = m + jnp.log2(den)

        def pv_stage(i, sl):
            j, r = split(i)
            slot = r * (nsub + 1) + j
            rows = rows_of(i)
            vprev = vt[slot]
            vcur = vt[slot + 1]
            ots, lts = [], []
            for hh in range(2):
                vsl = slice(hh * half, (hh + 1) * half)
                vtt = jnp.concatenate([vprev[vsl, :], vcur[vsl, :]], axis=1)
                ots.append(_dot(vtt, p_buf[sl, hh]) * stat_buf[sl, 2 * hh:2 * hh + 1, :])
                lts.append(jnp.broadcast_to(stat_buf[sl, 2 * hh + 1:2 * hh + 2, :], (half, BAND)))
            o_blk = jnp.concatenate(ots, axis=0).T
            lse_blk = jnp.concatenate(lts, axis=0).T
            if g == 0:
                lse_ref[rows, :] = lse_blk
                acc_ref[rows, :] = o_blk
            else:
                lse_run = lse_ref[rows, :]
                acc_run = acc_ref[rows, :]
                mx = jnp.maximum(lse_run, lse_blk)
                ea = jnp.exp2(lse_run - mx)
                eb = jnp.exp2(lse_blk - mx)
                tot = ea + eb
                inv = 1.0 / tot
                merged = (ea * inv) * acc_run + (eb * inv) * o_blk
                if g == len(ATTN_GROUPS) - 1:
                    o_ref[rows, :] = merged
                else:
                    lse_ref[rows, :] = mx + jnp.log2(tot)
                    acc_ref[rows, :] = merged

        npairs = nblk // 2
        firsts_of = lambda pair: (2 * pair < dil, 2 * pair + 1 < dil)

        def qk_pair(pair, par):
            qk_stage(2 * pair, 2 * par)
            qk_stage(2 * pair + 1, 2 * par + 1)

        def softmax_pair(par, firsts):
            softmax_stage(2 * par, firsts[0])
            softmax_stage(2 * par + 1, firsts[1])

        def pv_pair(pair, par):
            pv_stage(2 * pair, 2 * par)
            pv_stage(2 * pair + 1, 2 * par + 1)

        qk_pair(0, 0)
        softmax_pair(0, firsts_of(0))
        qk_pair(1, 1)
        all_first = dil >= nblk

        def steady(t, c, firsts):
            for par in range(2):
                pair = 2 * t + 2 + par
                pv_pair(pair - 2, par)
                softmax_pair(1 - par, firsts[par])
                qk_pair(pair, par)
            return c

        trips = (npairs - 2) // 2
        start = 0
        if any(firsts_of(1)) and not all_first:
            steady(0, 0, (firsts_of(1), firsts_of(2)))
            start = 1
        lax.fori_loop(start, trips, functools.partial(steady, firsts=((all_first,) * 2,) * 2), 0)
        pv_pair(npairs - 2, 0)
        softmax_pair(1, firsts_of(npairs - 1))
        pv_pair(npairs - 1, 1)

    @pl.when(n == pl.num_programs(2) - 1)
    def _():
        for g, (win, _) in enumerate(ATTN_GROUPS):
            for which, ref in enumerate((k_refs[g], v_refs[g])):
                t = ref[ATTN_TILE - win:, :].T
                for hh in range(2):
                    (kvo0, kvo1, kvo2)[g][0, which, hh] = t[hh * HEAD_DIM:(hh + 1) * HEAD_DIM, :]


def _attn_prompt(u, batch, seq):
    nt = seq // ATTN_TILE
    hp = A_WIDTH // LANES
    qkv_blk = 0

    def spec(g, which):
        col = qkv_blk + (3 * g + which) * hp
        return pl.BlockSpec((ATTN_TILE, LANES), lambda b, h, n, col=col: (b * nt + n, col + h))

    in_specs = [spec(g, w) for g in range(3) for w in range(3)]
    scratch = []
    for _, dil in ATTN_GROUPS:
        nslots = dil * (ATTN_TILE // (BAND * dil) + 1)
        scratch += [pltpu.VMEM((nslots, BAND, LANES), BF16), pltpu.VMEM((nslots, LANES, BAND), BF16)]
    scratch += [pltpu.VMEM((ATTN_TILE, LANES), F32), pltpu.VMEM((ATTN_TILE, LANES), F32)]
    scratch += [pltpu.VMEM((4, 2 * BAND, 2 * BAND), F32), pltpu.VMEM((4, 2, 2 * BAND, BAND), BF16),
                pltpu.VMEM((4, HALO, BAND), F32)]
    return pl.pallas_call(
        _attn_prompt_kernel,
        grid=(batch, hp, nt),
        in_specs=in_specs,
        out_specs=[pl.BlockSpec((ATTN_TILE, LANES), lambda b, h, n: (b * nt + n, h))]
        + [pl.BlockSpec((1, 2, 2, HEAD_DIM, w), lambda b, h, n: (b, 0, h, 0, 0)) for w, _ in ATTN_GROUPS],
        out_shape=[jax.ShapeDtypeStruct((batch * seq, A_WIDTH), F32)]
        + [jax.ShapeDtypeStruct((batch, 2, HEADS, HEAD_DIM, w), F32) for w, _ in ATTN_GROUPS],
        scratch_shapes=scratch,
        compiler_params=_cparams(("parallel", "parallel", "arbitrary")),
        name="attn_prompt",
    )(*([u] * 9))


def _conv_silu(pad_ref, w_ref, b_ref, rows):
    acc = b_ref[...] + pad_ref[HALO - (CONV_W - 1):HALO - (CONV_W - 1) + rows, :] * w_ref[0:1, :]
    for i in range(1, CONV_W):
        off = HALO - (CONV_W - 1) + i
        acc = acc + pad_ref[off:off + rows, :] * w_ref[i:i + 1, :]
    return _silu(acc)


def _conv_silu_chunk(x_ref, tail_ref, w_ref, b_ref):
    x = x_ref[...].astype(F32)
    ext = jnp.concatenate([tail_ref[...], x], axis=0)
    tail_ref[...] = x[x.shape[0] - HALO:, :]
    acc = b_ref[...] + x * w_ref[CONV_W - 1:CONV_W, :]
    for k in range(1, CONV_W):
        acc = acc + pltpu.roll(ext, k, 0)[HALO:, :] * w_ref[CONV_W - 1 - k:CONV_W - k, :]
    return _silu(acc)


def _softplus(x):
    return jnp.maximum(x, 0.0) + jnp.log1p(jnp.exp(-jnp.abs(x)))


def _ssd_prompt_kernel(x_ref, b_ref, c_ref, dt_ref, cwx, cwb, cwc, cbx, cbb, cbc,
                       dtb_ref, alog_ref, dexp_ref, e_ref, q_ref, new_ref, c0, c1, c2,
                       y_ref, st_ref, ao_ref, o0, o1, o2,
                       xtail, btail, ctail, state):
    c = pl.program_id(1)
    nc = pl.num_programs(1)

    @pl.when(c == 0)
    def _():
        xtail[...] = jnp.zeros(xtail.shape, F32)
        btail[...] = jnp.zeros(btail.shape, F32)
        ctail[...] = jnp.zeros(ctail.shape, F32)
        state[...] = jnp.zeros(state.shape, F32)

    for hx in range(q_ref.shape[1]):
        _attn_sample_head(hx, q_ref, new_ref, (c0, c1, c2), ao_ref, (o0, o1, o2))

    xc = _conv_silu_chunk(x_ref, xtail, cwx, cbx)
    bm = _conv_silu_chunk(b_ref, btail, cwb, cbb)
    cm = _conv_silu_chunk(c_ref, ctail, cwc, cbc)

    dt = _softplus(dt_ref[...] + dtb_ref[...])
    a = -jnp.exp(alog_ref[...])
    dta = dt * a
    ti = lax.broadcasted_iota(jnp.int32, (CHUNK, CHUNK), 0)
    si = lax.broadcasted_iota(jnp.int32, (CHUNK, CHUNK), 1)
    causal = si <= ti
    tril = jnp.where(causal, 1.0, 0.0).astype(F32)
    cum = jnp.dot(tril, dta, preferred_element_type=F32, precision=lax.Precision.HIGHEST)
    cum2 = cum * LOG2E
    cum2_t = cum2.T
    cum2_last = cum2[CHUNK - 1:CHUNK, :]
    expcum = jnp.exp2(cum2)
    to_end = jnp.exp2(cum2_last - cum2) * dt
    ex = _expand01(jnp.concatenate([expcum, to_end, dt], axis=0), e_ref[...])
    expcum_x = ex[0:CHUNK, :]
    to_end_x = ex[CHUNK:2 * CHUNK, :]
    dt_x = ex[2 * CHUNK:3 * CHUNK, :]
    dec_x = expcum_x[CHUNK - 1:CHUNK, :]

    lane = lax.broadcasted_iota(jnp.int32, (CHUNK, LANES), 1)
    head0 = lane < SSD_HEAD_DIM
    hpg = SSD_HEADS // SSD_GROUPS
    for g in range(SSD_GROUPS):
        gs = slice(g * SSD_GROUP_W, (g + 1) * SSD_GROUP_W)
        ns = slice(g * SSD_STATE, (g + 1) * SSD_STATE)
        bg = bm[:, ns].astype(BF16)
        cg = cm[:, ns].astype(BF16)
        cb = _dot_nt(cg, bg)
        xg = xc[:, gs]
        xdt = xg * dt_x[:, gs]
        ys = []
        for pr in range(hpg // 2):
            ws = []
            for hh in range(2):
                h = g * hpg + 2 * pr + hh
                colb = jnp.broadcast_to(cum2[:, h:h + 1], (CHUNK, CHUNK))
                rowb = jnp.broadcast_to(cum2_t[h:h + 1, :], (CHUNK, CHUNK))
                decay = jnp.exp2(jnp.where(causal, colb - rowb, -jnp.inf))
                ws.append((cb * decay).astype(BF16))
            w2 = jnp.concatenate(ws, axis=1)
            xp = xdt[:, pr * LANES:(pr + 1) * LANES]
            x2 = jnp.concatenate([jnp.where(head0, xp, 0.0), jnp.where(head0, 0.0, xp)], axis=0).astype(BF16)
            ys.append(_dot(w2, x2))
        st = state[g]
        y_state = _dot(cg, st.astype(BF16))
        y_ref[:, gs] = jnp.concatenate(ys, axis=1) + y_state * expcum_x[:, gs] + dexp_ref[:, gs] * xg
        xs = (xg * to_end_x[:, gs]).astype(BF16)
        state[g] = dec_x[:, gs] * st + _dot_tn(bg, xs)

    @pl.when(c == nc - 1)
    def _():
        for g in range(SSD_GROUPS):
            st_ref[0, g * hpg:(g + 1) * hpg, :, :] = state[g].T.reshape(hpg, SSD_HEAD_DIM, SSD_STATE)


def _ssd_prompt(u, dt_raw, conv_w, conv_b, dt_bias, a_log, d_exp, e_mat, batch, seq, q_s, new_t, caches):
    nc = seq // CHUNK
    cwx, cwb, cwc = conv_w[:, :D_INNER], conv_w[:, D_INNER:D_INNER + SSD_GROUP_W], conv_w[:, D_INNER + SSD_GROUP_W:]
    cbx, cbb, cbc = conv_b[:, :D_INNER], conv_b[:, D_INNER:D_INNER + SSD_GROUP_W], conv_b[:, D_INNER + SSD_GROUP_W:]
    full = lambda shape: pl.BlockSpec(shape, lambda b, c: (0,) * len(shape))
    dbatch, _, _, n_new, _ = q_s.shape
    steps = batch * nc
    assert (dbatch * HEADS) % steps == 0, "sample heads must spread evenly over the prompt's SSD steps"
    hps = dbatch * HEADS // steps
    assert HEADS % hps == 0
    spb = HEADS // hps
    sb = lambda b, c: (b * nc + c) // spb
    hb = lambda b, c: (b * nc + c) % spb
    q_spec = pl.BlockSpec((1, hps, len(ATTN_GROUPS), n_new, HEAD_DIM), lambda b, c: (sb(b, c), hb(b, c), 0, 0, 0))
    new_spec = pl.BlockSpec((1, hps, HEAD_DIM, LANES), lambda b, c: (sb(b, c), hb(b, c), 0, 0))
    ao_spec = pl.BlockSpec((1, hps, n_new, HEAD_DIM), lambda b, c: (sb(b, c), hb(b, c), 0, 0))
    cache_specs = [pl.BlockSpec((1, 2, hps, HEAD_DIM, w), lambda b, c: (sb(b, c), 0, hb(b, c), 0, 0))
                   for w, _ in ATTN_GROUPS]
    return pl.pallas_call(
        _ssd_prompt_kernel,
        grid=(batch, nc),
        in_specs=[
            pl.BlockSpec((CHUNK, D_INNER), lambda b, c: (b * nc + c, X_OFF // D_INNER)),
            pl.BlockSpec((CHUNK, SSD_GROUP_W), lambda b, c: (b * nc + c, B_OFF // SSD_GROUP_W)),
            pl.BlockSpec((CHUNK, SSD_GROUP_W), lambda b, c: (b * nc + c, C_OFF // SSD_GROUP_W)),
            pl.BlockSpec((CHUNK, LANES), lambda b, c: (b * nc + c, 0)),
            full((CONV_W, D_INNER)), full((CONV_W, SSD_GROUP_W)), full((CONV_W, SSD_GROUP_W)),
            full((1, D_INNER)), full((1, SSD_GROUP_W)), full((1, SSD_GROUP_W)),
            full((1, LANES)), full((1, LANES)), full((1, D_INNER)), full((LANES, D_INNER)),
            q_spec, new_spec,
        ] + cache_specs,
        out_specs=[
            pl.BlockSpec((CHUNK, D_INNER), lambda b, c: (b * nc + c, 0)),
            pl.BlockSpec((1, SSD_HEADS, SSD_HEAD_DIM, SSD_STATE), lambda b, c: (b, 0, 0, 0)),
            ao_spec,
        ] + cache_specs,
        out_shape=[
            jax.ShapeDtypeStruct((batch * seq, D_INNER), F32),
            jax.ShapeDtypeStruct((batch, SSD_HEADS, SSD_HEAD_DIM, SSD_STATE), F32),
            jax.ShapeDtypeStruct((dbatch, HEADS, n_new, HEAD_DIM), F32),
        ] + [jax.ShapeDtypeStruct(cc.shape, F32) for cc in caches],
        scratch_shapes=[
            pltpu.VMEM((HALO, D_INNER), F32),
            pltpu.VMEM((HALO, SSD_GROUP_W), F32),
            pltpu.VMEM((HALO, SSD_GROUP_W), F32),
            pltpu.VMEM((SSD_GROUPS, SSD_STATE, SSD_GROUP_W), F32),
        ],
        compiler_params=_cparams(("parallel", "arbitrary")),
        name="ssd_prompt",
    )(u, u, u, dt_raw, cwx, cwb, cwc, cbx, cbb, cbc, dt_bias, a_log, d_exp, e_mat, q_s, new_t, *caches)


def _mix_ffn_kernel(a_ref, y_ref, z_ref, ga_ref, gb_ref, x_ref, wa_ref, ws_ref, wo_ref,
                    sn_ref, pn_ref, wu_ref, wd_ref, n1_ref, n2_ref, o_ref):
    a_proj = _dot(a_ref[...].astype(BF16), wa_ref[...])
    g = y_ref[...] * _silu(z_ref[...].astype(F32))
    parts = []
    for i in range(SSD_GROUPS):
        gi = g[:, i * SSD_GROUP_W:(i + 1) * SSD_GROUP_W]
        parts.append(gi * lax.rsqrt(jnp.mean(gi * gi, axis=-1, keepdims=True) + RMS_EPS))
    yn = (jnp.concatenate(parts, axis=1) * sn_ref[...]).astype(BF16)
    b_proj = _dot(yn, ws_ref[...])
    mixed_in = _sigmoid(ga_ref[...].astype(F32)) * a_proj + _sigmoid(gb_ref[...].astype(F32)) * b_proj
    mixed = _dot(mixed_in.astype(BF16), wo_ref[...])
    x1 = x_ref[...] + _rms(mixed, pn_ref[...])
    h = _rms(x1, n1_ref[...]).astype(BF16)
    gu = _dot(h, wu_ref[...])
    act = (_silu(gu[:, :D_FF]) * gu[:, D_FF:]).astype(BF16)
    f = _dot(act, wd_ref[...])
    o_ref[...] = x1 + _rms(f, n2_ref[...])


def _mix_ffn(a_out, y, u, x, w_attn_out, w_ssd_out, w_out, ssd_norm, post_mix_norm,
             w_up, w_down, pre_norm, post_norm, tm):
    t = x.shape[0]
    const = lambda shape: pl.BlockSpec(shape, lambda i: (0, 0), pipeline_mode=pl.Buffered(1))
    return pl.pallas_call(
        _mix_ffn_kernel,
        grid=(t // tm,),
        in_specs=[
            pl.BlockSpec((tm, A_WIDTH), lambda i: (i, 0)),
            pl.BlockSpec((tm, D_INNER), lambda i: (i, 0)),
            pl.BlockSpec((tm, D_INNER), lambda i: (i, Z_OFF // D_INNER)),
            pl.BlockSpec((tm, D_MODEL), lambda i: (i, GATE_OFF // D_MODEL)),
            pl.BlockSpec((tm, D_MODEL), lambda i: (i, GATE_OFF // D_MODEL + 1)),
            pl.BlockSpec((tm, D_MODEL), lambda i: (i, 0)),
            const((A_WIDTH, D_MODEL)), const((D_INNER, D_MODEL)), const((D_MODEL, D_MODEL)),
            const((1, D_INNER)), const((1, D_MODEL)),
            const((D_MODEL, 2 * D_FF)), const((D_FF, D_MODEL)),
            const((1, D_MODEL)), const((1, D_MODEL)),
        ],
        out_specs=pl.BlockSpec((tm, D_MODEL), lambda i: (i, 0)),
        out_shape=jax.ShapeDtypeStruct((t, D_MODEL), F32),
        compiler_params=_cparams(("parallel",)),
        name="mix_ffn",
    )(a_out, y, u, u, u, x, w_attn_out, w_ssd_out, w_out, ssd_norm, post_mix_norm,
      w_up, w_down, pre_norm, post_norm)


def _attn_sample_head(hx, q_ref, new_ref, caches, ao_ref, outs):
    n_new = q_ref.shape[3]
    new_t = new_ref[0, hx]
    m_run = l_run = o_run = None
    for g, (win, dil) in enumerate(ATTN_GROUPS):
        q = (q_ref[0, hx, g] * ATTN_SCALE).astype(BF16)
        k_t = caches[g][0, 0, hx]
        v_t = caches[g][0, 1, hx]
        knt = pltpu.roll(new_t, LANES - n_new - (2 * g) * n_new, 1)
        vnt = pltpu.roll(new_t, LANES - n_new - (2 * g + 1) * n_new, 1)
        qi = lax.broadcasted_iota(jnp.int32, (n_new, win), 0)
        ci = lax.broadcasted_iota(jnp.int32, (n_new, win), 1)
        ok_c = jnp.logical_and(ci >= qi, ((ci - qi) & (dil - 1)) == 0)
        qn = lax.broadcasted_iota(jnp.int32, (n_new, LANES), 0)
        nn = lax.broadcasted_iota(jnp.int32, (n_new, LANES), 1) - (LANES - n_new)
        ok_n = jnp.logical_and(jnp.logical_and(nn >= 0, nn <= qn), ((qn - nn) & (dil - 1)) == 0)
        s_c = jnp.where(ok_c, _dot(q, k_t.astype(BF16)), -jnp.inf)
        s_n = jnp.where(ok_n, _dot(q, knt.astype(BF16)), -jnp.inf)
        m = jnp.maximum(jnp.max(s_c, axis=1, keepdims=True), jnp.max(s_n, axis=1, keepdims=True))
        p_c = jnp.exp(s_c - m)
        p_n = jnp.exp(s_n - m)
        den = jnp.sum(p_c, axis=1, keepdims=True) + jnp.sum(p_n, axis=1, keepdims=True)
        o = _dot_nt(p_c.astype(BF16), v_t.astype(BF16)) + _dot_nt(p_n.astype(BF16), vnt.astype(BF16))
        if g == 0:
            m_run, l_run, o_run = m, den, o
        else:
            m_new = jnp.maximum(m_run, m)
            ea = jnp.exp(m_run - m_new)
            eb = jnp.exp(m - m_new)
            l_run = ea * l_run + eb * den
            o_run = ea * o_run + eb * o
            m_run = m_new
        lane = lax.broadcasted_iota(jnp.int32, (HEAD_DIM, LANES), 1)
        keep = lane < LANES - n_new
        for which, (src, new) in enumerate(((k_t, knt), (v_t, vnt))):
            rolled = pltpu.roll(src, win - n_new, 1)
            if win > LANES:
                outs[g][0, which, hx, :, 0:win - LANES] = rolled[:, 0:win - LANES]
            outs[g][0, which, hx, :, win - LANES:win] = jnp.where(keep, rolled[:, win - LANES:win], new)
    ao_ref[0, hx] = o_run * (1.0 / l_run)


def _ssd_sample_kernel(x_ref, b_ref, c_ref, dt_ref, sx_ref, sb_ref, sc_ref, st_in,
                       cwx, cwb, cwc, cbx, cbb, cbc, dtb_ref, alog_ref, dexp_ref, e_ref,
                       y_ref, st_out, xpad, bpad, cpad):
    n = x_ref.shape[0]
    keep = CONV_W - 1
    for pad, new, old in ((xpad, x_ref, sx_ref), (bpad, b_ref, sb_ref), (cpad, c_ref, sc_ref)):
        pad[0:HALO, :] = jnp.zeros((HALO, pad.shape[1]), F32)
        pad[HALO - keep:HALO, :] = old[0]
        pad[HALO:HALO + n, :] = new[...]
    xc = _conv_silu(xpad, cwx, cbx, n)
    bm = _conv_silu(bpad, cwb, cbb, n)
    cm = _conv_silu(cpad, cwc, cbc, n)

    dt = _softplus(dt_ref[...] + dtb_ref[...])
    a = -jnp.exp(alog_ref[...])
    dta = dt * a
    ti = lax.broadcasted_iota(jnp.int32, (n, n), 0)
    si = lax.broadcasted_iota(jnp.int32, (n, n), 1)
    tril = jnp.where(si <= ti, 1.0, 0.0).astype(F32)
    cum = jnp.dot(tril, dta, preferred_element_type=F32, precision=lax.Precision.HIGHEST)
    cum_last = cum[n - 1:n, :]
    expcum = jnp.exp(cum)
    to_end = jnp.exp(cum_last - cum) * dt

    hpg = SSD_HEADS // SSD_GROUPS
    lane = lax.broadcasted_iota(jnp.int32, (n, LANES), 1)
    trow = lax.broadcasted_iota(jnp.int32, (n, LANES), 0)
    cbs = [_dot_nt(cm[:, g * SSD_STATE:(g + 1) * SSD_STATE].astype(BF16),
                   bm[:, g * SSD_STATE:(g + 1) * SSD_STATE].astype(BF16)) for g in range(SSD_GROUPS)]
    coefs = []
    for s in range(n):
        cbh = jnp.zeros((n, LANES), F32)
        for g in range(SSD_GROUPS):
            in_g = jnp.logical_and(lane >= g * hpg, lane < (g + 1) * hpg)
            cbh = jnp.where(in_g, jnp.broadcast_to(cbs[g][:, s:s + 1], (n, LANES)), cbh)
        decay = jnp.exp(jnp.where(trow >= s, cum - cum[s:s + 1, :], -jnp.inf))
        coefs.append(cbh * decay * dt[s:s + 1, :])
    ex = _expand01(jnp.concatenate(coefs + [expcum, to_end], axis=0), e_ref[...])
    y = dexp_ref[...] * xc
    for s in range(n):
        y = y + ex[s * n:(s + 1) * n, :] * xc[s:s + 1, :]
    expcum_x = ex[n * n:n * n + n, :]
    to_end_x = ex[n * n + n:n * n + 2 * n, :]
    xs = xc * to_end_x
    dec_t = jnp.broadcast_to(expcum[n - 1:n, :], (LANES, LANES)).T
    ys = []
    for g in range(SSD_GROUPS):
        gs = slice(g * SSD_GROUP_W, (g + 1) * SSD_GROUP_W)
        ns = slice(g * SSD_STATE, (g + 1) * SSD_STATE)
        st = st_in[0, g * hpg:(g + 1) * hpg].reshape(SSD_GROUP_W, SSD_STATE)
        ys.append(_dot_nt(cm[:, ns].astype(BF16), st.astype(BF16)))
        upd = _dot_tn(xs[:, gs].astype(BF16), bm[:, ns].astype(BF16))
        for hh in range(hpg):
            h = g * hpg + hh
            rows = slice(hh * SSD_HEAD_DIM, (hh + 1) * SSD_HEAD_DIM)
            st_out[0, h] = dec_t[h:h + 1, :] * st[rows, :] + upd[rows, :]
    y_ref[...] = y + jnp.concatenate(ys, axis=1) * expcum_x


def _ssd_sample(u, dt_raw, conv_state, ssm_state, conv_w, conv_b, dt_bias, a_log, d_exp, e_mat, n_new):
    bsz = ssm_state.shape[0]
    keep = CONV_W - 1
    cwx, cwb, cwc = conv_w[:, :D_INNER], conv_w[:, D_INNER:D_INNER + SSD_GROUP_W], conv_w[:, D_INNER + SSD_GROUP_W:]
    cbx, cbb, cbc = conv_b[:, :D_INNER], conv_b[:, D_INNER:D_INNER + SSD_GROUP_W], conv_b[:, D_INNER + SSD_GROUP_W:]
    full = lambda shape: pl.BlockSpec(shape, lambda b: (0,) * len(shape))
    st_spec = pl.BlockSpec((1, SSD_HEADS, SSD_HEAD_DIM, SSD_STATE), lambda b: (b, 0, 0, 0))
    return pl.pallas_call(
        _ssd_sample_kernel,
        grid=(bsz,),
        in_specs=[
            pl.BlockSpec((n_new, D_INNER), lambda b: (b, X_OFF // D_INNER)),
            pl.BlockSpec((n_new, SSD_GROUP_W), lambda b: (b, B_OFF // SSD_GROUP_W)),
            pl.BlockSpec((n_new, SSD_GROUP_W), lambda b: (b, C_OFF // SSD_GROUP_W)),
            pl.BlockSpec((n_new, LANES), lambda b: (b, 0)),
            pl.BlockSpec((1, keep, D_INNER), lambda b: (b, 0, 0)),
            pl.BlockSpec((1, keep, SSD_GROUP_W), lambda b: (b, 0, D_INNER // SSD_GROUP_W)),
            pl.BlockSpec((1, keep, SSD_GROUP_W), lambda b: (b, 0, D_INNER // SSD_GROUP_W + 1)),
            st_spec,
            full((CONV_W, D_INNER)), full((CONV_W, SSD_GROUP_W)), full((CONV_W, SSD_GROUP_W)),
            full((1, D_INNER)), full((1, SSD_GROUP_W)), full((1, SSD_GROUP_W)),
            full((1, LANES)), full((1, LANES)), full((1, D_INNER)), full((LANES, D_INNER)),
        ],
        out_specs=[pl.BlockSpec((n_new, D_INNER), lambda b: (b, 0)), st_spec],
        out_shape=[
            jax.ShapeDtypeStruct((bsz * n_new, D_INNER), F32),
            jax.ShapeDtypeStruct(ssm_state.shape, F32),
        ],
        scratch_shapes=[
            pltpu.VMEM((HALO + n_new, D_INNER), F32),
            pltpu.VMEM((HALO + n_new, SSD_GROUP_W), F32),
            pltpu.VMEM((HALO + n_new, SSD_GROUP_W), F32),
        ],
        compiler_params=_cparams(("parallel",)),
        name="ssd_sample",
    )(u, u, u, dt_raw, conv_state, conv_state, conv_state, ssm_state,
      cwx, cwb, cwc, cbx, cbb, cbc, dt_bias, a_log, d_exp, e_mat)


def _prep_weights(w_in, dt_bias, d_skip):
    q_end = 3 * len(ATTN_GROUPS) * A_WIDTH
    z_end = q_end + D_INNER
    x_end = z_end + D_INNER
    b_end = x_end + SSD_GROUP_W
    c_end = b_end + SSD_GROUP_W
    dt_end = c_end + SSD_HEADS
    w_main = jnp.concatenate(
        [w_in[:, q_end:z_end], w_in[:, dt_end:], w_in[:, z_end:c_end], w_in[:, :q_end]], axis=1).astype(BF16)
    w_dt = jnp.pad(w_in[:, c_end:dt_end], ((0, 0), (0, LANES - SSD_HEADS))).astype(BF16)
    dtb = jnp.pad(dt_bias, (0, LANES - SSD_HEADS)).reshape(1, LANES)
    d_exp = jnp.repeat(d_skip, SSD_HEAD_DIM).reshape(1, D_INNER)
    return w_main, w_dt, dtb, d_exp


def _tail_conv(u, batch, seq):
    u3 = u.reshape(batch, seq, N_REST)
    return u3[:, seq - (CONV_W - 1):, X_OFF:C_OFF + SSD_GROUP_W].astype(F32)


def kernel(x_prompt, x_sample, cache_kv_w128, cache_kv_w512, cache_kv_w2048, state_conv, state_ssm,
           pre_mix_norm, w_in, conv_w, conv_b, dt_bias, a_log, d_skip, ssd_norm, w_attn_out, w_ssd_out,
           w_out, post_mix_norm, pre_ffn_norm, w_up, w_down, post_ffn_norm):
    batch, seq, _ = x_prompt.shape
    dbatch, dseq, _ = x_sample.shape
    assert w_in.shape[0] == 1, "single-layer trunk"
    assert seq % ATTN_TILE == 0 and dseq == HALO

    w_main, w_dt, dtb, d_exp = _prep_weights(w_in[0], dt_bias[0], d_skip[0])
    alog = jnp.pad(a_log[0], (0, LANES - SSD_HEADS)).reshape(1, LANES)
    e_mat = (jnp.arange(LANES)[:, None] == (jnp.arange(D_INNER)[None, :] // SSD_HEAD_DIM)).astype(BF16)
    nw = pre_mix_norm[0].reshape(1, D_MODEL)
    cw, cb = conv_w[0], conv_b[0].reshape(1, -1)
    wa, ws, wo = w_attn_out[0].astype(BF16), w_ssd_out[0].astype(BF16), w_out[0].astype(BF16)
    wu, wd = w_up[0].astype(BF16), w_down[0].astype(BF16)
    sn = ssd_norm[0].reshape(1, D_INNER)
    pmn, pfn, qfn = (post_mix_norm[0].reshape(1, -1), pre_ffn_norm[0].reshape(1, -1),
                     post_ffn_norm[0].reshape(1, -1))

    xp = x_prompt.reshape(batch * seq, D_MODEL)
    xs = x_sample.reshape(dbatch * dseq, D_MODEL)
    ur_p, uq_p, dt_p = _inproj(xp, nw, w_main, w_dt, min(2048, batch * seq), 512)
    ur_s, uq_s, dt_s = _inproj(xs, nw, w_main, w_dt, min(1024, dbatch * dseq), 512)
    qkv = uq_s.reshape(dbatch, dseq, 3, 3, HEADS, HEAD_DIM)
    q_s = qkv[:, :, :, 0].transpose(0, 3, 2, 1, 4)
    new_t = qkv[:, :, :, 1:3].transpose(0, 4, 5, 2, 3, 1).reshape(dbatch, HEADS, HEAD_DIM, 6 * dseq)
    new_t = jnp.pad(new_t, ((0, 0), (0, 0), (0, 0), (0, LANES - 6 * dseq)))
    caches = [c[0].transpose(0, 2, 3, 4, 1) for c in (cache_kv_w128, cache_kv_w512, cache_kv_w2048)]

    a_p, *kvt_p = _attn_prompt(uq_p, batch, seq)
    y_p, ssm_p, ao, nc0, nc1, nc2 = _ssd_prompt(ur_p, dt_p, cw, cb, dtb, alog, d_exp, e_mat, batch, seq,
                                                q_s, new_t, caches)
    out_p = _mix_ffn(a_p, y_p, ur_p, xp, wa, ws, wo, sn, pmn, wu, wd, pfn, qfn, 256).reshape(batch, seq, D_MODEL)
    kv_p = [t.transpose(0, 4, 1, 2, 3)[None] for t in kvt_p]
    conv_p = _tail_conv(ur_p, batch, seq)[None]

    a_s = ao.transpose(0, 2, 1, 3).reshape(dbatch * dseq, A_WIDTH)
    kv_s = [c.transpose(0, 4, 1, 2, 3)[None] for c in (nc0, nc1, nc2)]
    y_s, ssm_s = _ssd_sample(ur_s.astype(F32), dt_s, state_conv[0], state_ssm[0], cw, cb, dtb, alog, d_exp,
                             e_mat, dseq)
    tm_s = min(256, dbatch * dseq)
    out_s = _mix_ffn(a_s, y_s, ur_s, xs, wa, ws, wo, sn, pmn, wu, wd, pfn, qfn, tm_s).reshape(dbatch, dseq, D_MODEL)
    conv_s = _tail_conv(ur_s, dbatch, dseq)[None]

    return (out_p, out_s, kv_p[0], kv_p[1], kv_p[2], conv_p, ssm_p[None],
            kv_s[0], kv_s[1], kv_s[2], conv_s, ssm_s[None])
```

```python
import functools

import jax
import jax.numpy as jnp
from jax import lax
from jax.experimental import pallas as pl
from jax.experimental.pallas import tpu as pltpu

F32 = jnp.float32
BF16 = jnp.bfloat16

D_MODEL = 1024
ATTN_GROUPS = ((128, 1), (512, 4), (2048, 16))
BAND = 128
HEADS = 8
HEAD_DIM = 64
A_WIDTH = HEADS * HEAD_DIM
ATTN_SCALE = HEAD_DIM ** -0.5
LOG2E = 1.4426950408889634
ATTN_TILE = 2048
BLOCK_UNROLL = 4

D_INNER = 2048
SSD_HEADS = 32
SSD_HEAD_DIM = 64
SSD_GROUPS = 4
SSD_GROUP_W = D_INNER // SSD_GROUPS
SSD_STATE = 128
CONV_W = 4
CHUNK = 128
D_FF = 2816
RMS_EPS = 1e-6
LANES = 128
HALO = 8

Z_OFF = 0
GATE_OFF = 2048
X_OFF = 4096
B_OFF = 6144
C_OFF = 6656
N_REST = 7168
N_QKV = 3 * len(ATTN_GROUPS) * A_WIDTH
N_MAIN = N_REST + N_QKV

VMEM_LIMIT = 56 * 1024 * 1024


def _cparams(sem):
    return pltpu.CompilerParams(dimension_semantics=sem, vmem_limit_bytes=VMEM_LIMIT)


def _rms(x, w):
    return x * lax.rsqrt(jnp.mean(x * x, axis=-1, keepdims=True) + RMS_EPS) * w


def _silu(x):
    h = 0.5 * x
    return h + h * jnp.tanh(h)


def _sigmoid(x):
    return 0.5 + 0.5 * jnp.tanh(0.5 * x)


def _dot(a, b):
    return jnp.dot(a, b, preferred_element_type=F32)


def _dot_nt(a, b):
    return lax.dot_general(a, b, (((1,), (1,)), ((), ())), preferred_element_type=F32)


def _dot_tn(a, b):
    return lax.dot_general(a, b, (((0,), (0,)), ((), ())), preferred_element_type=F32)


def _expand01(a, e):
    hi = a.astype(BF16)
    mid = (a - hi.astype(F32)).astype(BF16)
    return _dot(hi, e) + _dot(mid, e)


def _inproj_kernel(x_ref, nw_ref, w_ref, wdt_ref, r_ref, q_ref, dt_ref, h_ref, *, rest_tiles):
    j = pl.program_id(1)

    @pl.when(j == 0)
    def _():
        hb = _rms(x_ref[...], nw_ref[...]).astype(BF16)
        h_ref[...] = hb
        dt_ref[...] = _dot(hb, wdt_ref[...])

    @pl.when(j < rest_tiles)
    def _():
        r_ref[...] = _dot(h_ref[...], w_ref[...]).astype(r_ref.dtype)

    @pl.when(j >= rest_tiles)
    def _():
        q_ref[...] = _dot(h_ref[...], w_ref[...])


def _inproj(x, nw, w_main, w_dt, tm, tn, rest_dtype):
    t = x.shape[0]
    rest_tiles = N_REST // tn
    return pl.pallas_call(
        functools.partial(_inproj_kernel, rest_tiles=rest_tiles),
        grid=(t // tm, N_MAIN // tn),
        in_specs=[
            pl.BlockSpec((tm, D_MODEL), lambda i, j: (i, 0)),
            pl.BlockSpec((1, D_MODEL), lambda i, j: (0, 0)),
            pl.BlockSpec((D_MODEL, tn), lambda i, j: (0, j)),
            pl.BlockSpec((D_MODEL, LANES), lambda i, j: (0, 0)),
        ],
        out_specs=[
            pl.BlockSpec((tm, tn), lambda i, j: (i, jnp.minimum(j, rest_tiles - 1))),
            pl.BlockSpec((tm, tn), lambda i, j: (i, jnp.maximum(j - rest_tiles, 0))),
            pl.BlockSpec((tm, LANES), lambda i, j: (i, 0)),
        ],
        out_shape=[
            jax.ShapeDtypeStruct((t, N_REST), rest_dtype),
            jax.ShapeDtypeStruct((t, N_QKV), F32),
            jax.ShapeDtypeStruct((t, LANES), F32),
        ],
        scratch_shapes=[pltpu.VMEM((tm, D_MODEL), BF16)],
        compiler_params=_cparams(("parallel", "arbitrary")),
        name="inproj",
    )(x, nw, w_main, w_dt)


def _unroll_for(trips):
    return max(u for u in range(1, BLOCK_UNROLL + 1) if trips % u == 0)


def _attn_prompt_kernel(q0, k0, v0, q1, k1, v1, q2, k2, v2, o_ref, kvo0, kvo1, kvo2,
                        kd0, vt0, kd1, vt1, kd2, vt2, lse_ref, acc_ref, s_buf, p_buf, stat_buf):
    n = pl.program_id(2)
    q_refs, k_refs, v_refs = (q0, q1, q2), (k0, k1, k2), (v0, v1, v2)
    kds, vts = (kd0, kd1, kd2), (vt0, vt1, vt2)
    krow = lax.broadcasted_iota(jnp.int32, (BAND, LANES), 0)
    qcol = lax.broadcasted_iota(jnp.int32, (BAND, LANES), 1)
    upper = krow > qcol
    diag = krow == qcol
    head0 = qcol < HEAD_DIM
    half = BAND // 2

    def group_stages(g):
        dil = ATTN_GROUPS[g][1]
        span = BAND * dil
        nsub = ATTN_TILE // span
        nblk = nsub * dil
        q_ref, k_ref, v_ref, kd, vt = q_refs[g], k_refs[g], v_refs[g], kds[g], vts[g]

        def split(i):
            if dil == 1:
                return i, 0
            j = i // dil
            return j, i - j * dil

        def rows_of(i):
            j, r = split(i)
            if dil == 1:
                return pl.ds(i * BAND if isinstance(i, int) else pl.multiple_of(i * BAND, BAND), BAND)
            return pl.ds(j * span + r, BAND, stride=dil)

        def carry():
            @pl.when(n == 0)
            def _():
                for r in range(dil):
                    kd[r * (nsub + 1)] = jnp.zeros((BAND, LANES), BF16)
                    vt[r * (nsub + 1)] = jnp.zeros((LANES, BAND), BF16)

            @pl.when(n > 0)
            def _():
                for r in range(dil):
                    kd[r * (nsub + 1)] = kd[r * (nsub + 1) + nsub]
                    vt[r * (nsub + 1)] = vt[r * (nsub + 1) + nsub]

        def deinterleave():
            for i in range(nblk):
                j, r = split(i)
                slot = r * (nsub + 1) + j + 1
                rows = rows_of(i)
                kd[slot] = k_ref[rows, :].astype(BF16)
                vt[slot] = v_ref[rows, :].T.astype(BF16)

        def qk_stage(i, sl):
            j, r = split(i)
            slot = r * (nsub + 1) + j
            q = q_ref[rows_of(i), :] * (ATTN_SCALE * LOG2E)
            q2 = jnp.concatenate([jnp.where(head0, q, 0.0), jnp.where(head0, 0.0, q)], axis=0).astype(BF16)
            kk = kd[pl.ds(slot, 2)].reshape(2 * BAND, LANES)
            s_buf[sl] = _dot_nt(kk, q2)

        def softmax_stage(sl, first):
            for hh in range(2):
                sp = s_buf[sl, 0:BAND, hh * BAND:(hh + 1) * BAND]
                sc = s_buf[sl, BAND:2 * BAND, hh * BAND:(hh + 1) * BAND]
                if first:
                    sp = jnp.where(n > 0, sp, -jnp.inf)
                comb = jnp.where(upper, sp, sc)
                dg = jnp.sum(jnp.where(diag, sp, 0.0), axis=0, keepdims=True)
                m = jnp.maximum(jnp.max(comb, axis=0, keepdims=True), dg)
                p = jnp.exp2(comb - m)
                pd = jnp.exp2(dg - m)
                den = jnp.sum(p, axis=0, keepdims=True) + pd
                p_buf[sl, hh, 0:BAND, :] = jnp.where(upper, p, jnp.where(diag, pd, 0.0)).astype(BF16)
                p_buf[sl, hh, BAND:2 * BAND, :] = jnp.where(upper, 0.0, p).astype(BF16)
                stat_buf[sl, 2 * hh:2 * hh + 1, :] = 1.0 / den
                stat_buf[sl, 2 * hh + 1:2 * hh + 2, :] = m + jnp.log2(den)

        def pv_stage(i, sl):
            j, r = split(i)
            slot = r * (nsub + 1) + j
            rows = rows_of(i)
            vprev = vt[slot]
            vcur = vt[slot + 1]
            ots, lts = [], []
            for hh in range(2):
                vsl = slice(hh * half, (hh + 1) * half)
                vtt = jnp.concatenate([vprev[vsl, :], vcur[vsl, :]], axis=1)
                ots.append(_dot(vtt, p_buf[sl, hh]) * stat_buf[sl, 2 * hh:2 * hh + 1, :])
                lts.append(jnp.broadcast_to(stat_buf[sl, 2 * hh + 1:2 * hh + 2, :], (half, BAND)))
            o_blk = jnp.concatenate(ots, axis=0).T
            lse_blk = jnp.concatenate(lts, axis=0).T
            if g == 0:
                lse_ref[rows, :] = lse_blk
                acc_ref[rows, :] = o_blk
            else:
                lse_run = lse_ref[rows, :]
                acc_run = acc_ref[rows, :]
                mx = jnp.maximum(lse_run, lse_blk)
                ea = jnp.exp2(lse_run - mx)
                eb = jnp.exp2(lse_blk - mx)
                tot = ea + eb
                inv = 1.0 / tot
                merged = (ea * inv) * acc_run + (eb * inv) * o_blk
                if g == len(ATTN_GROUPS) - 1:
                    o_ref[rows, :] = merged
                else:
                    lse_ref[rows, :] = mx + jnp.log2(tot)
                    acc_ref[rows, :] = merged

        def qk_pair(pair, par):
            qk_stage(2 * pair, 2 * par)
            qk_stage(2 * pair + 1, 2 * par + 1)

        def softmax_pair(pair, par):
            softmax_stage(2 * par, 2 * pair < dil)
            softmax_stage(2 * par + 1, 2 * pair + 1 < dil)

        def pv_pair(pair, par):
            pv_stage(2 * pair, 2 * par)
            pv_stage(2 * pair + 1, 2 * par + 1)

        return (carry, deinterleave), (qk_pair, softmax_pair, pv_pair), nblk // 2

    groups = [group_stages(g) for g in range(len(ATTN_GROUPS))]
    for (carry, _), _, _ in groups:
        carry()
    for (_, deinterleave), _, _ in groups:
        deinterleave()
    items = [(stages, pair) for _, stages, npairs in groups for pair in range(npairs)]
    for t in range(len(items) + 2):
        for depth in (2, 1, 0):
            if 0 <= t - depth < len(items):
                stages, pair = items[t - depth]
                stages[depth](pair, (t - depth) % 2)

    @pl.when(n == pl.num_programs(2) - 1)
    def _():
        for g, (win, _) in enumerate(ATTN_GROUPS):
            for which, ref in enumerate((k_refs[g], v_refs[g])):
                t = ref[ATTN_TILE - win:, :].T
                for hh in range(2):
                    (kvo0, kvo1, kvo2)[g][0, which, hh] = t[hh * HEAD_DIM:(hh + 1) * HEAD_DIM, :]


def _attn_prompt(u, batch, seq):
    nt = seq // ATTN_TILE
    hp = A_WIDTH // LANES
    qkv_blk = 0

    def spec(g, which):
        col = qkv_blk + (3 * g + which) * hp
        return pl.BlockSpec((ATTN_TILE, LANES), lambda b, h, n, col=col: (b * nt + n, col + h))

    in_specs = [spec(g, w) for g in range(3) for w in range(3)]
    scratch = []
    for _, dil in ATTN_GROUPS:
        nslots = dil * (ATTN_TILE // (BAND * dil) + 1)
        scratch += [pltpu.VMEM((nslots, BAND, LANES), BF16), pltpu.VMEM((nslots, LANES, BAND), BF16)]
    scratch += [pltpu.VMEM((ATTN_TILE, LANES), F32), pltpu.VMEM((ATTN_TILE, LANES), F32)]
    scratch += [pltpu.VMEM((4, 2 * BAND, 2 * BAND), F32), pltpu.VMEM((4, 2, 2 * BAND, BAND), BF16),
                pltpu.VMEM((4, HALO, BAND), F32)]
    return pl.pallas_call(
        _attn_prompt_kernel,
        grid=(batch, hp, nt),
        in_specs=in_specs,
        out_specs=[pl.BlockSpec((ATTN_TILE, LANES), lambda b, h, n: (b * nt + n, h))]
        + [pl.BlockSpec((1, 2, 2, HEAD_DIM, w), lambda b, h, n: (b, 0, h, 0, 0)) for w, _ in ATTN_GROUPS],
        out_shape=[jax.ShapeDtypeStruct((batch * seq, A_WIDTH), F32)]
        + [jax.ShapeDtypeStruct((batch, 2, HEADS, HEAD_DIM, w), F32) for w, _ in ATTN_GROUPS],
        scratch_shapes=scratch,
        compiler_params=_cparams(("parallel", "parallel", "arbitrary")),
        name="attn_prompt",
    )(*([u] * 9))


def _conv_silu(pad_ref, w_ref, b_ref, rows):
    acc = b_ref[...] + pad_ref[HALO - (CONV_W - 1):HALO - (CONV_W - 1) + rows, :] * w_ref[0:1, :]
    for i in range(1, CONV_W):
        off = HALO - (CONV_W - 1) + i
        acc = acc + pad_ref[off:off + rows, :] * w_ref[i:i + 1, :]
    return _silu(acc)


def _conv_silu_chunk(x_ref, tail_ref, w_ref, b_ref):
    x = x_ref[...].astype(F32)
    ext = jnp.concatenate([tail_ref[...], x], axis=0)
    tail_ref[...] = x[x.shape[0] - HALO:, :]
    acc = b_ref[...] + x * w_ref[CONV_W - 1:CONV_W, :]
    for k in range(1, CONV_W):
        acc = acc + pltpu.roll(ext, k, 0)[HALO:, :] * w_ref[CONV_W - 1 - k:CONV_W - k, :]
    return _silu(acc)


def _softplus(x):
    return jnp.maximum(x, 0.0) + jnp.log1p(jnp.exp(-jnp.abs(x)))


def _ssd_prompt_kernel(x_ref, b_ref, c_ref, dt_ref, cwx, cwb, cwc, cbx, cbb, cbc,
                       dtb_ref, alog_ref, dexp_ref, e_ref, q_ref, new_ref, c0, c1, c2,
                       y_ref, st_ref, ao_ref, o0, o1, o2,
                       xtail, btail, ctail, state):
    c = pl.program_id(1)
    nc = pl.num_programs(1)

    @pl.when(c == 0)
    def _():
        xtail[...] = jnp.zeros(xtail.shape, F32)
        btail[...] = jnp.zeros(btail.shape, F32)
        ctail[...] = jnp.zeros(ctail.shape, F32)
        state[...] = jnp.zeros(state.shape, F32)

    for hx in range(q_ref.shape[1]):
        _attn_sample_head(hx, q_ref, new_ref, (c0, c1, c2), ao_ref, (o0, o1, o2))

    xc = _conv_silu_chunk(x_ref, xtail, cwx, cbx)
    bm = _conv_silu_chunk(b_ref, btail, cwb, cbb)
    cm = _conv_silu_chunk(c_ref, ctail, cwc, cbc)

    dt = _softplus(dt_ref[...] + dtb_ref[...])
    a = -jnp.exp(alog_ref[...])
    dta = dt * a
    ti = lax.broadcasted_iota(jnp.int32, (CHUNK, CHUNK), 0)
    si = lax.broadcasted_iota(jnp.int32, (CHUNK, CHUNK), 1)
    causal = si <= ti
    tril = jnp.where(causal, 1.0, 0.0).astype(F32)
    cum = jnp.dot(tril, dta, preferred_element_type=F32, precision=lax.Precision.HIGHEST)
    cum2 = cum * LOG2E
    cum2_t = cum2.T
    cum2_last = cum2[CHUNK - 1:CHUNK, :]
    expcum = jnp.exp2(cum2)
    to_end = jnp.exp2(cum2_last - cum2) * dt
    ex = _expand01(jnp.concatenate([expcum, to_end, dt], axis=0), e_ref[...])
    expcum_x = ex[0:CHUNK, :]
    to_end_x = ex[CHUNK:2 * CHUNK, :]
    dt_x = ex[2 * CHUNK:3 * CHUNK, :]
    dec_x = expcum_x[CHUNK - 1:CHUNK, :]

    lane = lax.broadcasted_iota(jnp.int32, (CHUNK, LANES), 1)
    head0 = lane < SSD_HEAD_DIM
    hpg = SSD_HEADS // SSD_GROUPS
    for g in range(SSD_GROUPS):
        gs = slice(g * SSD_GROUP_W, (g + 1) * SSD_GROUP_W)
        ns = slice(g * SSD_STATE, (g + 1) * SSD_STATE)
        bg = bm[:, ns].astype(BF16)
        cg = cm[:, ns].astype(BF16)
        cb = _dot_nt(cg, bg)
        xg = xc[:, gs]
        xdt = xg * dt_x[:, gs]
        ys = []
        for pr in range(hpg // 2):
            ws = []
            for hh in range(2):
                h = g * hpg + 2 * pr + hh
                colb = jnp.broadcast_to(cum2[:, h:h + 1], (CHUNK, CHUNK))
                rowb = jnp.broadcast_to(cum2_t[h:h + 1, :], (CHUNK, CHUNK))
                decay = jnp.exp2(jnp.where(causal, colb - rowb, -jnp.inf))
                ws.append((cb * decay).astype(BF16))
            w2 = jnp.concatenate(ws, axis=1)
            xp = xdt[:, pr * LANES:(pr + 1) * LANES]
            x2 = jnp.concatenate([jnp.where(head0, xp, 0.0), jnp.where(head0, 0.0, xp)], axis=0).astype(BF16)
            ys.append(_dot(w2, x2))
        st = state[g]
        y_state = _dot(cg, st.astype(BF16))
        y_ref[:, gs] = jnp.concatenate(ys, axis=1) + y_state * expcum_x[:, gs] + dexp_ref[:, gs] * xg
        xs = (xg * to_end_x[:, gs]).astype(BF16)
        state[g] = dec_x[:, gs] * st + _dot_tn(bg, xs)

    @pl.when(c == nc - 1)
    def _():
        for g in range(SSD_GROUPS):
            st_ref[0, g * hpg:(g + 1) * hpg, :, :] = state[g].T.reshape(hpg, SSD_HEAD_DIM, SSD_STATE)


def _ssd_prompt(u, dt_raw, conv_w, conv_b, dt_bias, a_log, d_exp, e_mat, batch, seq, q_s, new_t, caches):
    nc = seq // CHUNK
    cwx, cwb, cwc = conv_w[:, :D_INNER], conv_w[:, D_INNER:D_INNER + SSD_GROUP_W], conv_w[:, D_INNER + SSD_GROUP_W:]
    cbx, cbb, cbc = conv_b[:, :D_INNER], conv_b[:, D_INNER:D_INNER + SSD_GROUP_W], conv_b[:, D_INNER + SSD_GROUP_W:]
    full = lambda shape: pl.BlockSpec(shape, lambda b, c: (0,) * len(shape))
    dbatch, _, _, n_new, _ = q_s.shape
    steps = batch * nc
    assert (dbatch * HEADS) % steps == 0, "sample heads must spread evenly over the prompt's SSD steps"
    hps = dbatch * HEADS // steps
    assert HEADS % hps == 0
    spb = HEADS // hps
    sb = lambda b, c: (b * nc + c) // spb
    hb = lambda b, c: (b * nc + c) % spb
    q_spec = pl.BlockSpec((1, hps, len(ATTN_GROUPS), n_new, HEAD_DIM), lambda b, c: (sb(b, c), hb(b, c), 0, 0, 0))
    new_spec = pl.BlockSpec((1, hps, HEAD_DIM, LANES), lambda b, c: (sb(b, c), hb(b, c), 0, 0))
    ao_spec = pl.BlockSpec((1, hps, n_new, HEAD_DIM), lambda b, c: (sb(b, c), hb(b, c), 0, 0))
    cache_specs = [pl.BlockSpec((1, 2, hps, HEAD_DIM, w), lambda b, c: (sb(b, c), 0, hb(b, c), 0, 0))
                   for w, _ in ATTN_GROUPS]
    return pl.pallas_call(
        _ssd_prompt_kernel,
        grid=(batch, nc),
        in_specs=[
            pl.BlockSpec((CHUNK, D_INNER), lambda b, c: (b * nc + c, X_OFF // D_INNER)),
            pl.BlockSpec((CHUNK, SSD_GROUP_W), lambda b, c: (b * nc + c, B_OFF // SSD_GROUP_W)),
            pl.BlockSpec((CHUNK, SSD_GROUP_W), lambda b, c: (b * nc + c, C_OFF // SSD_GROUP_W)),
            pl.BlockSpec((CHUNK, LANES), lambda b, c: (b * nc + c, 0)),
            full((CONV_W, D_INNER)), full((CONV_W, SSD_GROUP_W)), full((CONV_W, SSD_GROUP_W)),
            full((1, D_INNER)), full((1, SSD_GROUP_W)), full((1, SSD_GROUP_W)),
            full((1, LANES)), full((1, LANES)), full((1, D_INNER)), full((LANES, D_INNER)),
            q_spec, new_spec,
        ] + cache_specs,
        out_specs=[
            pl.BlockSpec((CHUNK, D_INNER), lambda b, c: (b * nc + c, 0)),
            pl.BlockSpec((1, SSD_HEADS, SSD_HEAD_DIM, SSD_STATE), lambda b, c: (b, 0, 0, 0)),
            ao_spec,
        ] + cache_specs,
        out_shape=[
            jax.ShapeDtypeStruct((batch * seq, D_INNER), F32),
            jax.ShapeDtypeStruct((batch, SSD_HEADS, SSD_HEAD_DIM, SSD_STATE), F32),
            jax.ShapeDtypeStruct((dbatch, HEADS, n_new, HEAD_DIM), F32),
        ] + [jax.ShapeDtypeStruct(cc.shape, F32) for cc in caches],
        scratch_shapes=[
            pltpu.VMEM((HALO, D_INNER), F32),
            pltpu.VMEM((HALO, SSD_GROUP_W), F32),
            pltpu.VMEM((HALO, SSD_GROUP_W), F32),
            pltpu.VMEM((SSD_GROUPS, SSD_STATE, SSD_GROUP_W), F32),
        ],
        compiler_params=_cparams(("parallel", "arbitrary")),
        name="ssd_prompt",
    )(u, u, u, dt_raw, cwx, cwb, cwc, cbx, cbb, cbc, dt_bias, a_log, d_exp, e_mat, q_s, new_t, *caches)


def _mix_ffn_kernel(a_ref, y_ref, z_ref, ga_ref, gb_ref, x_ref, wa_ref, ws_ref, wo_ref,
                    sn_ref, pn_ref, wu_ref, wd_ref, n1_ref, n2_ref, o_ref):
    a_proj = _dot(a_ref[...].astype(BF16), wa_ref[...])
    g = y_ref[...] * _silu(z_ref[...].astype(F32))
    parts = []
    for i in range(SSD_GROUPS):
        gi = g[:, i * SSD_GROUP_W:(i + 1) * SSD_GROUP_W]
        parts.append(gi * lax.rsqrt(jnp.mean(gi * gi, axis=-1, keepdims=True) + RMS_EPS))
    yn = (jnp.concatenate(parts, axis=1) * sn_ref[...]).astype(BF16)
    b_proj = _dot(yn, ws_ref[...])
    mixed_in = _sigmoid(ga_ref[...].astype(F32)) * a_proj + _sigmoid(gb_ref[...].astype(F32)) * b_proj
    mixed = _dot(mixed_in.astype(BF16), wo_ref[...])
    x1 = x_ref[...] + _rms(mixed, pn_ref[...])
    h = _rms(x1, n1_ref[...]).astype(BF16)
    gu = _dot(h, wu_ref[...])
    act = (_silu(gu[:, :D_FF]) * gu[:, D_FF:]).astype(BF16)
    f = _dot(act, wd_ref[...])
    o_ref[...] = x1 + _rms(f, n2_ref[...])


def _mix_ffn(a_out, y, u, x, w_attn_out, w_ssd_out, w_out, ssd_norm, post_mix_norm,
             w_up, w_down, pre_norm, post_norm, tm):
    t = x.shape[0]
    const = lambda shape: pl.BlockSpec(shape, lambda i: (0, 0), pipeline_mode=pl.Buffered(1))
    return pl.pallas_call(
        _mix_ffn_kernel,
        grid=(t // tm,),
        in_specs=[
            pl.BlockSpec((tm, A_WIDTH), lambda i: (i, 0)),
            pl.BlockSpec((tm, D_INNER), lambda i: (i, 0)),
            pl.BlockSpec((tm, D_INNER), lambda i: (i, Z_OFF // D_INNER)),
            pl.BlockSpec((tm, D_MODEL), lambda i: (i, GATE_OFF // D_MODEL)),
            pl.BlockSpec((tm, D_MODEL), lambda i: (i, GATE_OFF // D_MODEL + 1)),
            pl.BlockSpec((tm, D_MODEL), lambda i: (i, 0)),
            const((A_WIDTH, D_MODEL)), const((D_INNER, D_MODEL)), const((D_MODEL, D_MODEL)),
            const((1, D_INNER)), const((1, D_MODEL)),
            const((D_MODEL, 2 * D_FF)), const((D_FF, D_MODEL)),
            const((1, D_MODEL)), const((1, D_MODEL)),
        ],
        out_specs=pl.BlockSpec((tm, D_MODEL), lambda i: (i, 0)),
        out_shape=jax.ShapeDtypeStruct((t, D_MODEL), F32),
        compiler_params=_cparams(("parallel",)),
        name="mix_ffn",
    )(a_out, y, u, u, u, x, w_attn_out, w_ssd_out, w_out, ssd_norm, post_mix_norm,
      w_up, w_down, pre_norm, post_norm)


def _attn_sample_head(hx, q_ref, new_ref, caches, ao_ref, outs):
    n_new = q_ref.shape[3]
    new_t = new_ref[0, hx]
    m_run = l_run = o_run = None
    for g, (win, dil) in enumerate(ATTN_GROUPS):
        q = (q_ref[0, hx, g] * ATTN_SCALE).astype(BF16)
        k_t = caches[g][0, 0, hx]
        v_t = caches[g][0, 1, hx]
        knt = pltpu.roll(new_t, LANES - n_new - (2 * g) * n_new, 1)
        vnt = pltpu.roll(new_t, LANES - n_new - (2 * g + 1) * n_new, 1)
        qi = lax.broadcasted_iota(jnp.int32, (n_new, win), 0)
        ci = lax.broadcasted_iota(jnp.int32, (n_new, win), 1)
        ok_c = jnp.logical_and(ci >= qi, ((ci - qi) & (dil - 1)) == 0)
        qn = lax.broadcasted_iota(jnp.int32, (n_new, LANES), 0)
        nn = lax.broadcasted_iota(jnp.int32, (n_new, LANES), 1) - (LANES - n_new)
        ok_n = jnp.logical_and(jnp.logical_and(nn >= 0, nn <= qn), ((qn - nn) & (dil - 1)) == 0)
        s_c = jnp.where(ok_c, _dot(q, k_t.astype(BF16)), -jnp.inf)
        s_n = jnp.where(ok_n, _dot(q, knt.astype(BF16)), -jnp.inf)
        m = jnp.maximum(jnp.max(s_c, axis=1, keepdims=True), jnp.max(s_n, axis=1, keepdims=True))
        p_c = jnp.exp(s_c - m)
        p_n = jnp.exp(s_n - m)
        den = jnp.sum(p_c, axis=1, keepdims=True) + jnp.sum(p_n, axis=1, keepdims=True)
        o = _dot_nt(p_c.astype(BF16), v_t.astype(BF16)) + _dot_nt(p_n.astype(BF16), vnt.astype(BF16))
        if g == 0:
            m_run, l_run, o_run = m, den, o
        else:
            m_new = jnp.maximum(m_run, m)
            ea = jnp.exp(m_run - m_new)
            eb = jnp.exp(m - m_new)
            l_run = ea * l_run + eb * den
            o_run = ea * o_run + eb * o
            m_run = m_new
        lane = lax.broadcasted_iota(jnp.int32, (HEAD_DIM, LANES), 1)
        keep = lane < LANES - n_new
        for which, (src, new) in enumerate(((k_t, knt), (v_t, vnt))):
            rolled = pltpu.roll(src, win - n_new, 1)
            if win > LANES:
                outs[g][0, which, hx, :, 0:win - LANES] = rolled[:, 0:win - LANES]
            outs[g][0, which, hx, :, win - LANES:win] = jnp.where(keep, rolled[:, win - LANES:win], new)
    ao_ref[0, hx] = o_run * (1.0 / l_run)


def _ssd_sample_kernel(x_ref, b_ref, c_ref, dt_ref, sx_ref, sb_ref, sc_ref, st_in,
                       cwx, cwb, cwc, cbx, cbb, cbc, dtb_ref, alog_ref, dexp_ref, e_ref,
                       y_ref, st_out, xpad, bpad, cpad):
    n = x_ref.shape[0]
    keep = CONV_W - 1
    for pad, new, old in ((xpad, x_ref, sx_ref), (bpad, b_ref, sb_ref), (cpad, c_ref, sc_ref)):
        pad[0:HALO, :] = jnp.zeros((HALO, pad.shape[1]), F32)
        pad[HALO - keep:HALO, :] = old[0]
        pad[HALO:HALO + n, :] = new[...]
    xc = _conv_silu(xpad, cwx, cbx, n)
    bm = _conv_silu(bpad, cwb, cbb, n)
    cm = _conv_silu(cpad, cwc, cbc, n)

    dt = _softplus(dt_ref[...] + dtb_ref[...])
    a = -jnp.exp(alog_ref[...])
    dta = dt * a
    ti = lax.broadcasted_iota(jnp.int32, (n, n), 0)
    si = lax.broadcasted_iota(jnp.int32, (n, n), 1)
    tril = jnp.where(si <= ti, 1.0, 0.0).astype(F32)
    cum = jnp.dot(tril, dta, preferred_element_type=F32, precision=lax.Precision.HIGHEST)
    cum_last = cum[n - 1:n, :]
    expcum = jnp.exp(cum)
    to_end = jnp.exp(cum_last - cum) * dt

    hpg = SSD_HEADS // SSD_GROUPS
    lane = lax.broadcasted_iota(jnp.int32, (n, LANES), 1)
    trow = lax.broadcasted_iota(jnp.int32, (n, LANES), 0)
    cbs = [_dot_nt(cm[:, g * SSD_STATE:(g + 1) * SSD_STATE].astype(BF16),
                   bm[:, g * SSD_STATE:(g + 1) * SSD_STATE].astype(BF16)) for g in range(SSD_GROUPS)]
    coefs = []
    for s in range(n):
        cbh = jnp.zeros((n, LANES), F32)
        for g in range(SSD_GROUPS):
            in_g = jnp.logical_and(lane >= g * hpg, lane < (g + 1) * hpg)
            cbh = jnp.where(in_g, jnp.broadcast_to(cbs[g][:, s:s + 1], (n, LANES)), cbh)
        decay = jnp.exp(jnp.where(trow >= s, cum - cum[s:s + 1, :], -jnp.inf))
        coefs.append(cbh * decay * dt[s:s + 1, :])
    ex = _expand01(jnp.concatenate(coefs + [expcum, to_end], axis=0), e_ref[...])
    y = dexp_ref[...] * xc
    for s in range(n):
        y = y + ex[s * n:(s + 1) * n, :] * xc[s:s + 1, :]
    expcum_x = ex[n * n:n * n + n, :]
    to_end_x = ex[n * n + n:n * n + 2 * n, :]
    xs = xc * to_end_x
    dec_t = jnp.broadcast_to(expcum[n - 1:n, :], (LANES, LANES)).T
    ys = []
    for g in range(SSD_GROUPS):
        gs = slice(g * SSD_GROUP_W, (g + 1) * SSD_GROUP_W)
        ns = slice(g * SSD_STATE, (g + 1) * SSD_STATE)
        st = st_in[0, g * hpg:(g + 1) * hpg].reshape(SSD_GROUP_W, SSD_STATE)
        ys.append(_dot_nt(cm[:, ns].astype(BF16), st.astype(BF16)))
        upd = _dot_tn(xs[:, gs].astype(BF16), bm[:, ns].astype(BF16))
        for hh in range(hpg):
            h = g * hpg + hh
            rows = slice(hh * SSD_HEAD_DIM, (hh + 1) * SSD_HEAD_DIM)
            st_out[0, h] = dec_t[h:h + 1, :] * st[rows, :] + upd[rows, :]
    y_ref[...] = y + jnp.concatenate(ys, axis=1) * expcum_x


def _ssd_sample(u, dt_raw, conv_state, ssm_state, conv_w, conv_b, dt_bias, a_log, d_exp, e_mat, n_new):
    bsz = ssm_state.shape[0]
    keep = CONV_W - 1
    cwx, cwb, cwc = conv_w[:, :D_INNER], conv_w[:, D_INNER:D_INNER + SSD_GROUP_W], conv_w[:, D_INNER + SSD_GROUP_W:]
    cbx, cbb, cbc = conv_b[:, :D_INNER], conv_b[:, D_INNER:D_INNER + SSD_GROUP_W], conv_b[:, D_INNER + SSD_GROUP_W:]
    full = lambda shape: pl.BlockSpec(shape, lambda b: (0,) * len(shape))
    st_spec = pl.BlockSpec((1, SSD_HEADS, SSD_HEAD_DIM, SSD_STATE), lambda b: (b, 0, 0, 0))
    return pl.pallas_call(
        _ssd_sample_kernel,
        grid=(bsz,),
        in_specs=[
            pl.BlockSpec((n_new, D_INNER), lambda b: (b, X_OFF // D_INNER)),
            pl.BlockSpec((n_new, SSD_GROUP_W), lambda b: (b, B_OFF // SSD_GROUP_W)),
            pl.BlockSpec((n_new, SSD_GROUP_W), lambda b: (b, C_OFF // SSD_GROUP_W)),
            pl.BlockSpec((n_new, LANES), lambda b: (b, 0)),
            pl.BlockSpec((1, keep, D_INNER), lambda b: (b, 0, 0)),
            pl.BlockSpec((1, keep, SSD_GROUP_W), lambda b: (b, 0, D_INNER // SSD_GROUP_W)),
            pl.BlockSpec((1, keep, SSD_GROUP_W), lambda b: (b, 0, D_INNER // SSD_GROUP_W + 1)),
            st_spec,
            full((CONV_W, D_INNER)), full((CONV_W, SSD_GROUP_W)), full((CONV_W, SSD_GROUP_W)),
            full((1, D_INNER)), full((1, SSD_GROUP_W)), full((1, SSD_GROUP_W)),
            full((1, LANES)), full((1, LANES)), full((1, D_INNER)), full((LANES, D_INNER)),
        ],
        out_specs=[pl.BlockSpec((n_new, D_INNER), lambda b: (b, 0)), st_spec],
        out_shape=[
            jax.ShapeDtypeStruct((bsz * n_new, D_INNER), F32),
            jax.ShapeDtypeStruct(ssm_state.shape, F32),
        ],
        scratch_shapes=[
            pltpu.VMEM((HALO + n_new, D_INNER), F32),
            pltpu.VMEM((HALO + n_new, SSD_GROUP_W), F32),
            pltpu.VMEM((HALO + n_new, SSD_GROUP_W), F32),
        ],
        compiler_params=_cparams(("parallel",)),
        name="ssd_sample",
    )(u, u, u, dt_raw, conv_state, conv_state, conv_state, ssm_state,
      cwx, cwb, cwc, cbx, cbb, cbc, dt_bias, a_log, d_exp, e_mat)


def _prep_weights(w_in, dt_bias, d_skip):
    q_end = 3 * len(ATTN_GROUPS) * A_WIDTH
    z_end = q_end + D_INNER
    x_end = z_end + D_INNER
    b_end = x_end + SSD_GROUP_W
    c_end = b_end + SSD_GROUP_W
    dt_end = c_end + SSD_HEADS
    w_main = jnp.concatenate(
        [w_in[:, q_end:z_end], w_in[:, dt_end:], w_in[:, z_end:c_end], w_in[:, :q_end]], axis=1).astype(BF16)
    w_dt = jnp.pad(w_in[:, c_end:dt_end], ((0, 0), (0, LANES - SSD_HEADS))).astype(BF16)
    dtb = jnp.pad(dt_bias, (0, LANES - SSD_HEADS)).reshape(1, LANES)
    d_exp = jnp.repeat(d_skip, SSD_HEAD_DIM).reshape(1, D_INNER)
    return w_main, w_dt, dtb, d_exp


def _tail_conv(u, batch, seq):
    u3 = u.reshape(batch, seq, N_REST)
    return u3[:, seq - (CONV_W - 1):, X_OFF:C_OFF + SSD_GROUP_W].astype(F32)


def kernel(x_prompt, x_sample, cache_kv_w128, cache_kv_w512, cache_kv_w2048, state_conv, state_ssm,
           pre_mix_norm, w_in, conv_w, conv_b, dt_bias, a_log, d_skip, ssd_norm, w_attn_out, w_ssd_out,
           w_out, post_mix_norm, pre_ffn_norm, w_up, w_down, post_ffn_norm):
    batch, seq, _ = x_prompt.shape
    dbatch, dseq, _ = x_sample.shape
    assert w_in.shape[0] == 1, "single-layer trunk"
    assert seq % ATTN_TILE == 0 and dseq == HALO

    w_main, w_dt, dtb, d_exp = _prep_weights(w_in[0], dt_bias[0], d_skip[0])
    alog = jnp.pad(a_log[0], (0, LANES - SSD_HEADS)).reshape(1, LANES)
    e_mat = (jnp.arange(LANES)[:, None] == (jnp.arange(D_INNER)[None, :] // SSD_HEAD_DIM)).astype(BF16)
    nw = pre_mix_norm[0].reshape(1, D_MODEL)
    cw, cb = conv_w[0], conv_b[0].reshape(1, -1)
    wa, ws, wo = w_attn_out[0].astype(BF16), w_ssd_out[0].astype(BF16), w_out[0].astype(BF16)
    wu, wd = w_up[0].astype(BF16), w_down[0].astype(BF16)
    sn = ssd_norm[0].reshape(1, D_INNER)
    pmn, pfn, qfn = (post_mix_norm[0].reshape(1, -1), pre_ffn_norm[0].reshape(1, -1),
                     post_ffn_norm[0].reshape(1, -1))

    xp = x_prompt.reshape(batch * seq, D_MODEL)
    xs = x_sample.reshape(dbatch * dseq, D_MODEL)
    ur_p, uq_p, dt_p = _inproj(xp, nw, w_main, w_dt, min(2048, batch * seq), 512, BF16)
    ur_s, uq_s, dt_s = _inproj(xs, nw, w_main, w_dt, min(1024, dbatch * dseq), 512, F32)
    qkv = uq_s.reshape(dbatch, dseq, 3, 3, HEADS, HEAD_DIM)
    q_s = qkv[:, :, :, 0].transpose(0, 3, 2, 1, 4)
    new_t = qkv[:, :, :, 1:3].transpose(0, 4, 5, 2, 3, 1).reshape(dbatch, HEADS, HEAD_DIM, 6 * dseq)
    new_t = jnp.pad(new_t, ((0, 0), (0, 0), (0, 0), (0, LANES - 6 * dseq)))
    caches = [c[0].transpose(0, 2, 3, 4, 1) for c in (cache_kv_w128, cache_kv_w512, cache_kv_w2048)]

    a_p, *kvt_p = _attn_prompt(uq_p, batch, seq)
    y_p, ssm_p, ao, nc0, nc1, nc2 = _ssd_prompt(ur_p, dt_p, cw, cb, dtb, alog, d_exp, e_mat, batch, seq,
                                                q_s, new_t, caches)
    out_p = _mix_ffn(a_p, y_p, ur_p, xp, wa, ws, wo, sn, pmn, wu, wd, pfn, qfn, 256).reshape(batch, seq, D_MODEL)
    kv_p = [t.transpose(0, 4, 1, 2, 3)[None] for t in kvt_p]
    conv_p = _tail_conv(ur_p, batch, seq)[None]

    a_s = ao.transpose(0, 2, 1, 3).reshape(dbatch * dseq, A_WIDTH)
    kv_s = [c.transpose(0, 4, 1, 2, 3)[None] for c in (nc0, nc1, nc2)]
    y_s, ssm_s = _ssd_sample(ur_s, dt_s, state_conv[0], state_ssm[0], cw, cb, dtb, alog, d_exp,
                             e_mat, dseq)
    tm_s = min(256, dbatch * dseq)
    out_s = _mix_ffn(a_s, y_s, ur_s, xs, wa, ws, wo, sn, pmn, wu, wd, pfn, qfn, tm_s).reshape(dbatch, dseq, D_MODEL)
    conv_s = _tail_conv(ur_s, dbatch, dseq)[None]

    return (out_p, out_s, kv_p[0], kv_p[1], kv_p[2], conv_p, ssm_p[None],
            kv_s[0], kv_s[1], kv_s[2], conv_s, ssm_s[None])
```

```python
import functools

import jax
import jax.numpy as jnp
from jax import lax
from jax.experimental import pallas as pl
from jax.experimental.pallas import tpu as pltpu

F32 = jnp.float32
BF16 = jnp.bfloat16

D_MODEL = 1024
ATTN_GROUPS = ((128, 1), (512, 4), (2048, 16))
BAND = 128
HEADS = 8
HEAD_DIM = 64
A_WIDTH = HEADS * HEAD_DIM
ATTN_SCALE = HEAD_DIM ** -0.5
LOG2E = 1.4426950408889634
ATTN_TILE = 2048
BLOCK_UNROLL = 4

D_INNER = 2048
SSD_HEADS = 32
SSD_HEAD_DIM = 64
SSD_GROUPS = 4
SSD_GROUP_W = D_INNER // SSD_GROUPS
SSD_STATE = 128
CONV_W = 4
CHUNK = 128
D_FF = 2816
RMS_EPS = 1e-6
LANES = 128
HALO = 8

Z_OFF = 0
GATE_OFF = 2048
X_OFF = 4096
B_OFF = 6144
C_OFF = 6656
N_REST = 7168
N_QKV = 3 * len(ATTN_GROUPS) * A_WIDTH
N_MAIN = N_REST + N_QKV

VMEM_LIMIT = 56 * 1024 * 1024


def _cparams(sem):
    return pltpu.CompilerParams(dimension_semantics=sem, vmem_limit_bytes=VMEM_LIMIT)


def _rms(x, w):
    return x * lax.rsqrt(jnp.mean(x * x, axis=-1, keepdims=True) + RMS_EPS) * w


def _silu(x):
    h = 0.5 * x
    return h + h * jnp.tanh(h)


def _sigmoid(x):
    return 0.5 + 0.5 * jnp.tanh(0.5 * x)


def _dot(a, b):
    return jnp.dot(a, b, preferred_element_type=F32)


def _dot_nt(a, b):
    return lax.dot_general(a, b, (((1,), (1,)), ((), ())), preferred_element_type=F32)


def _dot_tn(a, b):
    return lax.dot_general(a, b, (((0,), (0,)), ((), ())), preferred_element_type=F32)


def _expand01(a, e):
    hi = a.astype(BF16)
    mid = (a - hi.astype(F32)).astype(BF16)
    return _dot(hi, e) + _dot(mid, e)


def _inproj_kernel(x_ref, nw_ref, w_ref, wdt_ref, r_ref, q_ref, dt_ref, h_ref, *, rest_tiles):
    j = pl.program_id(1)

    @pl.when(j == 0)
    def _():
        hb = _rms(x_ref[...], nw_ref[...]).astype(BF16)
        h_ref[...] = hb
        dt_ref[...] = _dot(hb, wdt_ref[...])

    @pl.when(j < rest_tiles)
    def _():
        r_ref[...] = _dot(h_ref[...], w_ref[...]).astype(r_ref.dtype)

    @pl.when(j >= rest_tiles)
    def _():
        q_ref[...] = _dot(h_ref[...], w_ref[...])


def _inproj(x, nw, w_main, w_dt, tm, tn, rest_dtype):
    t = x.shape[0]
    rest_tiles = N_REST // tn
    return pl.pallas_call(
        functools.partial(_inproj_kernel, rest_tiles=rest_tiles),
        grid=(t // tm, N_MAIN // tn),
        in_specs=[
            pl.BlockSpec((tm, D_MODEL), lambda i, j: (i, 0)),
            pl.BlockSpec((1, D_MODEL), lambda i, j: (0, 0)),
            pl.BlockSpec((D_MODEL, tn), lambda i, j: (0, j)),
            pl.BlockSpec((D_MODEL, LANES), lambda i, j: (0, 0)),
        ],
        out_specs=[
            pl.BlockSpec((tm, tn), lambda i, j: (i, jnp.minimum(j, rest_tiles - 1))),
            pl.BlockSpec((tm, tn), lambda i, j: (i, jnp.maximum(j - rest_tiles, 0))),
            pl.BlockSpec((tm, LANES), lambda i, j: (i, 0)),
        ],
        out_shape=[
            jax.ShapeDtypeStruct((t, N_REST), rest_dtype),
            jax.ShapeDtypeStruct((t, N_QKV), F32),
            jax.ShapeDtypeStruct((t, LANES), F32),
        ],
        scratch_shapes=[pltpu.VMEM((tm, D_MODEL), BF16)],
        compiler_params=_cparams(("parallel", "arbitrary")),
        name="inproj",
    )(x, nw, w_main, w_dt)


def _unroll_for(trips):
    return max(u for u in range(1, BLOCK_UNROLL + 1) if trips % u == 0)


def _attn_prompt_kernel(q0, k0, v0, q1, k1, v1, q2, k2, v2, o_ref, kvo0, kvo1, kvo2,
                        kd0, vt0, kd1, vt1, kd2, vt2, lse_ref, acc_ref, s_buf, p_buf, stat_buf):
    n = pl.program_id(2)
    q_refs, k_refs, v_refs = (q0, q1, q2), (k0, k1, k2), (v0, v1, v2)
    kds, vts = (kd0, kd1, kd2), (vt0, vt1, vt2)
    krow = lax.broadcasted_iota(jnp.int32, (BAND, LANES), 0)
    qcol = lax.broadcasted_iota(jnp.int32, (BAND, LANES), 1)
    upper = krow > qcol
    diag = krow == qcol
    head0 = qcol < HEAD_DIM
    half = BAND // 2

    def group_stages(g):
        dil = ATTN_GROUPS[g][1]
        span = BAND * dil
        nsub = ATTN_TILE // span
        nblk = nsub * dil
        q_ref, k_ref, v_ref, kd, vt = q_refs[g], k_refs[g], v_refs[g], kds[g], vts[g]

        def split(i):
            if dil == 1:
                return i, 0
            j = i // dil
            return j, i - j * dil

        def rows_of(i):
            j, r = split(i)
            if dil == 1:
                return pl.ds(i * BAND if isinstance(i, int) else pl.multiple_of(i * BAND, BAND), BAND)
            return pl.ds(j * span + r, BAND, stride=dil)

        def carry():
            @pl.when(n == 0)
            def _():
                for r in range(dil):
                    kd[r * (nsub + 1)] = jnp.zeros((BAND, LANES), BF16)
                    vt[r * (nsub + 1)] = jnp.zeros((LANES, BAND), BF16)

            @pl.when(n > 0)
            def _():
                for r in range(dil):
                    kd[r * (nsub + 1)] = kd[r * (nsub + 1) + nsub]
                    vt[r * (nsub + 1)] = vt[r * (nsub + 1) + nsub]

        def deinterleave():
            for i in range(nblk):
                j, r = split(i)
                slot = r * (nsub + 1) + j + 1
                rows = rows_of(i)
                kd[slot] = k_ref[rows, :].astype(BF16)
                vt[slot] = v_ref[rows, :].T.astype(BF16)

        def qk_stage(i, sl):
            j, r = split(i)
            slot = r * (nsub + 1) + j
            q = q_ref[rows_of(i), :] * (ATTN_SCALE * LOG2E)
            q2 = jnp.concatenate([jnp.where(head0, q, 0.0), jnp.where(head0, 0.0, q)], axis=0).astype(BF16)
            kk = kd[pl.ds(slot, 2)].reshape(2 * BAND, LANES)
            s_buf[sl] = _dot_nt(kk, q2)

        def softmax_stage(sl, first):
            for hh in range(2):
                sp = s_buf[sl, 0:BAND, hh * BAND:(hh + 1) * BAND]
                sc = s_buf[sl, BAND:2 * BAND, hh * BAND:(hh + 1) * BAND]
                if first:
                    sp = jnp.where(n > 0, sp, -jnp.inf)
                comb = jnp.where(upper, sp, sc)
                dg = jnp.sum(jnp.where(diag, sp, 0.0), axis=0, keepdims=True)
                m = jnp.maximum(jnp.max(comb, axis=0, keepdims=True), dg)
                p = jnp.exp2(comb - m)
                pd = jnp.exp2(dg - m)
                den = jnp.sum(p, axis=0, keepdims=True) + pd
                p_buf[sl, hh, 0:BAND, :] = jnp.where(upper, p, jnp.where(diag, pd, 0.0)).astype(BF16)
                p_buf[sl, hh, BAND:2 * BAND, :] = jnp.where(upper, 0.0, p).astype(BF16)
                stat_buf[sl, 2 * hh:2 * hh + 1, :] = 1.0 / den
                stat_buf[sl, 2 * hh + 1:2 * hh + 2, :] = m + jnp.log2(den)

        def pv_stage(i, sl):
            j, r = split(i)
            slot = r * (nsub + 1) + j
            rows = rows_of(i)
            vprev = vt[slot]
            vcur = vt[slot + 1]
            ots, lts = [], []
            for hh in range(2):
                vsl = slice(hh * half, (hh + 1) * half)
                vtt = jnp.concatenate([vprev[vsl, :], vcur[vsl, :]], axis=1)
                ots.append(_dot(vtt, p_buf[sl, hh]) * stat_buf[sl, 2 * hh:2 * hh + 1, :])
                lts.append(jnp.broadcast_to(stat_buf[sl, 2 * hh + 1:2 * hh + 2, :], (half, BAND)))
            o_blk = jnp.concatenate(ots, axis=0).T
            lse_blk = jnp.concatenate(lts, axis=0).T
            if g == 0:
                lse_ref[rows, :] = lse_blk
                acc_ref[rows, :] = o_blk
            else:
                lse_run = lse_ref[rows, :]
                acc_run = acc_ref[rows, :]
                mx = jnp.maximum(lse_run, lse_blk)
                ea = jnp.exp2(lse_run - mx)
                eb = jnp.exp2(lse_blk - mx)
                tot = ea + eb
                inv = 1.0 / tot
                merged = (ea * inv) * acc_run + (eb * inv) * o_blk
                if g == len(ATTN_GROUPS) - 1:
                    o_ref[rows, :] = merged
                else:
                    lse_ref[rows, :] = mx + jnp.log2(tot)
                    acc_ref[rows, :] = merged

        def qk_pair(pair, par):
            qk_stage(2 * pair, 2 * par)
            qk_stage(2 * pair + 1, 2 * par + 1)

        def softmax_pair(pair, par):
            softmax_stage(2 * par, 2 * pair < dil)
            softmax_stage(2 * par + 1, 2 * pair + 1 < dil)

        def pv_pair(pair, par):
            pv_stage(2 * pair, 2 * par)
            pv_stage(2 * pair + 1, 2 * par + 1)

        return (carry, deinterleave), (qk_pair, softmax_pair, pv_pair), nblk // 2

    groups = [group_stages(g) for g in range(len(ATTN_GROUPS))]
    for (carry, _), _, _ in groups:
        carry()
    for (_, deinterleave), _, _ in groups:
        deinterleave()
    items = [(stages, pair) for _, stages, npairs in groups for pair in range(npairs)]
    for t in range(len(items) + 2):
        for depth in (2, 1, 0):
            if 0 <= t - depth < len(items):
                stages, pair = items[t - depth]
                stages[depth](pair, (t - depth) % 2)

    @pl.when(n == pl.num_programs(2) - 1)
    def _():
        for g, (win, _) in enumerate(ATTN_GROUPS):
            for which, ref in enumerate((k_refs[g], v_refs[g])):
                t = ref[ATTN_TILE - win:, :].T
                for hh in range(2):
                    (kvo0, kvo1, kvo2)[g][0, which, hh] = t[hh * HEAD_DIM:(hh + 1) * HEAD_DIM, :]


def _attn_prompt(u, batch, seq):
    nt = seq // ATTN_TILE
    hp = A_WIDTH // LANES
    qkv_blk = 0

    def spec(g, which):
        col = qkv_blk + (3 * g + which) * hp
        return pl.BlockSpec((ATTN_TILE, LANES), lambda b, h, n, col=col: (b * nt + n, col + h))

    in_specs = [spec(g, w) for g in range(3) for w in range(3)]
    scratch = []
    for _, dil in ATTN_GROUPS:
        nslots = dil * (ATTN_TILE // (BAND * dil) + 1)
        scratch += [pltpu.VMEM((nslots, BAND, LANES), BF16), pltpu.VMEM((nslots, LANES, BAND), BF16)]
    scratch += [pltpu.VMEM((ATTN_TILE, LANES), F32), pltpu.VMEM((ATTN_TILE, LANES), F32)]
    scratch += [pltpu.VMEM((4, 2 * BAND, 2 * BAND), F32), pltpu.VMEM((4, 2, 2 * BAND, BAND), BF16),
                pltpu.VMEM((4, HALO, BAND), F32)]
    return pl.pallas_call(
        _attn_prompt_kernel,
        grid=(batch, hp, nt),
        in_specs=in_specs,
        out_specs=[pl.BlockSpec((ATTN_TILE, LANES), lambda b, h, n: (b * nt + n, h))]
        + [pl.BlockSpec((1, 2, 2, HEAD_DIM, w), lambda b, h, n: (b, 0, h, 0, 0)) for w, _ in ATTN_GROUPS],
        out_shape=[jax.ShapeDtypeStruct((batch * seq, A_WIDTH), F32)]
        + [jax.ShapeDtypeStruct((batch, 2, HEADS, HEAD_DIM, w), F32) for w, _ in ATTN_GROUPS],
        scratch_shapes=scratch,
        compiler_params=_cparams(("parallel", "parallel", "arbitrary")),
        name="attn_prompt",
    )(*([u] * 9))


def _conv_silu(pad_ref, w_ref, b_ref, rows):
    acc = b_ref[...] + pad_ref[HALO - (CONV_W - 1):HALO - (CONV_W - 1) + rows, :] * w_ref[0:1, :]
    for i in range(1, CONV_W):
        off = HALO - (CONV_W - 1) + i
        acc = acc + pad_ref[off:off + rows, :] * w_ref[i:i + 1, :]
    return _silu(acc)


def _conv_silu_chunk(x_ref, tail_ref, w_ref, b_ref, cols):
    x = x_ref[:, cols].astype(F32)
    ext = jnp.concatenate([tail_ref[:, cols], x], axis=0)
    tail_ref[:, cols] = x[x.shape[0] - HALO:, :]
    acc = b_ref[:, cols] + x * w_ref[CONV_W - 1:CONV_W, cols]
    for k in range(1, CONV_W):
        acc = acc + pltpu.roll(ext, k, 0)[HALO:, :] * w_ref[CONV_W - 1 - k:CONV_W - k, cols]
    return _silu(acc)


def _softplus(x):
    return jnp.maximum(x, 0.0) + jnp.log1p(jnp.exp(-jnp.abs(x)))


def _ssd_prompt_kernel(x_ref, b_ref, c_ref, dt_ref, cwx, cwb, cwc, cbx, cbb, cbc,
                       dtb_ref, alog_ref, dexp_ref, e_ref, q_ref, new_ref, c0, c1, c2,
                       y_ref, st_ref, ao_ref, o0, o1, o2,
                       xtail, btail, ctail, state, ex_s):
    c = pl.program_id(1)
    nc = pl.num_programs(1)

    @pl.when(c == 0)
    def _():
        xtail[...] = jnp.zeros(xtail.shape, F32)
        btail[...] = jnp.zeros(btail.shape, F32)
        ctail[...] = jnp.zeros(ctail.shape, F32)
        state[...] = jnp.zeros(state.shape, F32)

    sample_pieces = [p for hx in range(q_ref.shape[1])
                     for p in _attn_sample_pieces(hx, q_ref, new_ref, (c0, c1, c2), ao_ref, (o0, o1, o2))]


    dt = _softplus(dt_ref[...] + dtb_ref[...])
    a = -jnp.exp(alog_ref[...])
    dta = dt * a
    ti = lax.broadcasted_iota(jnp.int32, (CHUNK, CHUNK), 0)
    si = lax.broadcasted_iota(jnp.int32, (CHUNK, CHUNK), 1)
    causal = si <= ti
    tril = jnp.where(causal, 1.0, 0.0).astype(F32)
    cum = jnp.dot(tril, dta, preferred_element_type=F32, precision=lax.Precision.HIGHEST)
    cum2 = cum * LOG2E
    cum2_t = cum2.T
    cum2_last = cum2[CHUNK - 1:CHUNK, :]
    expcum = jnp.exp2(cum2)
    to_end = jnp.exp2(cum2_last - cum2) * dt
    ex_s[...] = _expand01(jnp.concatenate([expcum, to_end, dt], axis=0), e_ref[...])
    dec_x = ex_s[CHUNK - 1:CHUNK, :]

    lane = lax.broadcasted_iota(jnp.int32, (CHUNK, LANES), 1)
    head0 = lane < SSD_HEAD_DIM
    hpg = SSD_HEADS // SSD_GROUPS
    for g in range(SSD_GROUPS):
        for idx, piece in enumerate(sample_pieces):
            if idx * SSD_GROUPS // len(sample_pieces) == g:
                piece()
        gs = slice(g * SSD_GROUP_W, (g + 1) * SSD_GROUP_W)
        ns = slice(g * SSD_STATE, (g + 1) * SSD_STATE)
        bg = _conv_silu_chunk(b_ref, btail, cwb, cbb, ns).astype(BF16)
        cg = _conv_silu_chunk(c_ref, ctail, cwc, cbc, ns).astype(BF16)
        xg = _conv_silu_chunk(x_ref, xtail, cwx, cbx, gs)
        cb = _dot_nt(cg, bg)
        xdt = xg * ex_s[2 * CHUNK:3 * CHUNK, gs]
        ys = []
        for pr in range(hpg // 2):
            ws = []
            for hh in range(2):
                h = g * hpg + 2 * pr + hh
                colb = jnp.broadcast_to(cum2[:, h:h + 1], (CHUNK, CHUNK))
                rowb = jnp.broadcast_to(cum2_t[h:h + 1, :], (CHUNK, CHUNK))
                decay = jnp.exp2(jnp.where(causal, colb - rowb, -jnp.inf))
                ws.append((cb * decay).astype(BF16))
            w2 = jnp.concatenate(ws, axis=1)
            xp = xdt[:, pr * LANES:(pr + 1) * LANES]
            x2 = jnp.concatenate([jnp.where(head0, xp, 0.0), jnp.where(head0, 0.0, xp)], axis=0).astype(BF16)
            ys.append(_dot(w2, x2))
        st = state[g]
        y_state = _dot(cg, st.astype(BF16))
        y_ref[:, gs] = jnp.concatenate(ys, axis=1) + y_state * ex_s[0:CHUNK, gs] + dexp_ref[:, gs] * xg
        xs = (xg * ex_s[CHUNK:2 * CHUNK, gs]).astype(BF16)
        state[g] = dec_x[:, gs] * st + _dot_tn(bg, xs)

    @pl.when(c == nc - 1)
    def _():
        for g in range(SSD_GROUPS):
            st_ref[0, g * hpg:(g + 1) * hpg, :, :] = state[g].T.reshape(hpg, SSD_HEAD_DIM, SSD_STATE)


def _ssd_prompt(u, dt_raw, conv_w, conv_b, dt_bias, a_log, d_exp, e_mat, batch, seq, q_s, new_t, caches):
    nc = seq // CHUNK
    cwx, cwb, cwc = conv_w[:, :D_INNER], conv_w[:, D_INNER:D_INNER + SSD_GROUP_W], conv_w[:, D_INNER + SSD_GROUP_W:]
    cbx, cbb, cbc = conv_b[:, :D_INNER], conv_b[:, D_INNER:D_INNER + SSD_GROUP_W], conv_b[:, D_INNER + SSD_GROUP_W:]
    full = lambda shape: pl.BlockSpec(shape, lambda b, c: (0,) * len(shape))
    dbatch, _, _, n_new, _ = q_s.shape
    steps = batch * nc
    assert (dbatch * HEADS) % steps == 0, "sample heads must spread evenly over the prompt's SSD steps"
    hps = dbatch * HEADS // steps
    assert HEADS % hps == 0
    spb = HEADS // hps
    sb = lambda b, c: (b * nc + c) // spb
    hb = lambda b, c: (b * nc + c) % spb
    q_spec = pl.BlockSpec((1, hps, len(ATTN_GROUPS), n_new, HEAD_DIM), lambda b, c: (sb(b, c), hb(b, c), 0, 0, 0))
    new_spec = pl.BlockSpec((1, hps, HEAD_DIM, LANES), lambda b, c: (sb(b, c), hb(b, c), 0, 0))
    ao_spec = pl.BlockSpec((1, hps, n_new, HEAD_DIM), lambda b, c: (sb(b, c), hb(b, c), 0, 0))
    cache_specs = [pl.BlockSpec((1, 2, hps, HEAD_DIM, w), lambda b, c: (sb(b, c), 0, hb(b, c), 0, 0))
                   for w, _ in ATTN_GROUPS]
    return pl.pallas_call(
        _ssd_prompt_kernel,
        grid=(batch, nc),
        in_specs=[
            pl.BlockSpec((CHUNK, D_INNER), lambda b, c: (b * nc + c, X_OFF // D_INNER)),
            pl.BlockSpec((CHUNK, SSD_GROUP_W), lambda b, c: (b * nc + c, B_OFF // SSD_GROUP_W)),
            pl.BlockSpec((CHUNK, SSD_GROUP_W), lambda b, c: (b * nc + c, C_OFF // SSD_GROUP_W)),
            pl.BlockSpec((CHUNK, LANES), lambda b, c: (b * nc + c, 0)),
            full((CONV_W, D_INNER)), full((CONV_W, SSD_GROUP_W)), full((CONV_W, SSD_GROUP_W)),
            full((1, D_INNER)), full((1, SSD_GROUP_W)), full((1, SSD_GROUP_W)),
            full((1, LANES)), full((1, LANES)), full((1, D_INNER)), full((LANES, D_INNER)),
            q_spec, new_spec,
        ] + cache_specs,
        out_specs=[
            pl.BlockSpec((CHUNK, D_INNER), lambda b, c: (b * nc + c, 0)),
            pl.BlockSpec((1, SSD_HEADS, SSD_HEAD_DIM, SSD_STATE), lambda b, c: (b, 0, 0, 0)),
            ao_spec,
        ] + cache_specs,
        out_shape=[
            jax.ShapeDtypeStruct((batch * seq, D_INNER), F32),
            jax.ShapeDtypeStruct((batch, SSD_HEADS, SSD_HEAD_DIM, SSD_STATE), F32),
            jax.ShapeDtypeStruct((dbatch, HEADS, n_new, HEAD_DIM), F32),
        ] + [jax.ShapeDtypeStruct(cc.shape, F32) for cc in caches],
        scratch_shapes=[
            pltpu.VMEM((HALO, D_INNER), F32),
            pltpu.VMEM((HALO, SSD_GROUP_W), F32),
            pltpu.VMEM((HALO, SSD_GROUP_W), F32),
            pltpu.VMEM((SSD_GROUPS, SSD_STATE, SSD_GROUP_W), F32),
            pltpu.VMEM((3 * CHUNK, D_INNER), F32),
        ],
        compiler_params=_cparams(("parallel", "arbitrary")),
        name="ssd_prompt",
    )(u, u, u, dt_raw, cwx, cwb, cwc, cbx, cbb, cbc, dt_bias, a_log, d_exp, e_mat, q_s, new_t, *caches)


def _mix_ffn_kernel(a_ref, y_ref, z_ref, ga_ref, gb_ref, x_ref, wa_ref, ws_ref, wo_ref,
                    sn_ref, pn_ref, wu_ref, wd_ref, n1_ref, n2_ref, o_ref):
    a_proj = _dot(a_ref[...].astype(BF16), wa_ref[...])
    g = y_ref[...] * _silu(z_ref[...].astype(F32))
    parts = []
    for i in range(SSD_GROUPS):
        gi = g[:, i * SSD_GROUP_W:(i + 1) * SSD_GROUP_W]
        parts.append(gi * lax.rsqrt(jnp.mean(gi * gi, axis=-1, keepdims=True) + RMS_EPS))
    yn = (jnp.concatenate(parts, axis=1) * sn_ref[...]).astype(BF16)
    b_proj = _dot(yn, ws_ref[...])
    mixed_in = _sigmoid(ga_ref[...].astype(F32)) * a_proj + _sigmoid(gb_ref[...].astype(F32)) * b_proj
    mixed = _dot(mixed_in.astype(BF16), wo_ref[...])
    x1 = x_ref[...] + _rms(mixed, pn_ref[...])
    h = _rms(x1, n1_ref[...]).astype(BF16)
    gu = _dot(h, wu_ref[...])
    act = (_silu(gu[:, :D_FF]) * gu[:, D_FF:]).astype(BF16)
    f = _dot(act, wd_ref[...])
    o_ref[...] = x1 + _rms(f, n2_ref[...])


def _mix_ffn(a_out, y, u, x, w_attn_out, w_ssd_out, w_out, ssd_norm, post_mix_norm,
             w_up, w_down, pre_norm, post_norm, tm):
    t = x.shape[0]
    const = lambda shape: pl.BlockSpec(shape, lambda i: (0, 0), pipeline_mode=pl.Buffered(1))
    return pl.pallas_call(
        _mix_ffn_kernel,
        grid=(t // tm,),
        in_specs=[
            pl.BlockSpec((tm, A_WIDTH), lambda i: (i, 0)),
            pl.BlockSpec((tm, D_INNER), lambda i: (i, 0)),
            pl.BlockSpec((tm, D_INNER), lambda i: (i, Z_OFF // D_INNER)),
            pl.BlockSpec((tm, D_MODEL), lambda i: (i, GATE_OFF // D_MODEL)),
            pl.BlockSpec((tm, D_MODEL), lambda i: (i, GATE_OFF // D_MODEL + 1)),
            pl.BlockSpec((tm, D_MODEL), lambda i: (i, 0)),
            const((A_WIDTH, D_MODEL)), const((D_INNER, D_MODEL)), const((D_MODEL, D_MODEL)),
            const((1, D_INNER)), const((1, D_MODEL)),
            const((D_MODEL, 2 * D_FF)), const((D_FF, D_MODEL)),
            const((1, D_MODEL)), const((1, D_MODEL)),
        ],
        out_specs=pl.BlockSpec((tm, D_MODEL), lambda i: (i, 0)),
        out_shape=jax.ShapeDtypeStruct((t, D_MODEL), F32),
        compiler_params=_cparams(("parallel",)),
        name="mix_ffn",
    )(a_out, y, u, u, u, x, w_attn_out, w_ssd_out, w_out, ssd_norm, post_mix_norm,
      w_up, w_down, pre_norm, post_norm)


def _attn_sample_pieces(hx, q_ref, new_ref, caches, ao_ref, outs):
    n_new = q_ref.shape[3]
    run = {}

    def piece(g):
        win, dil = ATTN_GROUPS[g]
        new_t = new_ref[0, hx]
        q = (q_ref[0, hx, g] * ATTN_SCALE).astype(BF16)
        k_t = caches[g][0, 0, hx]
        v_t = caches[g][0, 1, hx]
        knt = pltpu.roll(new_t, LANES - n_new - (2 * g) * n_new, 1)
        vnt = pltpu.roll(new_t, LANES - n_new - (2 * g + 1) * n_new, 1)
        qi = lax.broadcasted_iota(jnp.int32, (n_new, win), 0)
        ci = lax.broadcasted_iota(jnp.int32, (n_new, win), 1)
        ok_c = jnp.logical_and(ci >= qi, ((ci - qi) & (dil - 1)) == 0)
        qn = lax.broadcasted_iota(jnp.int32, (n_new, LANES), 0)
        nn = lax.broadcasted_iota(jnp.int32, (n_new, LANES), 1) - (LANES - n_new)
        ok_n = jnp.logical_and(jnp.logical_and(nn >= 0, nn <= qn), ((qn - nn) & (dil - 1)) == 0)
        s_c = jnp.where(ok_c, _dot(q, k_t.astype(BF16)), -jnp.inf)
        s_n = jnp.where(ok_n, _dot(q, knt.astype(BF16)), -jnp.inf)
        m = jnp.maximum(jnp.max(s_c, axis=1, keepdims=True), jnp.max(s_n, axis=1, keepdims=True))
        p_c = jnp.exp(s_c - m)
        p_n = jnp.exp(s_n - m)
        den = jnp.sum(p_c, axis=1, keepdims=True) + jnp.sum(p_n, axis=1, keepdims=True)
        o = _dot_nt(p_c.astype(BF16), v_t.astype(BF16)) + _dot_nt(p_n.astype(BF16), vnt.astype(BF16))
        if not run:
            run.update(m=m, l=den, o=o)
        else:
            m_new = jnp.maximum(run["m"], m)
            ea = jnp.exp(run["m"] - m_new)
            eb = jnp.exp(m - m_new)
            run.update(m=m_new, l=ea * run["l"] + eb * den, o=ea * run["o"] + eb * o)
        if g == len(ATTN_GROUPS) - 1:
            ao_ref[0, hx] = run["o"] * (1.0 / run["l"])
        lane = lax.broadcasted_iota(jnp.int32, (HEAD_DIM, LANES), 1)
        keep = lane < LANES - n_new
        for which, (src, new) in enumerate(((k_t, knt), (v_t, vnt))):
            rolled = pltpu.roll(src, win - n_new, 1)
            if win > LANES:
                outs[g][0, which, hx, :, 0:win - LANES] = rolled[:, 0:win - LANES]
            outs[g][0, which, hx, :, win - LANES:win] = jnp.where(keep, rolled[:, win - LANES:win], new)

    return [functools.partial(piece, g) for g in range(len(ATTN_GROUPS))]


def _ssd_sample_kernel(x_ref, b_ref, c_ref, dt_ref, sx_ref, sb_ref, sc_ref, st_in,
                       cwx, cwb, cwc, cbx, cbb, cbc, dtb_ref, alog_ref, dexp_ref, e_ref,
                       y_ref, st_out, xpad, bpad, cpad):
    n = x_ref.shape[0]
    keep = CONV_W - 1
    for pad, new, old in ((xpad, x_ref, sx_ref), (bpad, b_ref, sb_ref), (cpad, c_ref, sc_ref)):
        pad[0:HALO, :] = jnp.zeros((HALO, pad.shape[1]), F32)
        pad[HALO - keep:HALO, :] = old[0]
        pad[HALO:HALO + n, :] = new[...]
    xc = _conv_silu(xpad, cwx, cbx, n)
    bm = _conv_silu(bpad, cwb, cbb, n)
    cm = _conv_silu(cpad, cwc, cbc, n)

    dt = _softplus(dt_ref[...] + dtb_ref[...])
    a = -jnp.exp(alog_ref[...])
    dta = dt * a
    ti = lax.broadcasted_iota(jnp.int32, (n, n), 0)
    si = lax.broadcasted_iota(jnp.int32, (n, n), 1)
    tril = jnp.where(si <= ti, 1.0, 0.0).astype(F32)
    cum = jnp.dot(tril, dta, preferred_element_type=F32, precision=lax.Precision.HIGHEST)
    cum_last = cum[n - 1:n, :]
    expcum = jnp.exp(cum)
    to_end = jnp.exp(cum_last - cum) * dt

    hpg = SSD_HEADS // SSD_GROUPS
    lane = lax.broadcasted_iota(jnp.int32, (n, LANES), 1)
    trow = lax.broadcasted_iota(jnp.int32, (n, LANES), 0)
    cbs = [_dot_nt(cm[:, g * SSD_STATE:(g + 1) * SSD_STATE].astype(BF16),
                   bm[:, g * SSD_STATE:(g + 1) * SSD_STATE].astype(BF16)) for g in range(SSD_GROUPS)]
    coefs = []
    for s in range(n):
        cbh = jnp.zeros((n, LANES), F32)
        for g in range(SSD_GROUPS):
            in_g = jnp.logical_and(lane >= g * hpg, lane < (g + 1) * hpg)
            cbh = jnp.where(in_g, jnp.broadcast_to(cbs[g][:, s:s + 1], (n, LANES)), cbh)
        decay = jnp.exp(jnp.where(trow >= s, cum - cum[s:s + 1, :], -jnp.inf))
        coefs.append(cbh * decay * dt[s:s + 1, :])
    ex = _expand01(jnp.concatenate(coefs + [expcum, to_end], axis=0), e_ref[...])
    y = dexp_ref[...] * xc
    for s in range(n):
        y = y + ex[s * n:(s + 1) * n, :] * xc[s:s + 1, :]
    expcum_x = ex[n * n:n * n + n, :]
    to_end_x = ex[n * n + n:n * n + 2 * n, :]
    xs = xc * to_end_x
    dec_t = jnp.broadcast_to(expcum[n - 1:n, :], (LANES, LANES)).T
    ys = []
    for g in range(SSD_GROUPS):
        gs = slice(g * SSD_GROUP_W, (g + 1) * SSD_GROUP_W)
        ns = slice(g * SSD_STATE, (g + 1) * SSD_STATE)
        st = st_in[0, g * hpg:(g + 1) * hpg].reshape(SSD_GROUP_W, SSD_STATE)
        ys.append(_dot_nt(cm[:, ns].astype(BF16), st.astype(BF16)))
        upd = _dot_tn(xs[:, gs].astype(BF16), bm[:, ns].astype(BF16))
        for hh in range(hpg):
            h = g * hpg + hh
            rows = slice(hh * SSD_HEAD_DIM, (hh + 1) * SSD_HEAD_DIM)
            st_out[0, h] = dec_t[h:h + 1, :] * st[rows, :] + upd[rows, :]
    y_ref[...] = y + jnp.concatenate(ys, axis=1) * expcum_x


def _ssd_sample(u, dt_raw, conv_state, ssm_state, conv_w, conv_b, dt_bias, a_log, d_exp, e_mat, n_new):
    bsz = ssm_state.shape[0]
    keep = CONV_W - 1
    cwx, cwb, cwc = conv_w[:, :D_INNER], conv_w[:, D_INNER:D_INNER + SSD_GROUP_W], conv_w[:, D_INNER + SSD_GROUP_W:]
    cbx, cbb, cbc = conv_b[:, :D_INNER], conv_b[:, D_INNER:D_INNER + SSD_GROUP_W], conv_b[:, D_INNER + SSD_GROUP_W:]
    full = lambda shape: pl.BlockSpec(shape, lambda b: (0,) * len(shape))
    st_spec = pl.BlockSpec((1, SSD_HEADS, SSD_HEAD_DIM, SSD_STATE), lambda b: (b, 0, 0, 0))
    return pl.pallas_call(
        _ssd_sample_kernel,
        grid=(bsz,),
        in_specs=[
            pl.BlockSpec((n_new, D_INNER), lambda b: (b, X_OFF // D_INNER)),
            pl.BlockSpec((n_new, SSD_GROUP_W), lambda b: (b, B_OFF // SSD_GROUP_W)),
            pl.BlockSpec((n_new, SSD_GROUP_W), lambda b: (b, C_OFF // SSD_GROUP_W)),
            pl.BlockSpec((n_new, LANES), lambda b: (b, 0)),
            pl.BlockSpec((1, keep, D_INNER), lambda b: (b, 0, 0)),
            pl.BlockSpec((1, keep, SSD_GROUP_W), lambda b: (b, 0, D_INNER // SSD_GROUP_W)),
            pl.BlockSpec((1, keep, SSD_GROUP_W), lambda b: (b, 0, D_INNER // SSD_GROUP_W + 1)),
            st_spec,
            full((CONV_W, D_INNER)), full((CONV_W, SSD_GROUP_W)), full((CONV_W, SSD_GROUP_W)),
            full((1, D_INNER)), full((1, SSD_GROUP_W)), full((1, SSD_GROUP_W)),
            full((1, LANES)), full((1, LANES)), full((1, D_INNER)), full((LANES, D_INNER)),
        ],
        out_specs=[pl.BlockSpec((n_new, D_INNER), lambda b: (b, 0)), st_spec],
        out_shape=[
            jax.ShapeDtypeStruct((bsz * n_new, D_INNER), F32),
            jax.ShapeDtypeStruct(ssm_state.shape, F32),
        ],
        scratch_shapes=[
            pltpu.VMEM((HALO + n_new, D_INNER), F32),
            pltpu.VMEM((HALO + n_new, SSD_GROUP_W), F32),
            pltpu.VMEM((HALO + n_new, SSD_GROUP_W), F32),
        ],
        compiler_params=_cparams(("parallel",)),
        name="ssd_sample",
    )(u, u, u, dt_raw, conv_state, conv_state, conv_state, ssm_state,
      cwx, cwb, cwc, cbx, cbb, cbc, dt_bias, a_log, d_exp, e_mat)


def _prep_weights(w_in, dt_bias, d_skip):
    q_end = 3 * len(ATTN_GROUPS) * A_WIDTH
    z_end = q_end + D_INNER
    x_end = z_end + D_INNER
    b_end = x_end + SSD_GROUP_W
    c_end = b_end + SSD_GROUP_W
    dt_end = c_end + SSD_HEADS
    w_main = jnp.concatenate(
        [w_in[:, q_end:z_end], w_in[:, dt_end:], w_in[:, z_end:c_end], w_in[:, :q_end]], axis=1).astype(BF16)
    w_dt = jnp.pad(w_in[:, c_end:dt_end], ((0, 0), (0, LANES - SSD_HEADS))).astype(BF16)
    dtb = jnp.pad(dt_bias, (0, LANES - SSD_HEADS)).reshape(1, LANES)
    d_exp = jnp.repeat(d_skip, SSD_HEAD_DIM).reshape(1, D_INNER)
    return w_main, w_dt, dtb, d_exp


def _tail_conv(u, batch, seq):
    u3 = u.reshape(batch, seq, N_REST)
    return u3[:, seq - (CONV_W - 1):, X_OFF:C_OFF + SSD_GROUP_W].astype(F32)


def kernel(x_prompt, x_sample, cache_kv_w128, cache_kv_w512, cache_kv_w2048, state_conv, state_ssm,
           pre_mix_norm, w_in, conv_w, conv_b, dt_bias, a_log, d_skip, ssd_norm, w_attn_out, w_ssd_out,
           w_out, post_mix_norm, pre_ffn_norm, w_up, w_down, post_ffn_norm):
    batch, seq, _ = x_prompt.shape
    dbatch, dseq, _ = x_sample.shape
    assert w_in.shape[0] == 1, "single-layer trunk"
    assert seq % ATTN_TILE == 0 and dseq == HALO

    w_main, w_dt, dtb, d_exp = _prep_weights(w_in[0], dt_bias[0], d_skip[0])
    alog = jnp.pad(a_log[0], (0, LANES - SSD_HEADS)).reshape(1, LANES)
    e_mat = (jnp.arange(LANES)[:, None] == (jnp.arange(D_INNER)[None, :] // SSD_HEAD_DIM)).astype(BF16)
    nw = pre_mix_norm[0].reshape(1, D_MODEL)
    cw, cb = conv_w[0], conv_b[0].reshape(1, -1)
    wa, ws, wo = w_attn_out[0].astype(BF16), w_ssd_out[0].astype(BF16), w_out[0].astype(BF16)
    wu, wd = w_up[0].astype(BF16), w_down[0].astype(BF16)
    sn = ssd_norm[0].reshape(1, D_INNER)
    pmn, pfn, qfn = (post_mix_norm[0].reshape(1, -1), pre_ffn_norm[0].reshape(1, -1),
                     post_ffn_norm[0].reshape(1, -1))

    xp = x_prompt.reshape(batch * seq, D_MODEL)
    xs = x_sample.reshape(dbatch * dseq, D_MODEL)
    ur_p, uq_p, dt_p = _inproj(xp, nw, w_main, w_dt, min(2048, batch * seq), 512, BF16)
    ur_s, uq_s, dt_s = _inproj(xs, nw, w_main, w_dt, min(1024, dbatch * dseq), 512, F32)
    qkv = uq_s.reshape(dbatch, dseq, 3, 3, HEADS, HEAD_DIM)
    q_s = qkv[:, :, :, 0].transpose(0, 3, 2, 1, 4)
    new_t = qkv[:, :, :, 1:3].transpose(0, 4, 5, 2, 3, 1).reshape(dbatch, HEADS, HEAD_DIM, 6 * dseq)
    new_t = jnp.pad(new_t, ((0, 0), (0, 0), (0, 0), (0, LANES - 6 * dseq)))
    caches = [c[0].transpose(0, 2, 3, 4, 1) for c in (cache_kv_w128, cache_kv_w512, cache_kv_w2048)]

    a_p, *kvt_p = _attn_prompt(uq_p, batch, seq)
    y_p, ssm_p, ao, nc0, nc1, nc2 = _ssd_prompt(ur_p, dt_p, cw, cb, dtb, alog, d_exp, e_mat, batch, seq,
                                                q_s, new_t, caches)
    out_p = _mix_ffn(a_p, y_p, ur_p, xp, wa, ws, wo, sn, pmn, wu, wd, pfn, qfn, 256).reshape(batch, seq, D_MODEL)
    kv_p = [t.transpose(0, 4, 1, 2, 3)[None] for t in kvt_p]
    conv_p = _tail_conv(ur_p, batch, seq)[None]

    a_s = ao.transpose(0, 2, 1, 3).reshape(dbatch * dseq, A_WIDTH)
    kv_s = [c.transpose(0, 4, 1, 2, 3)[None] for c in (nc0, nc1, nc2)]
    y_s, ssm_s = _ssd_sample(ur_s, dt_s, state_conv[0], state_ssm[0], cw, cb, dtb, alog, d_exp,
                             e_mat, dseq)
    tm_s = min(256, dbatch * dseq)
    out_s = _mix_ffn(a_s, y_s, ur_s, xs, wa, ws, wo, sn, pmn, wu, wd, pfn, qfn, tm_s).reshape(dbatch, dseq, D_MODEL)
    conv_s = _tail_conv(ur_s, dbatch, dseq)[None]

    return (out_p, out_s, kv_p[0], kv_p[1], kv_p[2], conv_p, ssm_p[None],
            kv_s[0], kv_s[1], kv_s[2], conv_s, ssm_s[None])
```

```python
import functools

import jax
import jax.numpy as jnp
from jax import lax
from jax.experimental import pallas as pl
from jax.experimental.pallas import tpu as pltpu

F32 = jnp.float32
BF16 = jnp.bfloat16

D_MODEL = 1024
ATTN_GROUPS = ((128, 1), (512, 4), (2048, 16))
BAND = 128
HEADS = 8
HEAD_DIM = 64
A_WIDTH = HEADS * HEAD_DIM
ATTN_SCALE = HEAD_DIM ** -0.5
LOG2E = 1.4426950408889634
ATTN_TILE = 2048
BLOCK_UNROLL = 4

D_INNER = 2048
SSD_HEADS = 32
SSD_HEAD_DIM = 64
SSD_GROUPS = 4
SSD_GROUP_W = D_INNER // SSD_GROUPS
SSD_STATE = 128
CONV_W = 4
CHUNK = 128
D_FF = 2816
RMS_EPS = 1e-6
LANES = 128
HALO = 8

Z_OFF = 0
GATE_OFF = 2048
X_OFF = 4096
B_OFF = 6144
C_OFF = 6656
N_REST = 7168
N_QKV = 3 * len(ATTN_GROUPS) * A_WIDTH
N_MAIN = N_REST + N_QKV

VMEM_LIMIT = 56 * 1024 * 1024


def _cparams(sem):
    return pltpu.CompilerParams(dimension_semantics=sem, vmem_limit_bytes=VMEM_LIMIT)


def _rms(x, w):
    return x * lax.rsqrt(jnp.mean(x * x, axis=-1, keepdims=True) + RMS_EPS) * w


def _silu(x):
    h = 0.5 * x
    return h + h * jnp.tanh(h)


def _sigmoid(x):
    return 0.5 + 0.5 * jnp.tanh(0.5 * x)


def _dot(a, b):
    return jnp.dot(a, b, preferred_element_type=F32)


def _dot_nt(a, b):
    return lax.dot_general(a, b, (((1,), (1,)), ((), ())), preferred_element_type=F32)


def _dot_tn(a, b):
    return lax.dot_general(a, b, (((0,), (0,)), ((), ())), preferred_element_type=F32)


def _expand01(a, e):
    hi = a.astype(BF16)
    mid = (a - hi.astype(F32)).astype(BF16)
    return _dot(hi, e) + _dot(mid, e)


def _inproj_kernel(x_ref, nw_ref, w_ref, wdt_ref, r_ref, q_ref, dt_ref, h_ref, *, rest_tiles):
    j = pl.program_id(1)

    @pl.when(j == 0)
    def _():
        hb = _rms(x_ref[...], nw_ref[...]).astype(BF16)
        h_ref[...] = hb
        dt_ref[...] = _dot(hb, wdt_ref[...])

    @pl.when(j < rest_tiles)
    def _():
        r_ref[...] = _dot(h_ref[...], w_ref[...]).astype(r_ref.dtype)

    @pl.when(j >= rest_tiles)
    def _():
        q_ref[...] = _dot(h_ref[...], w_ref[...])


def _inproj(x, nw, w_main, w_dt, tm, tn, rest_dtype):
    t = x.shape[0]
    rest_tiles = N_REST // tn
    return pl.pallas_call(
        functools.partial(_inproj_kernel, rest_tiles=rest_tiles),
        grid=(t // tm, N_MAIN // tn),
        in_specs=[
            pl.BlockSpec((tm, D_MODEL), lambda i, j: (i, 0)),
            pl.BlockSpec((1, D_MODEL), lambda i, j: (0, 0)),
            pl.BlockSpec((D_MODEL, tn), lambda i, j: (0, j)),
            pl.BlockSpec((D_MODEL, LANES), lambda i, j: (0, 0)),
        ],
        out_specs=[
            pl.BlockSpec((tm, tn), lambda i, j: (i, jnp.minimum(j, rest_tiles - 1))),
            pl.BlockSpec((tm, tn), lambda i, j: (i, jnp.maximum(j - rest_tiles, 0))),
            pl.BlockSpec((tm, LANES), lambda i, j: (i, 0)),
        ],
        out_shape=[
            jax.ShapeDtypeStruct((t, N_REST), rest_dtype),
            jax.ShapeDtypeStruct((t, N_QKV), F32),
            jax.ShapeDtypeStruct((t, LANES), F32),
        ],
        scratch_shapes=[pltpu.VMEM((tm, D_MODEL), BF16)],
        compiler_params=_cparams(("parallel", "arbitrary")),
        name="inproj",
    )(x, nw, w_main, w_dt)


def _unroll_for(trips):
    return max(u for u in range(1, BLOCK_UNROLL + 1) if trips % u == 0)


def _attn_prompt_kernel(q0, k0, v0, q1, k1, v1, q2, k2, v2, o_ref, kvo0, kvo1, kvo2,
                        kd0, vt0, kd1, vt1, kd2, vt2, lse_ref, acc_ref, s_buf, p_buf, stat_buf):
    n = pl.program_id(2)
    q_refs, k_refs, v_refs = (q0, q1, q2), (k0, k1, k2), (v0, v1, v2)
    kds, vts = (kd0, kd1, kd2), (vt0, vt1, vt2)
    krow = lax.broadcasted_iota(jnp.int32, (BAND, LANES), 0)
    qcol = lax.broadcasted_iota(jnp.int32, (BAND, LANES), 1)
    upper = krow > qcol
    diag = krow == qcol
    head0 = qcol < HEAD_DIM
    half = BAND // 2

    def group_stages(g):
        dil = ATTN_GROUPS[g][1]
        span = BAND * dil
        nsub = ATTN_TILE // span
        nblk = nsub * dil
        q_ref, k_ref, v_ref, kd, vt = q_refs[g], k_refs[g], v_refs[g], kds[g], vts[g]

        def split(i):
            if dil == 1:
                return i, 0
            j = i // dil
            return j, i - j * dil

        def rows_of(i):
            j, r = split(i)
            if dil == 1:
                return pl.ds(i * BAND if isinstance(i, int) else pl.multiple_of(i * BAND, BAND), BAND)
            return pl.ds(j * span + r, BAND, stride=dil)

        def carry():
            @pl.when(n == 0)
            def _():
                for r in range(dil):
                    kd[r * (nsub + 1)] = jnp.zeros((BAND, LANES), BF16)
                    vt[r * (nsub + 1)] = jnp.zeros((LANES, BAND), BF16)

            @pl.when(n > 0)
            def _():
                for r in range(dil):
                    kd[r * (nsub + 1)] = kd[r * (nsub + 1) + nsub]
                    vt[r * (nsub + 1)] = vt[r * (nsub + 1) + nsub]

        def deinterleave():
            for i in range(nblk):
                j, r = split(i)
                slot = r * (nsub + 1) + j + 1
                rows = rows_of(i)
                kd[slot] = k_ref[rows, :].astype(BF16)
                vt[slot] = v_ref[rows, :].T.astype(BF16)

        def qk_stage(i, sl):
            j, r = split(i)
            slot = r * (nsub + 1) + j
            q = q_ref[rows_of(i), :] * (ATTN_SCALE * LOG2E)
            q2 = jnp.concatenate([jnp.where(head0, q, 0.0), jnp.where(head0, 0.0, q)], axis=0).astype(BF16)
            kk = kd[pl.ds(slot, 2)].reshape(2 * BAND, LANES)
            s_buf[sl] = _dot_nt(kk, q2)

        def softmax_stage(sl, first):
            for hh in range(2):
                sp = s_buf[sl, 0:BAND, hh * BAND:(hh + 1) * BAND]
                sc = s_buf[sl, BAND:2 * BAND, hh * BAND:(hh + 1) * BAND]
                if first:
                    sp = jnp.where(n > 0, sp, -jnp.inf)
                comb = jnp.where(upper, sp, sc)
                dg = jnp.sum(jnp.where(diag, sp, 0.0), axis=0, keepdims=True)
                m = jnp.maximum(jnp.max(comb, axis=0, keepdims=True), dg)
                p = jnp.exp2(comb - m)
                pd = jnp.exp2(dg - m)
                den = jnp.sum(p, axis=0, keepdims=True) + pd
                p_buf[sl, hh, 0:BAND, :] = jnp.where(upper, p, jnp.where(diag, pd, 0.0)).astype(BF16)
                p_buf[sl, hh, BAND:2 * BAND, :] = jnp.where(upper, 0.0, p).astype(BF16)
                stat_buf[sl, 2 * hh:2 * hh + 1, :] = 1.0 / den
                stat_buf[sl, 2 * hh + 1:2 * hh + 2, :] = m + jnp.log2(den)

        def pv_stage(i, sl):
            j, r = split(i)
            slot = r * (nsub + 1) + j
            rows = rows_of(i)
            vprev = vt[slot]
            vcur = vt[slot + 1]
            ots, lts = [], []
            for hh in range(2):
                vsl = slice(hh * half, (hh + 1) * half)
                vtt = jnp.concatenate([vprev[vsl, :], vcur[vsl, :]], axis=1)
                ots.append(_dot(vtt, p_buf[sl, hh]) * stat_buf[sl, 2 * hh:2 * hh + 1, :])
                lts.append(jnp.broadcast_to(stat_buf[sl, 2 * hh + 1:2 * hh + 2, :], (half, BAND)))
            o_blk = jnp.concatenate(ots, axis=0).T
            lse_blk = jnp.concatenate(lts, axis=0).T
            if g == 0:
                lse_ref[rows, :] = lse_blk
                acc_ref[rows, :] = o_blk
            else:
                lse_run = lse_ref[rows, :]
                acc_run = acc_ref[rows, :]
                mx = jnp.maximum(lse_run, lse_blk)
                ea = jnp.exp2(lse_run - mx)
                eb = jnp.exp2(lse_blk - mx)
                tot = ea + eb
                inv = 1.0 / tot
                merged = (ea * inv) * acc_run + (eb * inv) * o_blk
                if g == len(ATTN_GROUPS) - 1:
                    o_ref[rows, :] = merged
                else:
                    lse_ref[rows, :] = mx + jnp.log2(tot)
                    acc_ref[rows, :] = merged

        def qk_pair(pair, par):
            qk_stage(2 * pair, 2 * par)
            qk_stage(2 * pair + 1, 2 * par + 1)

        def softmax_pair(pair, par):
            softmax_stage(2 * par, 2 * pair < dil)
            softmax_stage(2 * par + 1, 2 * pair + 1 < dil)

        def pv_pair(pair, par):
            pv_stage(2 * pair, 2 * par)
            pv_stage(2 * pair + 1, 2 * par + 1)

        return (carry, deinterleave), (qk_pair, softmax_pair, pv_pair), nblk // 2

    groups = [group_stages(g) for g in range(len(ATTN_GROUPS))]
    for (carry, _), _, _ in groups:
        carry()
    for (_, deinterleave), _, _ in groups:
        deinterleave()
    items = [(stages, pair) for _, stages, npairs in groups for pair in range(npairs)]
    for t in range(len(items) + 2):
        for depth in (2, 1, 0):
            if 0 <= t - depth < len(items):
                stages, pair = items[t - depth]
                stages[depth](pair, (t - depth) % 2)

    @pl.when(n == pl.num_programs(2) - 1)
    def _():
        for g, (win, _) in enumerate(ATTN_GROUPS):
            for which, ref in enumerate((k_refs[g], v_refs[g])):
                t = ref[ATTN_TILE - win:, :].T
                for hh in range(2):
                    (kvo0, kvo1, kvo2)[g][0, which, hh] = t[hh * HEAD_DIM:(hh + 1) * HEAD_DIM, :]


def _attn_prompt(u, batch, seq):
    nt = seq // ATTN_TILE
    hp = A_WIDTH // LANES
    qkv_blk = 0

    def spec(g, which):
        col = qkv_blk + (3 * g + which) * hp
        return pl.BlockSpec((ATTN_TILE, LANES), lambda b, h, n, col=col: (b * nt + n, col + h))

    in_specs = [spec(g, w) for g in range(3) for w in range(3)]
    scratch = []
    for _, dil in ATTN_GROUPS:
        nslots = dil * (ATTN_TILE // (BAND * dil) + 1)
        scratch += [pltpu.VMEM((nslots, BAND, LANES), BF16), pltpu.VMEM((nslots, LANES, BAND), BF16)]
    scratch += [pltpu.VMEM((ATTN_TILE, LANES), F32), pltpu.VMEM((ATTN_TILE, LANES), F32)]
    scratch += [pltpu.VMEM((4, 2 * BAND, 2 * BAND), F32), pltpu.VMEM((4, 2, 2 * BAND, BAND), BF16),
                pltpu.VMEM((4, HALO, BAND), F32)]
    return pl.pallas_call(
        _attn_prompt_kernel,
        grid=(batch, hp, nt),
        in_specs=in_specs,
        out_specs=[pl.BlockSpec((ATTN_TILE, LANES), lambda b, h, n: (b * nt + n, h))]
        + [pl.BlockSpec((1, 2, 2, HEAD_DIM, w), lambda b, h, n: (b, 0, h, 0, 0)) for w, _ in ATTN_GROUPS],
        out_shape=[jax.ShapeDtypeStruct((batch * seq, A_WIDTH), F32)]
        + [jax.ShapeDtypeStruct((batch, 2, HEADS, HEAD_DIM, w), F32) for w, _ in ATTN_GROUPS],
        scratch_shapes=scratch,
        compiler_params=_cparams(("parallel", "parallel", "arbitrary")),
        name="attn_prompt",
    )(*([u] * 9))


def _conv_silu(pad_ref, w_ref, b_ref, rows):
    acc = b_ref[...] + pad_ref[HALO - (CONV_W - 1):HALO - (CONV_W - 1) + rows, :] * w_ref[0:1, :]
    for i in range(1, CONV_W):
        off = HALO - (CONV_W - 1) + i
        acc = acc + pad_ref[off:off + rows, :] * w_ref[i:i + 1, :]
    return _silu(acc)


def _conv_silu_chunk(x_ref, tail_ref, w_ref, b_ref, rows, cols):
    x = x_ref[rows, cols].astype(F32)
    ext = jnp.concatenate([tail_ref[:, cols], x], axis=0)
    tail_ref[:, cols] = x[x.shape[0] - HALO:, :]
    acc = b_ref[:, cols] + x * w_ref[CONV_W - 1:CONV_W, cols]
    for k in range(1, CONV_W):
        acc = acc + pltpu.roll(ext, k, 0)[HALO:, :] * w_ref[CONV_W - 1 - k:CONV_W - k, cols]
    return _silu(acc)


def _softplus(x):
    return jnp.maximum(x, 0.0) + jnp.log1p(jnp.exp(-jnp.abs(x)))


def _ssd_prompt_kernel(x_ref, b_ref, c_ref, dt_ref, dtn_ref, cwx, cwb, cwc, cbx, cbb, cbc,
                       dtb_ref, alog_ref, dexp_ref, e_ref, q_ref, new_ref, c0, c1, c2,
                       y_ref, st_ref, ao_ref, o0, o1, o2,
                       xtail, btail, ctail, state, ex_a, ex_b, cum_a, cum_b):
    c = pl.program_id(1)
    nc = pl.num_programs(1)

    @pl.when(c == 0)
    def _():
        xtail[...] = jnp.zeros(xtail.shape, F32)
        btail[...] = jnp.zeros(btail.shape, F32)
        ctail[...] = jnp.zeros(ctail.shape, F32)
        state[...] = jnp.zeros(state.shape, F32)

    ti = lax.broadcasted_iota(jnp.int32, (CHUNK, CHUNK), 0)
    si = lax.broadcasted_iota(jnp.int32, (CHUNK, CHUNK), 1)
    causal = si <= ti
    lane = lax.broadcasted_iota(jnp.int32, (CHUNK, LANES), 1)
    head0 = lane < SSD_HEAD_DIM
    hpg = SSD_HEADS // SSD_GROUPS

    def prepare(dt_blk, ex_ref, cum_ref):
        dt = _softplus(dt_blk + dtb_ref[...])
        dta = dt * -jnp.exp(alog_ref[...])
        tril = jnp.where(causal, 1.0, 0.0).astype(F32)
        cum = jnp.dot(tril, dta, preferred_element_type=F32, precision=lax.Precision.HIGHEST)
        cum2 = cum * LOG2E
        cum_ref[0] = cum2
        cum_ref[1] = cum2.T
        expcum = jnp.exp2(cum2)
        to_end = jnp.exp2(cum2[CHUNK - 1:CHUNK, :] - cum2) * dt
        ex_ref[...] = _expand01(jnp.concatenate([expcum, to_end, dt], axis=0), e_ref[...])

    @pl.when(c == 0)
    def _():
        prepare(dt_ref[0:CHUNK, :], ex_a, cum_a)

    def chunk(rows, ex_s, cum_s, hooks):
        cum2 = cum_s[0]
        cum2_t = cum_s[1]
        dec_x = ex_s[CHUNK - 1:CHUNK, :]
        for g in range(SSD_GROUPS):
            for hook in hooks[g]:
                hook()
            gs = slice(g * SSD_GROUP_W, (g + 1) * SSD_GROUP_W)
            ns = slice(g * SSD_STATE, (g + 1) * SSD_STATE)
            bg = _conv_silu_chunk(b_ref, btail, cwb, cbb, rows, ns).astype(BF16)
            cg = _conv_silu_chunk(c_ref, ctail, cwc, cbc, rows, ns).astype(BF16)
            xg = _conv_silu_chunk(x_ref, xtail, cwx, cbx, rows, gs)
            cb = _dot_nt(cg, bg)
            xdt = xg * ex_s[2 * CHUNK:3 * CHUNK, gs]
            ys = []
            for pr in range(hpg // 2):
                ws = []
                for hh in range(2):
                    h = g * hpg + 2 * pr + hh
                    colb = jnp.broadcast_to(cum2[:, h:h + 1], (CHUNK, CHUNK))
                    rowb = jnp.broadcast_to(cum2_t[h:h + 1, :], (CHUNK, CHUNK))
                    decay = jnp.exp2(jnp.where(causal, colb - rowb, -jnp.inf))
                    ws.append((cb * decay).astype(BF16))
                w2 = jnp.concatenate(ws, axis=1)
                xp = xdt[:, pr * LANES:(pr + 1) * LANES]
                x2 = jnp.concatenate([jnp.where(head0, xp, 0.0), jnp.where(head0, 0.0, xp)], axis=0).astype(BF16)
                ys.append(_dot(w2, x2))
            st = state[g]
            y_state = _dot(cg, st.astype(BF16))
            y_ref[rows, gs] = jnp.concatenate(ys, axis=1) + y_state * ex_s[0:CHUNK, gs] + dexp_ref[:, gs] * xg
            xs = (xg * ex_s[CHUNK:2 * CHUNK, gs]).astype(BF16)
            state[g] = dec_x[:, gs] * st + _dot_tn(bg, xs)

    sample_pieces = [p for hx in range(q_ref.shape[1])
                     for p in _attn_sample_pieces(hx, q_ref, new_ref, (c0, c1, c2), ao_ref, (o0, o1, o2))]
    nslots = 2 * SSD_GROUPS
    hooks = [[] for _ in range(nslots)]
    for idx, piece in enumerate(sample_pieces):
        hooks[idx * nslots // len(sample_pieces)].append(piece)
    hooks[1].append(lambda: prepare(dt_ref[CHUNK:2 * CHUNK, :], ex_b, cum_b))
    hooks[SSD_GROUPS + 1].append(lambda: prepare(dtn_ref[...], ex_a, cum_a))
    chunk(slice(0, CHUNK), ex_a, cum_a, hooks[:SSD_GROUPS])
    chunk(slice(CHUNK, 2 * CHUNK), ex_b, cum_b, hooks[SSD_GROUPS:])

    @pl.when(c == nc - 1)
    def _():
        for g in range(SSD_GROUPS):
            st_ref[0, g * hpg:(g + 1) * hpg, :, :] = state[g].T.reshape(hpg, SSD_HEAD_DIM, SSD_STATE)


def _ssd_prompt(u, dt_raw, conv_w, conv_b, dt_bias, a_log, d_exp, e_mat, batch, seq, q_s, new_t, caches):
    nc = seq // CHUNK
    cwx, cwb, cwc = conv_w[:, :D_INNER], conv_w[:, D_INNER:D_INNER + SSD_GROUP_W], conv_w[:, D_INNER + SSD_GROUP_W:]
    cbx, cbb, cbc = conv_b[:, :D_INNER], conv_b[:, D_INNER:D_INNER + SSD_GROUP_W], conv_b[:, D_INNER + SSD_GROUP_W:]
    full = lambda shape: pl.BlockSpec(shape, lambda b, c: (0,) * len(shape))
    dbatch, _, _, n_new, _ = q_s.shape
    assert nc % 2 == 0
    nchunks, nc, rows = nc, nc // 2, 2 * CHUNK
    steps = batch * nc
    assert (dbatch * HEADS) % steps == 0, "sample heads must spread evenly over the prompt's SSD steps"
    hps = dbatch * HEADS // steps
    assert HEADS % hps == 0
    spb = HEADS // hps
    sb = lambda b, c: (b * nc + c) // spb
    hb = lambda b, c: (b * nc + c) % spb
    q_spec = pl.BlockSpec((1, hps, len(ATTN_GROUPS), n_new, HEAD_DIM), lambda b, c: (sb(b, c), hb(b, c), 0, 0, 0))
    new_spec = pl.BlockSpec((1, hps, HEAD_DIM, LANES), lambda b, c: (sb(b, c), hb(b, c), 0, 0))
    ao_spec = pl.BlockSpec((1, hps, n_new, HEAD_DIM), lambda b, c: (sb(b, c), hb(b, c), 0, 0))
    cache_specs = [pl.BlockSpec((1, 2, hps, HEAD_DIM, w), lambda b, c: (sb(b, c), 0, hb(b, c), 0, 0))
                   for w, _ in ATTN_GROUPS]
    return pl.pallas_call(
        _ssd_prompt_kernel,
        grid=(batch, nc),
        in_specs=[
            pl.BlockSpec((rows, D_INNER), lambda b, c: (b * nc + c, X_OFF // D_INNER)),
            pl.BlockSpec((rows, SSD_GROUP_W), lambda b, c: (b * nc + c, B_OFF // SSD_GROUP_W)),
            pl.BlockSpec((rows, SSD_GROUP_W), lambda b, c: (b * nc + c, C_OFF // SSD_GROUP_W)),
            pl.BlockSpec((rows, LANES), lambda b, c: (b * nc + c, 0)),
            pl.BlockSpec((CHUNK, LANES), lambda b, c: (b * nchunks + jnp.minimum(2 * c + 2, nchunks - 1), 0)),
            full((CONV_W, D_INNER)), full((CONV_W, SSD_GROUP_W)), full((CONV_W, SSD_GROUP_W)),
            full((1, D_INNER)), full((1, SSD_GROUP_W)), full((1, SSD_GROUP_W)),
            full((1, LANES)), full((1, LANES)), full((1, D_INNER)), full((LANES, D_INNER)),
            q_spec, new_spec,
        ] + cache_specs,
        out_specs=[
            pl.BlockSpec((rows, D_INNER), lambda b, c: (b * nc + c, 0)),
            pl.BlockSpec((1, SSD_HEADS, SSD_HEAD_DIM, SSD_STATE), lambda b, c: (b, 0, 0, 0)),
            ao_spec,
        ] + cache_specs,
        out_shape=[
            jax.ShapeDtypeStruct((batch * seq, D_INNER), F32),
            jax.ShapeDtypeStruct((batch, SSD_HEADS, SSD_HEAD_DIM, SSD_STATE), F32),
            jax.ShapeDtypeStruct((dbatch, HEADS, n_new, HEAD_DIM), F32),
        ] + [jax.ShapeDtypeStruct(cc.shape, F32) for cc in caches],
        scratch_shapes=[
            pltpu.VMEM((HALO, D_INNER), F32),
            pltpu.VMEM((HALO, SSD_GROUP_W), F32),
            pltpu.VMEM((HALO, SSD_GROUP_W), F32),
            pltpu.VMEM((SSD_GROUPS, SSD_STATE, SSD_GROUP_W), F32),
            pltpu.VMEM((3 * CHUNK, D_INNER), F32), pltpu.VMEM((3 * CHUNK, D_INNER), F32),
            pltpu.VMEM((2, CHUNK, LANES), F32), pltpu.VMEM((2, CHUNK, LANES), F32),
        ],
        compiler_params=_cparams(("parallel", "arbitrary")),
        name="ssd_prompt",
    )(u, u, u, dt_raw, dt_raw, cwx, cwb, cwc, cbx, cbb, cbc, dt_bias, a_log, d_exp, e_mat, q_s, new_t, *caches)


def _mix_ffn_kernel(a_ref, y_ref, z_ref, ga_ref, gb_ref, x_ref, wa_ref, ws_ref, wo_ref,
                    sn_ref, pn_ref, wu_ref, wd_ref, n1_ref, n2_ref, o_ref):
    a_proj = _dot(a_ref[...].astype(BF16), wa_ref[...])
    g = y_ref[...] * _silu(z_ref[...].astype(F32))
    parts = []
    for i in range(SSD_GROUPS):
        gi = g[:, i * SSD_GROUP_W:(i + 1) * SSD_GROUP_W]
        parts.append(gi * lax.rsqrt(jnp.mean(gi * gi, axis=-1, keepdims=True) + RMS_EPS))
    yn = (jnp.concatenate(parts, axis=1) * sn_ref[...]).astype(BF16)
    b_proj = _dot(yn, ws_ref[...])
    mixed_in = _sigmoid(ga_ref[...].astype(F32)) * a_proj + _sigmoid(gb_ref[...].astype(F32)) * b_proj
    mixed = _dot(mixed_in.astype(BF16), wo_ref[...])
    x1 = x_ref[...] + _rms(mixed, pn_ref[...])
    h = _rms(x1, n1_ref[...]).astype(BF16)
    gu = _dot(h, wu_ref[...])
    act = (_silu(gu[:, :D_FF]) * gu[:, D_FF:]).astype(BF16)
    f = _dot(act, wd_ref[...])
    o_ref[...] = x1 + _rms(f, n2_ref[...])


def _mix_ffn(a_out, y, u, x, w_attn_out, w_ssd_out, w_out, ssd_norm, post_mix_norm,
             w_up, w_down, pre_norm, post_norm, tm):
    t = x.shape[0]
    const = lambda shape: pl.BlockSpec(shape, lambda i: (0, 0), pipeline_mode=pl.Buffered(1))
    return pl.pallas_call(
        _mix_ffn_kernel,
        grid=(t // tm,),
        in_specs=[
            pl.BlockSpec((tm, A_WIDTH), lambda i: (i, 0)),
            pl.BlockSpec((tm, D_INNER), lambda i: (i, 0)),
            pl.BlockSpec((tm, D_INNER), lambda i: (i, Z_OFF // D_INNER)),
            pl.BlockSpec((tm, D_MODEL), lambda i: (i, GATE_OFF // D_MODEL)),
            pl.BlockSpec((tm, D_MODEL), lambda i: (i, GATE_OFF // D_MODEL + 1)),
            pl.BlockSpec((tm, D_MODEL), lambda i: (i, 0)),
            const((A_WIDTH, D_MODEL)), const((D_INNER, D_MODEL)), const((D_MODEL, D_MODEL)),
            const((1, D_INNER)), const((1, D_MODEL)),
            const((D_MODEL, 2 * D_FF)), const((D_FF, D_MODEL)),
            const((1, D_MODEL)), const((1, D_MODEL)),
        ],
        out_specs=pl.BlockSpec((tm, D_MODEL), lambda i: (i, 0)),
        out_shape=jax.ShapeDtypeStruct((t, D_MODEL), F32),
        compiler_params=_cparams(("parallel",)),
        name="mix_ffn",
    )(a_out, y, u, u, u, x, w_attn_out, w_ssd_out, w_out, ssd_norm, post_mix_norm,
      w_up, w_down, pre_norm, post_norm)


def _attn_sample_pieces(hx, q_ref, new_ref, caches, ao_ref, outs):
    n_new = q_ref.shape[3]
    run = {}

    def piece(g):
        win, dil = ATTN_GROUPS[g]
        new_t = new_ref[0, hx]
        q = (q_ref[0, hx, g] * ATTN_SCALE).astype(BF16)
        k_t = caches[g][0, 0, hx]
        v_t = caches[g][0, 1, hx]
        knt = pltpu.roll(new_t, LANES - n_new - (2 * g) * n_new, 1)
        vnt = pltpu.roll(new_t, LANES - n_new - (2 * g + 1) * n_new, 1)
        qi = lax.broadcasted_iota(jnp.int32, (n_new, win), 0)
        ci = lax.broadcasted_iota(jnp.int32, (n_new, win), 1)
        ok_c = jnp.logical_and(ci >= qi, ((ci - qi) & (dil - 1)) == 0)
        qn = lax.broadcasted_iota(jnp.int32, (n_new, LANES), 0)
        nn = lax.broadcasted_iota(jnp.int32, (n_new, LANES), 1) - (LANES - n_new)
        ok_n = jnp.logical_and(jnp.logical_and(nn >= 0, nn <= qn), ((qn - nn) & (dil - 1)) == 0)
        s_c = jnp.where(ok_c, _dot(q, k_t.astype(BF16)), -jnp.inf)
        s_n = jnp.where(ok_n, _dot(q, knt.astype(BF16)), -jnp.inf)
        m = jnp.maximum(jnp.max(s_c, axis=1, keepdims=True), jnp.max(s_n, axis=1, keepdims=True))
        p_c = jnp.exp(s_c - m)
        p_n = jnp.exp(s_n - m)
        den = jnp.sum(p_c, axis=1, keepdims=True) + jnp.sum(p_n, axis=1, keepdims=True)
        o = _dot_nt(p_c.astype(BF16), v_t.astype(BF16)) + _dot_nt(p_n.astype(BF16), vnt.astype(BF16))
        if not run:
            run.update(m=m, l=den, o=o)
        else:
            m_new = jnp.maximum(run["m"], m)
            ea = jnp.exp(run["m"] - m_new)
            eb = jnp.exp(m - m_new)
            run.update(m=m_new, l=ea * run["l"] + eb * den, o=ea * run["o"] + eb * o)
        if g == len(ATTN_GROUPS) - 1:
            ao_ref[0, hx] = run["o"] * (1.0 / run["l"])
        lane = lax.broadcasted_iota(jnp.int32, (HEAD_DIM, LANES), 1)
        keep = lane < LANES - n_new
        for which, (src, new) in enumerate(((k_t, knt), (v_t, vnt))):
            rolled = pltpu.roll(src, win - n_new, 1)
            if win > LANES:
                outs[g][0, which, hx, :, 0:win - LANES] = rolled[:, 0:win - LANES]
            outs[g][0, which, hx, :, win - LANES:win] = jnp.where(keep, rolled[:, win - LANES:win], new)

    return [functools.partial(piece, g) for g in range(len(ATTN_GROUPS))]


def _ssd_sample_kernel(x_ref, b_ref, c_ref, dt_ref, sx_ref, sb_ref, sc_ref, st_in,
                       cwx, cwb, cwc, cbx, cbb, cbc, dtb_ref, alog_ref, dexp_ref, e_ref,
                       y_ref, st_out, xpad, bpad, cpad):
    n = x_ref.shape[0]
    keep = CONV_W - 1
    for pad, new, old in ((xpad, x_ref, sx_ref), (bpad, b_ref, sb_ref), (cpad, c_ref, sc_ref)):
        pad[0:HALO, :] = jnp.zeros((HALO, pad.shape[1]), F32)
        pad[HALO - keep:HALO, :] = old[0]
        pad[HALO:HALO + n, :] = new[...]
    xc = _conv_silu(xpad, cwx, cbx, n)
    bm = _conv_silu(bpad, cwb, cbb, n)
    cm = _conv_silu(cpad, cwc, cbc, n)

    dt = _softplus(dt_ref[...] + dtb_ref[...])
    a = -jnp.exp(alog_ref[...])
    dta = dt * a
    ti = lax.broadcasted_iota(jnp.int32, (n, n), 0)
    si = lax.broadcasted_iota(jnp.int32, (n, n), 1)
    tril = jnp.where(si <= ti, 1.0, 0.0).astype(F32)
    cum = jnp.dot(tril, dta, preferred_element_type=F32, precision=lax.Precision.HIGHEST)
    cum_last = cum[n - 1:n, :]
    expcum = jnp.exp(cum)
    to_end = jnp.exp(cum_last - cum) * dt

    hpg = SSD_HEADS // SSD_GROUPS
    lane = lax.broadcasted_iota(jnp.int32, (n, LANES), 1)
    trow = lax.broadcasted_iota(jnp.int32, (n, LANES), 0)
    cbs = [_dot_nt(cm[:, g * SSD_STATE:(g + 1) * SSD_STATE].astype(BF16),
                   bm[:, g * SSD_STATE:(g + 1) * SSD_STATE].astype(BF16)) for g in range(SSD_GROUPS)]
    coefs = []
    for s in range(n):
        cbh = jnp.zeros((n, LANES), F32)
        for g in range(SSD_GROUPS):
            in_g = jnp.logical_and(lane >= g * hpg, lane < (g + 1) * hpg)
            cbh = jnp.where(in_g, jnp.broadcast_to(cbs[g][:, s:s + 1], (n, LANES)), cbh)
        decay = jnp.exp(jnp.where(trow >= s, cum - cum[s:s + 1, :], -jnp.inf))
        coefs.append(cbh * decay * dt[s:s + 1, :])
    ex = _expand01(jnp.concatenate(coefs + [expcum, to_end], axis=0), e_ref[...])
    y = dexp_ref[...] * xc
    for s in range(n):
        y = y + ex[s * n:(s + 1) * n, :] * xc[s:s + 1, :]
    expcum_x = ex[n * n:n * n + n, :]
    to_end_x = ex[n * n + n:n * n + 2 * n, :]
    xs = xc * to_end_x
    dec_t = jnp.broadcast_to(expcum[n - 1:n, :], (LANES, LANES)).T
    ys = []
    for g in range(SSD_GROUPS):
        gs = slice(g * SSD_GROUP_W, (g + 1) * SSD_GROUP_W)
        ns = slice(g * SSD_STATE, (g + 1) * SSD_STATE)
        st = st_in[0, g * hpg:(g + 1) * hpg].reshape(SSD_GROUP_W, SSD_STATE)
        ys.append(_dot_nt(cm[:, ns].astype(BF16), st.astype(BF16)))
        upd = _dot_tn(xs[:, gs].astype(BF16), bm[:, ns].astype(BF16))
        for hh in range(hpg):
            h = g * hpg + hh
            rows = slice(hh * SSD_HEAD_DIM, (hh + 1) * SSD_HEAD_DIM)
            st_out[0, h] = dec_t[h:h + 1, :] * st[rows, :] + upd[rows, :]
    y_ref[...] = y + jnp.concatenate(ys, axis=1) * expcum_x


def _ssd_sample(u, dt_raw, conv_state, ssm_state, conv_w, conv_b, dt_bias, a_log, d_exp, e_mat, n_new):
    bsz = ssm_state.shape[0]
    keep = CONV_W - 1
    cwx, cwb, cwc = conv_w[:, :D_INNER], conv_w[:, D_INNER:D_INNER + SSD_GROUP_W], conv_w[:, D_INNER + SSD_GROUP_W:]
    cbx, cbb, cbc = conv_b[:, :D_INNER], conv_b[:, D_INNER:D_INNER + SSD_GROUP_W], conv_b[:, D_INNER + SSD_GROUP_W:]
    full = lambda shape: pl.BlockSpec(shape, lambda b: (0,) * len(shape))
    st_spec = pl.BlockSpec((1, SSD_HEADS, SSD_HEAD_DIM, SSD_STATE), lambda b: (b, 0, 0, 0))
    return pl.pallas_call(
        _ssd_sample_kernel,
        grid=(bsz,),
        in_specs=[
            pl.BlockSpec((n_new, D_INNER), lambda b: (b, X_OFF // D_INNER)),
            pl.BlockSpec((n_new, SSD_GROUP_W), lambda b: (b, B_OFF // SSD_GROUP_W)),
            pl.BlockSpec((n_new, SSD_GROUP_W), lambda b: (b, C_OFF // SSD_GROUP_W)),
            pl.BlockSpec((n_new, LANES), lambda b: (b, 0)),
            pl.BlockSpec((1, keep, D_INNER), lambda b: (b, 0, 0)),
            pl.BlockSpec((1, keep, SSD_GROUP_W), lambda b: (b, 0, D_INNER // SSD_GROUP_W)),
            pl.BlockSpec((1, keep, SSD_GROUP_W), lambda b: (b, 0, D_INNER // SSD_GROUP_W + 1)),
            st_spec,
            full((CONV_W, D_INNER)), full((CONV_W, SSD_GROUP_W)), full((CONV_W, SSD_GROUP_W)),
            full((1, D_INNER)), full((1, SSD_GROUP_W)), full((1, SSD_GROUP_W)),
            full((1, LANES)), full((1, LANES)), full((1, D_INNER)), full((LANES, D_INNER)),
        ],
        out_specs=[pl.BlockSpec((n_new, D_INNER), lambda b: (b, 0)), st_spec],
        out_shape=[
            jax.ShapeDtypeStruct((bsz * n_new, D_INNER), F32),
            jax.ShapeDtypeStruct(ssm_state.shape, F32),
        ],
        scratch_shapes=[
            pltpu.VMEM((HALO + n_new, D_INNER), F32),
            pltpu.VMEM((HALO + n_new, SSD_GROUP_W), F32),
            pltpu.VMEM((HALO + n_new, SSD_GROUP_W), F32),
        ],
        compiler_params=_cparams(("parallel",)),
        name="ssd_sample",
    )(u, u, u, dt_raw, conv_state, conv_state, conv_state, ssm_state,
      cwx, cwb, cwc, cbx, cbb, cbc, dt_bias, a_log, d_exp, e_mat)


def _prep_weights(w_in, dt_bias, d_skip):
    q_end = 3 * len(ATTN_GROUPS) * A_WIDTH
    z_end = q_end + D_INNER
    x_end = z_end + D_INNER
    b_end = x_end + SSD_GROUP_W
    c_end = b_end + SSD_GROUP_W
    dt_end = c_end + SSD_HEADS
    w_main = jnp.concatenate(
        [w_in[:, q_end:z_end], w_in[:, dt_end:], w_in[:, z_end:c_end], w_in[:, :q_end]], axis=1).astype(BF16)
    w_dt = jnp.pad(w_in[:, c_end:dt_end], ((0, 0), (0, LANES - SSD_HEADS))).astype(BF16)
    dtb = jnp.pad(dt_bias, (0, LANES - SSD_HEADS)).reshape(1, LANES)
    d_exp = jnp.repeat(d_skip, SSD_HEAD_DIM).reshape(1, D_INNER)
    return w_main, w_dt, dtb, d_exp


def _tail_conv(u, batch, seq):
    u3 = u.reshape(batch, seq, N_REST)
    return u3[:, seq - (CONV_W - 1):, X_OFF:C_OFF + SSD_GROUP_W].astype(F32)


def kernel(x_prompt, x_sample, cache_kv_w128, cache_kv_w512, cache_kv_w2048, state_conv, state_ssm,
           pre_mix_norm, w_in, conv_w, conv_b, dt_bias, a_log, d_skip, ssd_norm, w_attn_out, w_ssd_out,
           w_out, post_mix_norm, pre_ffn_norm, w_up, w_down, post_ffn_norm):
    batch, seq, _ = x_prompt.shape
    dbatch, dseq, _ = x_sample.shape
    assert w_in.shape[0] == 1, "single-layer trunk"
    assert seq % ATTN_TILE == 0 and dseq == HALO

    w_main, w_dt, dtb, d_exp = _prep_weights(w_in[0], dt_bias[0], d_skip[0])
    alog = jnp.pad(a_log[0], (0, LANES - SSD_HEADS)).reshape(1, LANES)
    e_mat = (jnp.arange(LANES)[:, None] == (jnp.arange(D_INNER)[None, :] // SSD_HEAD_DIM)).astype(BF16)
    nw = pre_mix_norm[0].reshape(1, D_MODEL)
    cw, cb = conv_w[0], conv_b[0].reshape(1, -1)
    wa, ws, wo = w_attn_out[0].astype(BF16), w_ssd_out[0].astype(BF16), w_out[0].astype(BF16)
    wu, wd = w_up[0].astype(BF16), w_down[0].astype(BF16)
    sn = ssd_norm[0].reshape(1, D_INNER)
    pmn, pfn, qfn = (post_mix_norm[0].reshape(1, -1), pre_ffn_norm[0].reshape(1, -1),
                     post_ffn_norm[0].reshape(1, -1))

    xp = x_prompt.reshape(batch * seq, D_MODEL)
    xs = x_sample.reshape(dbatch * dseq, D_MODEL)
    ur_p, uq_p, dt_p = _inproj(xp, nw, w_main, w_dt, min(2048, batch * seq), 512, BF16)
    ur_s, uq_s, dt_s = _inproj(xs, nw, w_main, w_dt, min(1024, dbatch * dseq), 512, F32)
    qkv = uq_s.reshape(dbatch, dseq, 3, 3, HEADS, HEAD_DIM)
    q_s = qkv[:, :, :, 0].transpose(0, 3, 2, 1, 4)
    new_t = qkv[:, :, :, 1:3].transpose(0, 4, 5, 2, 3, 1).reshape(dbatch, HEADS, HEAD_DIM, 6 * dseq)
    new_t = jnp.pad(new_t, ((0, 0), (0, 0), (0, 0), (0, LANES - 6 * dseq)))
    caches = [c[0].transpose(0, 2, 3, 4, 1) for c in (cache_kv_w128, cache_kv_w512, cache_kv_w2048)]

    a_p, *kvt_p = _attn_prompt(uq_p, batch, seq)
    y_p, ssm_p, ao, nc0, nc1, nc2 = _ssd_prompt(ur_p, dt_p, cw, cb, dtb, alog, d_exp, e_mat, batch, seq,
                                                q_s, new_t, caches)
    out_p = _mix_ffn(a_p, y_p, ur_p, xp, wa, ws, wo, sn, pmn, wu, wd, pfn, qfn, 256).reshape(batch, seq, D_MODEL)
    kv_p = [t.transpose(0, 4, 1, 2, 3)[None] for t in kvt_p]
    conv_p = _tail_conv(ur_p, batch, seq)[None]

    a_s = ao.transpose(0, 2, 1, 3).reshape(dbatch * dseq, A_WIDTH)
    kv_s = [c.transpose(0, 4, 1, 2, 3)[None] for c in (nc0, nc1, nc2)]
    y_s, ssm_s = _ssd_sample(ur_s, dt_s, state_conv[0], state_ssm[0], cw, cb, dtb, alog, d_exp,
                             e_mat, dseq)
    tm_s = min(256, dbatch * dseq)
    out_s = _mix_ffn(a_s, y_s, ur_s, xs, wa, ws, wo, sn, pmn, wu, wd, pfn, qfn, tm_s).reshape(dbatch, dseq, D_MODEL)
    conv_s = _tail_conv(ur_s, dbatch, dseq)[None]

    return (out_p, out_s, kv_p[0], kv_p[1], kv_p[2], conv_p, ssm_p[None],
            kv_s[0], kv_s[1], kv_s[2], conv_s, ssm_s[None])
```

```python
import functools

import jax
import jax.numpy as jnp
from jax import lax
from jax.experimental import pallas as pl
from jax.experimental.pallas import tpu as pltpu

F32 = jnp.float32
BF16 = jnp.bfloat16

D_MODEL = 1024
ATTN_GROUPS = ((128, 1), (512, 4), (2048, 16))
BAND = 128
HEADS = 8
HEAD_DIM = 64
A_WIDTH = HEADS * HEAD_DIM
ATTN_SCALE = HEAD_DIM ** -0.5
LOG2E = 1.4426950408889634
ATTN_TILE = 2048
BLOCK_UNROLL = 4

D_INNER = 2048
SSD_HEADS = 32
SSD_HEAD_DIM = 64
SSD_GROUPS = 4
SSD_GROUP_W = D_INNER // SSD_GROUPS
SSD_STATE = 128
CONV_W = 4
CHUNK = 128
D_FF = 2816
RMS_EPS = 1e-6
LANES = 128
HALO = 8

Z_OFF = 0
GATE_OFF = 2048
X_OFF = 4096
B_OFF = 6144
C_OFF = 6656
N_REST = 7168
N_QKV = 3 * len(ATTN_GROUPS) * A_WIDTH
N_MAIN = N_REST + N_QKV

VMEM_LIMIT = 56 * 1024 * 1024


def _cparams(sem):
    return pltpu.CompilerParams(dimension_semantics=sem, vmem_limit_bytes=VMEM_LIMIT)


def _rms(x, w):
    return x * lax.rsqrt(jnp.mean(x * x, axis=-1, keepdims=True) + RMS_EPS) * w


def _silu(x):
    h = 0.5 * x
    return h + h * jnp.tanh(h)


def _sigmoid(x):
    return 0.5 + 0.5 * jnp.tanh(0.5 * x)


def _dot(a, b):
    return jnp.dot(a, b, preferred_element_type=F32)


def _dot_nt(a, b):
    return lax.dot_general(a, b, (((1,), (1,)), ((), ())), preferred_element_type=F32)


def _dot_tn(a, b):
    return lax.dot_general(a, b, (((0,), (0,)), ((), ())), preferred_element_type=F32)


def _expand01(a, e):
    hi = a.astype(BF16)
    mid = (a - hi.astype(F32)).astype(BF16)
    return _dot(hi, e) + _dot(mid, e)


def _inproj_order(j, rest_tiles, q_tiles):
    mixed = j < 2 * q_tiles
    is_q = jnp.logical_and(mixed, j % 2 == 1)
    rest_idx = jnp.where(mixed, j // 2, j - q_tiles)
    q_idx = jnp.clip((j - 1) // 2, 0, q_tiles - 1)
    return is_q, rest_idx, q_idx


def _inproj_kernel(x_ref, nw_ref, w_ref, wdt_ref, r_ref, q_ref, dt_ref, h_ref, *, rest_tiles, q_tiles):
    j = pl.program_id(1)
    is_q, _, _ = _inproj_order(j, rest_tiles, q_tiles)

    @pl.when(j == 0)
    def _():
        hb = _rms(x_ref[...], nw_ref[...]).astype(BF16)
        h_ref[...] = hb
        dt_ref[...] = _dot(hb, wdt_ref[...])

    @pl.when(jnp.logical_not(is_q))
    def _():
        r_ref[...] = _dot(h_ref[...], w_ref[...]).astype(r_ref.dtype)

    @pl.when(is_q)
    def _():
        q_ref[...] = _dot(h_ref[...], w_ref[...])


def _inproj(x, nw, w_main, w_dt, tm, tn, rest_dtype):
    t = x.shape[0]
    rest_tiles, q_tiles = N_REST // tn, N_QKV // tn
    assert rest_tiles >= q_tiles
    order = functools.partial(_inproj_order, rest_tiles=rest_tiles, q_tiles=q_tiles)

    def w_tile(i, j):
        is_q, rest_idx, q_idx = order(j)
        return 0, jnp.where(is_q, rest_tiles + q_idx, rest_idx)

    return pl.pallas_call(
        functools.partial(_inproj_kernel, rest_tiles=rest_tiles, q_tiles=q_tiles),
        grid=(t // tm, N_MAIN // tn),
        in_specs=[
            pl.BlockSpec((tm, D_MODEL), lambda i, j: (i, 0)),
            pl.BlockSpec((1, D_MODEL), lambda i, j: (0, 0)),
            pl.BlockSpec((D_MODEL, tn), w_tile),
            pl.BlockSpec((D_MODEL, LANES), lambda i, j: (0, 0)),
        ],
        out_specs=[
            pl.BlockSpec((tm, tn), lambda i, j: (i, order(j)[1])),
            pl.BlockSpec((tm, tn), lambda i, j: (i, order(j)[2])),
            pl.BlockSpec((tm, LANES), lambda i, j: (i, 0)),
        ],
        out_shape=[
            jax.ShapeDtypeStruct((t, N_REST), rest_dtype),
            jax.ShapeDtypeStruct((t, N_QKV), F32),
            jax.ShapeDtypeStruct((t, LANES), F32),
        ],
        scratch_shapes=[pltpu.VMEM((tm, D_MODEL), BF16)],
        compiler_params=_cparams(("parallel", "arbitrary")),
        name="inproj",
    )(x, nw, w_main, w_dt)


def _unroll_for(trips):
    return max(u for u in range(1, BLOCK_UNROLL + 1) if trips % u == 0)


def _attn_prompt_kernel(q0, k0, v0, q1, k1, v1, q2, k2, v2, o_ref, kvo0, kvo1, kvo2,
                        kd0, vt0, kd1, vt1, kd2, vt2, lse_ref, acc_ref, s_buf, p_buf, stat_buf):
    n = pl.program_id(2)
    q_refs, k_refs, v_refs = (q0, q1, q2), (k0, k1, k2), (v0, v1, v2)
    kds, vts = (kd0, kd1, kd2), (vt0, vt1, vt2)
    krow = lax.broadcasted_iota(jnp.int32, (BAND, LANES), 0)
    qcol = lax.broadcasted_iota(jnp.int32, (BAND, LANES), 1)
    upper = krow > qcol
    diag = krow == qcol
    head0 = qcol < HEAD_DIM
    half = BAND // 2

    def group_stages(g):
        dil = ATTN_GROUPS[g][1]
        span = BAND * dil
        nsub = ATTN_TILE // span
        nblk = nsub * dil
        q_ref, k_ref, v_ref, kd, vt = q_refs[g], k_refs[g], v_refs[g], kds[g], vts[g]

        def split(i):
            if dil == 1:
                return i, 0
            j = i // dil
            return j, i - j * dil

        def rows_of(i):
            j, r = split(i)
            if dil == 1:
                return pl.ds(i * BAND if isinstance(i, int) else pl.multiple_of(i * BAND, BAND), BAND)
            return pl.ds(j * span + r, BAND, stride=dil)

        def carry():
            @pl.when(n == 0)
            def _():
                for r in range(dil):
                    kd[r * (nsub + 1)] = jnp.zeros((BAND, LANES), BF16)
                    vt[r * (nsub + 1)] = jnp.zeros((LANES, BAND), BF16)

            @pl.when(n > 0)
            def _():
                for r in range(dil):
                    kd[r * (nsub + 1)] = kd[r * (nsub + 1) + nsub]
                    vt[r * (nsub + 1)] = vt[r * (nsub + 1) + nsub]

        def deinterleave():
            for i in range(nblk):
                j, r = split(i)
                slot = r * (nsub + 1) + j + 1
                rows = rows_of(i)
                kd[slot] = k_ref[rows, :].astype(BF16)
                vt[slot] = v_ref[rows, :].T.astype(BF16)

        def qk_stage(i, sl):
            j, r = split(i)
            slot = r * (nsub + 1) + j
            q = q_ref[rows_of(i), :] * (ATTN_SCALE * LOG2E)
            q2 = jnp.concatenate([jnp.where(head0, q, 0.0), jnp.where(head0, 0.0, q)], axis=0).astype(BF16)
            kk = kd[pl.ds(slot, 2)].reshape(2 * BAND, LANES)
            s_buf[sl] = _dot_nt(kk, q2)

        def softmax_stage(sl, first):
            for hh in range(2):
                sp = s_buf[sl, 0:BAND, hh * BAND:(hh + 1) * BAND]
                sc = s_buf[sl, BAND:2 * BAND, hh * BAND:(hh + 1) * BAND]
                if first:
                    sp = jnp.where(n > 0, sp, -jnp.inf)
                comb = jnp.where(upper, sp, sc)
                dg = jnp.sum(jnp.where(diag, sp, 0.0), axis=0, keepdims=True)
                m = jnp.maximum(jnp.max(comb, axis=0, keepdims=True), dg)
                p = jnp.exp2(comb - m)
                pd = jnp.exp2(dg - m)
                den = jnp.sum(p, axis=0, keepdims=True) + pd
                p_buf[sl, hh, 0:BAND, :] = jnp.where(upper, p, jnp.where(diag, pd, 0.0)).astype(BF16)
                p_buf[sl, hh, BAND:2 * BAND, :] = jnp.where(upper, 0.0, p).astype(BF16)
                stat_buf[sl, 2 * hh:2 * hh + 1, :] = 1.0 / den
                stat_buf[sl, 2 * hh + 1:2 * hh + 2, :] = m + jnp.log2(den)

        def pv_stage(i, sl):
            j, r = split(i)
            slot = r * (nsub + 1) + j
            rows = rows_of(i)
            vprev = vt[slot]
            vcur = vt[slot + 1]
            ots, lts = [], []
            for hh in range(2):
                vsl = slice(hh * half, (hh + 1) * half)
                vtt = jnp.concatenate([vprev[vsl, :], vcur[vsl, :]], axis=1)
                ots.append(_dot(vtt, p_buf[sl, hh]) * stat_buf[sl, 2 * hh:2 * hh + 1, :])
                lts.append(jnp.broadcast_to(stat_buf[sl, 2 * hh + 1:2 * hh + 2, :], (half, BAND)))
            o_blk = jnp.concatenate(ots, axis=0).T
            lse_blk = jnp.concatenate(lts, axis=0).T
            if g == 0:
                lse_ref[rows, :] = lse_blk
                acc_ref[rows, :] = o_blk
            else:
                lse_run = lse_ref[rows, :]
                acc_run = acc_ref[rows, :]
                mx = jnp.maximum(lse_run, lse_blk)
                ea = jnp.exp2(lse_run - mx)
                eb = jnp.exp2(lse_blk - mx)
                tot = ea + eb
                inv = 1.0 / tot
                merged = (ea * inv) * acc_run + (eb * inv) * o_blk
                if g == len(ATTN_GROUPS) - 1:
                    o_ref[rows, :] = merged
                else:
                    lse_ref[rows, :] = mx + jnp.log2(tot)
                    acc_ref[rows, :] = merged

        def qk_pair(pair, par):
            qk_stage(2 * pair, 2 * par)
            qk_stage(2 * pair + 1, 2 * par + 1)

        def softmax_pair(pair, par):
            softmax_stage(2 * par, 2 * pair < dil)
            softmax_stage(2 * par + 1, 2 * pair + 1 < dil)

        def pv_pair(pair, par):
            pv_stage(2 * pair, 2 * par)
            pv_stage(2 * pair + 1, 2 * par + 1)

        return (carry, deinterleave), (qk_pair, softmax_pair, pv_pair), nblk // 2

    groups = [group_stages(g) for g in range(len(ATTN_GROUPS))]
    for (carry, _), _, _ in groups:
        carry()
    for (_, deinterleave), _, _ in groups:
        deinterleave()
    items = [(stages, pair) for _, stages, npairs in groups for pair in range(npairs)]
    for t in range(len(items) + 2):
        for depth in (2, 1, 0):
            if 0 <= t - depth < len(items):
                stages, pair = items[t - depth]
                stages[depth](pair, (t - depth) % 2)

    @pl.when(n == pl.num_programs(2) - 1)
    def _():
        for g, (win, _) in enumerate(ATTN_GROUPS):
            for which, ref in enumerate((k_refs[g], v_refs[g])):
                t = ref[ATTN_TILE - win:, :].T
                for hh in range(2):
                    (kvo0, kvo1, kvo2)[g][0, which, hh] = t[hh * HEAD_DIM:(hh + 1) * HEAD_DIM, :]


def _attn_prompt(u, batch, seq):
    nt = seq // ATTN_TILE
    hp = A_WIDTH // LANES
    qkv_blk = 0

    def spec(g, which):
        col = qkv_blk + (3 * g + which) * hp
        return pl.BlockSpec((ATTN_TILE, LANES), lambda b, h, n, col=col: (b * nt + n, col + h))

    in_specs = [spec(g, w) for g in range(3) for w in range(3)]
    scratch = []
    for _, dil in ATTN_GROUPS:
        nslots = dil * (ATTN_TILE // (BAND * dil) + 1)
        scratch += [pltpu.VMEM((nslots, BAND, LANES), BF16), pltpu.VMEM((nslots, LANES, BAND), BF16)]
    scratch += [pltpu.VMEM((ATTN_TILE, LANES), F32), pltpu.VMEM((ATTN_TILE, LANES), F32)]
    scratch += [pltpu.VMEM((4, 2 * BAND, 2 * BAND), F32), pltpu.VMEM((4, 2, 2 * BAND, BAND), BF16),
                pltpu.VMEM((4, HALO, BAND), F32)]
    return pl.pallas_call(
        _attn_prompt_kernel,
        grid=(batch, hp, nt),
        in_specs=in_specs,
        out_specs=[pl.BlockSpec((ATTN_TILE, LANES), lambda b, h, n: (b * nt + n, h))]
        + [pl.BlockSpec((1, 2, 2, HEAD_DIM, w), lambda b, h, n: (b, 0, h, 0, 0)) for w, _ in ATTN_GROUPS],
        out_shape=[jax.ShapeDtypeStruct((batch * seq, A_WIDTH), F32)]
        + [jax.ShapeDtypeStruct((batch, 2, HEADS, HEAD_DIM, w), F32) for w, _ in ATTN_GROUPS],
        scratch_shapes=scratch,
        compiler_params=_cparams(("parallel", "parallel", "arbitrary")),
        name="attn_prompt",
    )(*([u] * 9))


def _conv_silu(pad_ref, w_ref, b_ref, rows):
    acc = b_ref[...] + pad_ref[HALO - (CONV_W - 1):HALO - (CONV_W - 1) + rows, :] * w_ref[0:1, :]
    for i in range(1, CONV_W):
        off = HALO - (CONV_W - 1) + i
        acc = acc + pad_ref[off:off + rows, :] * w_ref[i:i + 1, :]
    return _silu(acc)


def _conv_silu_chunk(x_ref, tail_ref, w_ref, b_ref, rows, cols):
    x = x_ref[rows, cols].astype(F32)
    ext = jnp.concatenate([tail_ref[:, cols], x], axis=0)
    tail_ref[:, cols] = x[x.shape[0] - HALO:, :]
    acc = b_ref[:, cols] + x * w_ref[CONV_W - 1:CONV_W, cols]
    for k in range(1, CONV_W):
        acc = acc + pltpu.roll(ext, k, 0)[HALO:, :] * w_ref[CONV_W - 1 - k:CONV_W - k, cols]
    return _silu(acc)


def _softplus(x):
    return jnp.maximum(x, 0.0) + jnp.log1p(jnp.exp(-jnp.abs(x)))


def _ssd_prompt_kernel(x_ref, b_ref, c_ref, dt_ref, dtn_ref, cwx, cwb, cwc, cbx, cbb, cbc,
                       dtb_ref, alog_ref, dexp_ref, e_ref, q_ref, new_ref, c0, c1, c2,
                       y_ref, st_ref, ao_ref, o0, o1, o2,
                       xtail, btail, ctail, state, ex_a, ex_b, cum_a, cum_b):
    c = pl.program_id(1)
    nc = pl.num_programs(1)

    @pl.when(c == 0)
    def _():
        xtail[...] = jnp.zeros(xtail.shape, F32)
        btail[...] = jnp.zeros(btail.shape, F32)
        ctail[...] = jnp.zeros(ctail.shape, F32)
        state[...] = jnp.zeros(state.shape, F32)

    ti = lax.broadcasted_iota(jnp.int32, (CHUNK, CHUNK), 0)
    si = lax.broadcasted_iota(jnp.int32, (CHUNK, CHUNK), 1)
    causal = si <= ti
    lane = lax.broadcasted_iota(jnp.int32, (CHUNK, LANES), 1)
    head0 = lane < SSD_HEAD_DIM
    hpg = SSD_HEADS // SSD_GROUPS

    def prepare(dt_blk, ex_ref, cum_ref):
        dt = _softplus(dt_blk + dtb_ref[...])
        dta = dt * -jnp.exp(alog_ref[...])
        tril = jnp.where(causal, 1.0, 0.0).astype(F32)
        cum = jnp.dot(tril, dta, preferred_element_type=F32, precision=lax.Precision.HIGHEST)
        cum2 = cum * LOG2E
        cum_ref[0] = cum2
        cum_ref[1] = cum2.T
        expcum = jnp.exp2(cum2)
        to_end = jnp.exp2(cum2[CHUNK - 1:CHUNK, :] - cum2) * dt
        ex_ref[...] = _expand01(jnp.concatenate([expcum, to_end, dt], axis=0), e_ref[...])

    @pl.when(c == 0)
    def _():
        prepare(dt_ref[0:CHUNK, :], ex_a, cum_a)

    def chunk(rows, ex_s, cum_s, hooks):
        cum2 = cum_s[0]
        cum2_t = cum_s[1]
        dec_x = ex_s[CHUNK - 1:CHUNK, :]
        for g in range(SSD_GROUPS):
            for hook in hooks[g]:
                hook()
            gs = slice(g * SSD_GROUP_W, (g + 1) * SSD_GROUP_W)
            ns = slice(g * SSD_STATE, (g + 1) * SSD_STATE)
            bg = _conv_silu_chunk(b_ref, btail, cwb, cbb, rows, ns).astype(BF16)
            cg = _conv_silu_chunk(c_ref, ctail, cwc, cbc, rows, ns).astype(BF16)
            xg = _conv_silu_chunk(x_ref, xtail, cwx, cbx, rows, gs)
            cb = _dot_nt(cg, bg)
            xdt = xg * ex_s[2 * CHUNK:3 * CHUNK, gs]
            ys = []
            for pr in range(hpg // 2):
                ws = []
                for hh in range(2):
                    h = g * hpg + 2 * pr + hh
                    colb = jnp.broadcast_to(cum2[:, h:h + 1], (CHUNK, CHUNK))
                    rowb = jnp.broadcast_to(cum2_t[h:h + 1, :], (CHUNK, CHUNK))
                    decay = jnp.exp2(jnp.where(causal, colb - rowb, -jnp.inf))
                    ws.append((cb * decay).astype(BF16))
                w2 = jnp.concatenate(ws, axis=1)
                xp = xdt[:, pr * LANES:(pr + 1) * LANES]
                x2 = jnp.concatenate([jnp.where(head0, xp, 0.0), jnp.where(head0, 0.0, xp)], axis=0).astype(BF16)
                ys.append(_dot(w2, x2))
            st = state[g]
            y_state = _dot(cg, st.astype(BF16))
            y_ref[rows, gs] = jnp.concatenate(ys, axis=1) + y_state * ex_s[0:CHUNK, gs] + dexp_ref[:, gs] * xg
            xs = (xg * ex_s[CHUNK:2 * CHUNK, gs]).astype(BF16)
            state[g] = dec_x[:, gs] * st + _dot_tn(bg, xs)

    sample_pieces = [p for hx in range(q_ref.shape[1])
                     for p in _attn_sample_pieces(hx, q_ref, new_ref, (c0, c1, c2), ao_ref, (o0, o1, o2))]
    nslots = 2 * SSD_GROUPS
    hooks = [[] for _ in range(nslots)]
    for idx, piece in enumerate(sample_pieces):
        hooks[idx * nslots // len(sample_pieces)].append(piece)
    hooks[1].append(lambda: prepare(dt_ref[CHUNK:2 * CHUNK, :], ex_b, cum_b))
    hooks[SSD_GROUPS + 1].append(lambda: prepare(dtn_ref[...], ex_a, cum_a))
    chunk(slice(0, CHUNK), ex_a, cum_a, hooks[:SSD_GROUPS])
    chunk(slice(CHUNK, 2 * CHUNK), ex_b, cum_b, hooks[SSD_GROUPS:])

    @pl.when(c == nc - 1)
    def _():
        for g in range(SSD_GROUPS):
            st_ref[0, g * hpg:(g + 1) * hpg, :, :] = state[g].T.reshape(hpg, SSD_HEAD_DIM, SSD_STATE)


def _ssd_prompt(u, dt_raw, conv_w, conv_b, dt_bias, a_log, d_exp, e_mat, batch, seq, q_s, new_t, caches):
    nc = seq // CHUNK
    cwx, cwb, cwc = conv_w[:, :D_INNER], conv_w[:, D_INNER:D_INNER + SSD_GROUP_W], conv_w[:, D_INNER + SSD_GROUP_W:]
    cbx, cbb, cbc = conv_b[:, :D_INNER], conv_b[:, D_INNER:D_INNER + SSD_GROUP_W], conv_b[:, D_INNER + SSD_GROUP_W:]
    full = lambda shape: pl.BlockSpec(shape, lambda b, c: (0,) * len(shape))
    dbatch, _, _, n_new, _ = q_s.shape
    assert nc % 2 == 0
    nchunks, nc, rows = nc, nc // 2, 2 * CHUNK
    steps = batch * nc
    assert (dbatch * HEADS) % steps == 0, "sample heads must spread evenly over the prompt's SSD steps"
    hps = dbatch * HEADS // steps
    assert HEADS % hps == 0
    spb = HEADS // hps
    sb = lambda b, c: (b * nc + c) // spb
    hb = lambda b, c: (b * nc + c) % spb
    q_spec = pl.BlockSpec((1, hps, len(ATTN_GROUPS), n_new, HEAD_DIM), lambda b, c: (sb(b, c), hb(b, c), 0, 0, 0))
    new_spec = pl.BlockSpec((1, hps, HEAD_DIM, LANES), lambda b, c: (sb(b, c), hb(b, c), 0, 0))
    ao_spec = pl.BlockSpec((1, hps, n_new, HEAD_DIM), lambda b, c: (sb(b, c), hb(b, c), 0, 0))
    cache_specs = [pl.BlockSpec((1, 2, hps, HEAD_DIM, w), lambda b, c: (sb(b, c), 0, hb(b, c), 0, 0))
                   for w, _ in ATTN_GROUPS]
    return pl.pallas_call(
        _ssd_prompt_kernel,
        grid=(batch, nc),
        in_specs=[
            pl.BlockSpec((rows, D_INNER), lambda b, c: (b * nc + c, X_OFF // D_INNER)),
            pl.BlockSpec((rows, SSD_GROUP_W), lambda b, c: (b * nc + c, B_OFF // SSD_GROUP_W)),
            pl.BlockSpec((rows, SSD_GROUP_W), lambda b, c: (b * nc + c, C_OFF // SSD_GROUP_W)),
            pl.BlockSpec((rows, LANES), lambda b, c: (b * nc + c, 0)),
            pl.BlockSpec((CHUNK, LANES), lambda b, c: (b * nchunks + jnp.minimum(2 * c + 2, nchunks - 1), 0)),
            full((CONV_W, D_INNER)), full((CONV_W, SSD_GROUP_W)), full((CONV_W, SSD_GROUP_W)),
            full((1, D_INNER)), full((1, SSD_GROUP_W)), full((1, SSD_GROUP_W)),
            full((1, LANES)), full((1, LANES)), full((1, D_INNER)), full((LANES, D_INNER)),
            q_spec, new_spec,
        ] + cache_specs,
        out_specs=[
            pl.BlockSpec((rows, D_INNER), lambda b, c: (b * nc + c, 0)),
            pl.BlockSpec((1, SSD_HEADS, SSD_HEAD_DIM, SSD_STATE), lambda b, c: (b, 0, 0, 0)),
            ao_spec,
        ] + cache_specs,
        out_shape=[
            jax.ShapeDtypeStruct((batch * seq, D_INNER), F32),
            jax.ShapeDtypeStruct((batch, SSD_HEADS, SSD_HEAD_DIM, SSD_STATE), F32),
            jax.ShapeDtypeStruct((dbatch, HEADS, n_new, HEAD_DIM), F32),
        ] + [jax.ShapeDtypeStruct(cc.shape, F32) for cc in caches],
        scratch_shapes=[
            pltpu.VMEM((HALO, D_INNER), F32),
            pltpu.VMEM((HALO, SSD_GROUP_W), F32),
            pltpu.VMEM((HALO, SSD_GROUP_W), F32),
            pltpu.VMEM((SSD_GROUPS, SSD_STATE, SSD_GROUP_W), F32),
            pltpu.VMEM((3 * CHUNK, D_INNER), F32), pltpu.VMEM((3 * CHUNK, D_INNER), F32),
            pltpu.VMEM((2, CHUNK, LANES), F32), pltpu.VMEM((2, CHUNK, LANES), F32),
        ],
        compiler_params=_cparams(("parallel", "arbitrary")),
        name="ssd_prompt",
    )(u, u, u, dt_raw, dt_raw, cwx, cwb, cwc, cbx, cbb, cbc, dt_bias, a_log, d_exp, e_mat, q_s, new_t, *caches)


def _mix_ffn_kernel(a_ref, y_ref, z_ref, ga_ref, gb_ref, x_ref, wa_ref, ws_ref, wo_ref,
                    sn_ref, pn_ref, wu_ref, wd_ref, n1_ref, n2_ref, o_ref):
    a_proj = _dot(a_ref[...].astype(BF16), wa_ref[...])
    g = y_ref[...] * _silu(z_ref[...].astype(F32))
    parts = []
    for i in range(SSD_GROUPS):
        gi = g[:, i * SSD_GROUP_W:(i + 1) * SSD_GROUP_W]
        parts.append(gi * lax.rsqrt(jnp.mean(gi * gi, axis=-1, keepdims=True) + RMS_EPS))
    yn = (jnp.concatenate(parts, axis=1) * sn_ref[...]).astype(BF16)
    b_proj = _dot(yn, ws_ref[...])
    mixed_in = _sigmoid(ga_ref[...].astype(F32)) * a_proj + _sigmoid(gb_ref[...].astype(F32)) * b_proj
    mixed = _dot(mixed_in.astype(BF16), wo_ref[...])
    x1 = x_ref[...] + _rms(mixed, pn_ref[...])
    h = _rms(x1, n1_ref[...]).astype(BF16)
    gu = _dot(h, wu_ref[...])
    act = (_silu(gu[:, :D_FF]) * gu[:, D_FF:]).astype(BF16)
    f = _dot(act, wd_ref[...])
    o_ref[...] = x1 + _rms(f, n2_ref[...])


def _mix_ffn(a_out, y, u, x, w_attn_out, w_ssd_out, w_out, ssd_norm, post_mix_norm,
             w_up, w_down, pre_norm, post_norm, tm):
    t = x.shape[0]
    const = lambda shape: pl.BlockSpec(shape, lambda i: (0, 0), pipeline_mode=pl.Buffered(1))
    return pl.pallas_call(
        _mix_ffn_kernel,
        grid=(t // tm,),
        in_specs=[
            pl.BlockSpec((tm, A_WIDTH), lambda i: (i, 0)),
            pl.BlockSpec((tm, D_INNER), lambda i: (i, 0)),
            pl.BlockSpec((tm, D_INNER), lambda i: (i, Z_OFF // D_INNER)),
            pl.BlockSpec((tm, D_MODEL), lambda i: (i, GATE_OFF // D_MODEL)),
            pl.BlockSpec((tm, D_MODEL), lambda i: (i, GATE_OFF // D_MODEL + 1)),
            pl.BlockSpec((tm, D_MODEL), lambda i: (i, 0)),
            const((A_WIDTH, D_MODEL)), const((D_INNER, D_MODEL)), const((D_MODEL, D_MODEL)),
            const((1, D_INNER)), const((1, D_MODEL)),
            const((D_MODEL, 2 * D_FF)), const((D_FF, D_MODEL)),
            const((1, D_MODEL)), const((1, D_MODEL)),
        ],
        out_specs=pl.BlockSpec((tm, D_MODEL), lambda i: (i, 0)),
        out_shape=jax.ShapeDtypeStruct((t, D_MODEL), F32),
        compiler_params=_cparams(("parallel",)),
        name="mix_ffn",
    )(a_out, y, u, u, u, x, w_attn_out, w_ssd_out, w_out, ssd_norm, post_mix_norm,
      w_up, w_down, pre_norm, post_norm)


def _attn_sample_pieces(hx, q_ref, new_ref, caches, ao_ref, outs):
    n_new = q_ref.shape[3]
    run = {}

    def piece(g):
        win, dil = ATTN_GROUPS[g]
        new_t = new_ref[0, hx]
        q = (q_ref[0, hx, g] * ATTN_SCALE).astype(BF16)
        k_t = caches[g][0, 0, hx]
        v_t = caches[g][0, 1, hx]
        knt = pltpu.roll(new_t, LANES - n_new - (2 * g) * n_new, 1)
        vnt = pltpu.roll(new_t, LANES - n_new - (2 * g + 1) * n_new, 1)
        qi = lax.broadcasted_iota(jnp.int32, (n_new, win), 0)
        ci = lax.broadcasted_iota(jnp.int32, (n_new, win), 1)
        ok_c = jnp.logical_and(ci >= qi, ((ci - qi) & (dil - 1)) == 0)
        qn = lax.broadcasted_iota(jnp.int32, (n_new, LANES), 0)
        nn = lax.broadcasted_iota(jnp.int32, (n_new, LANES), 1) - (LANES - n_new)
        ok_n = jnp.logical_and(jnp.logical_and(nn >= 0, nn <= qn), ((qn - nn) & (dil - 1)) == 0)
        s_c = jnp.where(ok_c, _dot(q, k_t.astype(BF16)), -jnp.inf)
        s_n = jnp.where(ok_n, _dot(q, knt.astype(BF16)), -jnp.inf)
        m = jnp.maximum(jnp.max(s_c, axis=1, keepdims=True), jnp.max(s_n, axis=1, keepdims=True))
        p_c = jnp.exp(s_c - m)
        p_n = jnp.exp(s_n - m)
        den = jnp.sum(p_c, axis=1, keepdims=True) + jnp.sum(p_n, axis=1, keepdims=True)
        o = _dot_nt(p_c.astype(BF16), v_t.astype(BF16)) + _dot_nt(p_n.astype(BF16), vnt.astype(BF16))
        if not run:
            run.update(m=m, l=den, o=o)
        else:
            m_new = jnp.maximum(run["m"], m)
            ea = jnp.exp(run["m"] - m_new)
            eb = jnp.exp(m - m_new)
            run.update(m=m_new, l=ea * run["l"] + eb * den, o=ea * run["o"] + eb * o)
        if g == len(ATTN_GROUPS) - 1:
            ao_ref[0, hx] = run["o"] * (1.0 / run["l"])
        lane = lax.broadcasted_iota(jnp.int32, (HEAD_DIM, LANES), 1)
        keep = lane < LANES - n_new
        for which, (src, new) in enumerate(((k_t, knt), (v_t, vnt))):
            rolled = pltpu.roll(src, win - n_new, 1)
            if win > LANES:
                outs[g][0, which, hx, :, 0:win - LANES] = rolled[:, 0:win - LANES]
            outs[g][0, which, hx, :, win - LANES:win] = jnp.where(keep, rolled[:, win - LANES:win], new)

    return [functools.partial(piece, g) for g in range(len(ATTN_GROUPS))]


def _ssd_sample_kernel(x_ref, b_ref, c_ref, dt_ref, sx_ref, sb_ref, sc_ref, st_in,
                       cwx, cwb, cwc, cbx, cbb, cbc, dtb_ref, alog_ref, dexp_ref, e_ref,
                       y_ref, st_out, xpad, bpad, cpad):
    n = x_ref.shape[0]
    keep = CONV_W - 1
    for pad, new, old in ((xpad, x_ref, sx_ref), (bpad, b_ref, sb_ref), (cpad, c_ref, sc_ref)):
        pad[0:HALO, :] = jnp.zeros((HALO, pad.shape[1]), F32)
        pad[HALO - keep:HALO, :] = old[0]
        pad[HALO:HALO + n, :] = new[...]
    xc = _conv_silu(xpad, cwx, cbx, n)
    bm = _conv_silu(bpad, cwb, cbb, n)
    cm = _conv_silu(cpad, cwc, cbc, n)

    dt = _softplus(dt_ref[...] + dtb_ref[...])
    a = -jnp.exp(alog_ref[...])
    dta = dt * a
    ti = lax.broadcasted_iota(jnp.int32, (n, n), 0)
    si = lax.broadcasted_iota(jnp.int32, (n, n), 1)
    tril = jnp.where(si <= ti, 1.0, 0.0).astype(F32)
    cum = jnp.dot(tril, dta, preferred_element_type=F32, precision=lax.Precision.HIGHEST)
    cum_last = cum[n - 1:n, :]
    expcum = jnp.exp(cum)
    to_end = jnp.exp(cum_last - cum) * dt

    hpg = SSD_HEADS // SSD_GROUPS
    lane = lax.broadcasted_iota(jnp.int32, (n, LANES), 1)
    trow = lax.broadcasted_iota(jnp.int32, (n, LANES), 0)
    cbs = [_dot_nt(cm[:, g * SSD_STATE:(g + 1) * SSD_STATE].astype(BF16),
                   bm[:, g * SSD_STATE:(g + 1) * SSD_STATE].astype(BF16)) for g in range(SSD_GROUPS)]
    coefs = []
    for s in range(n):
        cbh = jnp.zeros((n, LANES), F32)
        for g in range(SSD_GROUPS):
            in_g = jnp.logical_and(lane >= g * hpg, lane < (g + 1) * hpg)
            cbh = jnp.where(in_g, jnp.broadcast_to(cbs[g][:, s:s + 1], (n, LANES)), cbh)
        decay = jnp.exp(jnp.where(trow >= s, cum - cum[s:s + 1, :], -jnp.inf))
        coefs.append(cbh * decay * dt[s:s + 1, :])
    ex = _expand01(jnp.concatenate(coefs + [expcum, to_end], axis=0), e_ref[...])
    y = dexp_ref[...] * xc
    for s in range(n):
        y = y + ex[s * n:(s + 1) * n, :] * xc[s:s + 1, :]
    expcum_x = ex[n * n:n * n + n, :]
    to_end_x = ex[n * n + n:n * n + 2 * n, :]
    xs = xc * to_end_x
    dec_t = jnp.broadcast_to(expcum[n - 1:n, :], (LANES, LANES)).T
    ys = []
    for g in range(SSD_GROUPS):
        gs = slice(g * SSD_GROUP_W, (g + 1) * SSD_GROUP_W)
        ns = slice(g * SSD_STATE, (g + 1) * SSD_STATE)
        st = st_in[0, g * hpg:(g + 1) * hpg].reshape(SSD_GROUP_W, SSD_STATE)
        ys.append(_dot_nt(cm[:, ns].astype(BF16), st.astype(BF16)))
        upd = _dot_tn(xs[:, gs].astype(BF16), bm[:, ns].astype(BF16))
        for hh in range(hpg):
            h = g * hpg + hh
            rows = slice(hh * SSD_HEAD_DIM, (hh + 1) * SSD_HEAD_DIM)
            st_out[0, h] = dec_t[h:h + 1, :] * st[rows, :] + upd[rows, :]
    y_ref[...] = y + jnp.concatenate(ys, axis=1) * expcum_x


def _ssd_sample(u, dt_raw, conv_state, ssm_state, conv_w, conv_b, dt_bias, a_log, d_exp, e_mat, n_new):
    bsz = ssm_state.shape[0]
    keep = CONV_W - 1
    cwx, cwb, cwc = conv_w[:, :D_INNER], conv_w[:, D_INNER:D_INNER + SSD_GROUP_W], conv_w[:, D_INNER + SSD_GROUP_W:]
    cbx, cbb, cbc = conv_b[:, :D_INNER], conv_b[:, D_INNER:D_INNER + SSD_GROUP_W], conv_b[:, D_INNER + SSD_GROUP_W:]
    full = lambda shape: pl.BlockSpec(shape, lambda b: (0,) * len(shape))
    st_spec = pl.BlockSpec((1, SSD_HEADS, SSD_HEAD_DIM, SSD_STATE), lambda b: (b, 0, 0, 0))
    return pl.pallas_call(
        _ssd_sample_kernel,
        grid=(bsz,),
        in_specs=[
            pl.BlockSpec((n_new, D_INNER), lambda b: (b, X_OFF // D_INNER)),
            pl.BlockSpec((n_new, SSD_GROUP_W), lambda b: (b, B_OFF // SSD_GROUP_W)),
            pl.BlockSpec((n_new, SSD_GROUP_W), lambda b: (b, C_OFF // SSD_GROUP_W)),
            pl.BlockSpec((n_new, LANES), lambda b: (b, 0)),
            pl.BlockSpec((1, keep, D_INNER), lambda b: (b, 0, 0)),
            pl.BlockSpec((1, keep, SSD_GROUP_W), lambda b: (b, 0, D_INNER // SSD_GROUP_W)),
            pl.BlockSpec((1, keep, SSD_GROUP_W), lambda b: (b, 0, D_INNER // SSD_GROUP_W + 1)),
            st_spec,
            full((CONV_W, D_INNER)), full((CONV_W, SSD_GROUP_W)), full((CONV_W, SSD_GROUP_W)),
            full((1, D_INNER)), full((1, SSD_GROUP_W)), full((1, SSD_GROUP_W)),
            full((1, LANES)), full((1, LANES)), full((1, D_INNER)), full((LANES, D_INNER)),
        ],
        out_specs=[pl.BlockSpec((n_new, D_INNER), lambda b: (b, 0)), st_spec],
        out_shape=[
            jax.ShapeDtypeStruct((bsz * n_new, D_INNER), F32),
            jax.ShapeDtypeStruct(ssm_state.shape, F32),
        ],
        scratch_shapes=[
            pltpu.VMEM((HALO + n_new, D_INNER), F32),
            pltpu.VMEM((HALO + n_new, SSD_GROUP_W), F32),
            pltpu.VMEM((HALO + n_new, SSD_GROUP_W), F32),
        ],
        compiler_params=_cparams(("parallel",)),
        name="ssd_sample",
    )(u, u, u, dt_raw, conv_state, conv_state, conv_state, ssm_state,
      cwx, cwb, cwc, cbx, cbb, cbc, dt_bias, a_log, d_exp, e_mat)


def _prep_weights(w_in, dt_bias, d_skip):
    q_end = 3 * len(ATTN_GROUPS) * A_WIDTH
    z_end = q_end + D_INNER
    x_end = z_end + D_INNER
    b_end = x_end + SSD_GROUP_W
    c_end = b_end + SSD_GROUP_W
    dt_end = c_end + SSD_HEADS
    w_main = jnp.concatenate(
        [w_in[:, q_end:z_end], w_in[:, dt_end:], w_in[:, z_end:c_end], w_in[:, :q_end]], axis=1).astype(BF16)
    w_dt = jnp.pad(w_in[:, c_end:dt_end], ((0, 0), (0, LANES - SSD_HEADS))).astype(BF16)
    dtb = jnp.pad(dt_bias, (0, LANES - SSD_HEADS)).reshape(1, LANES)
    d_exp = jnp.repeat(d_skip, SSD_HEAD_DIM).reshape(1, D_INNER)
    return w_main, w_dt, dtb, d_exp


def _tail_conv(u, batch, seq):
    u3 = u.reshape(batch, seq, N_REST)
    return u3[:, seq - (CONV_W - 1):, X_OFF:C_OFF + SSD_GROUP_W].astype(F32)


def kernel(x_prompt, x_sample, cache_kv_w128, cache_kv_w512, cache_kv_w2048, state_conv, state_ssm,
           pre_mix_norm, w_in, conv_w, conv_b, dt_bias, a_log, d_skip, ssd_norm, w_attn_out, w_ssd_out,
           w_out, post_mix_norm, pre_ffn_norm, w_up, w_down, post_ffn_norm):
    batch, seq, _ = x_prompt.shape
    dbatch, dseq, _ = x_sample.shape
    assert w_in.shape[0] == 1, "single-layer trunk"
    assert seq % ATTN_TILE == 0 and dseq == HALO

    w_main, w_dt, dtb, d_exp = _prep_weights(w_in[0], dt_bias[0], d_skip[0])
    alog = jnp.pad(a_log[0], (0, LANES - SSD_HEADS)).reshape(1, LANES)
    e_mat = (jnp.arange(LANES)[:, None] == (jnp.arange(D_INNER)[None, :] // SSD_HEAD_DIM)).astype(BF16)
    nw = pre_mix_norm[0].reshape(1, D_MODEL)
    cw, cb = conv_w[0], conv_b[0].reshape(1, -1)
    wa, ws, wo = w_attn_out[0].astype(BF16), w_ssd_out[0].astype(BF16), w_out[0].astype(BF16)
    wu, wd = w_up[0].astype(BF16), w_down[0].astype(BF16)
    sn = ssd_norm[0].reshape(1, D_INNER)
    pmn, pfn, qfn = (post_mix_norm[0].reshape(1, -1), pre_ffn_norm[0].reshape(1, -1),
                     post_ffn_norm[0].reshape(1, -1))

    xp = x_prompt.reshape(batch * seq, D_MODEL)
    xs = x_sample.reshape(dbatch * dseq, D_MODEL)
    ur_p, uq_p, dt_p = _inproj(xp, nw, w_main, w_dt, min(2048, batch * seq), 512, BF16)
    ur_s, uq_s, dt_s = _inproj(xs, nw, w_main, w_dt, min(1024, dbatch * dseq), 512, F32)
    qkv = uq_s.reshape(dbatch, dseq, 3, 3, HEADS, HEAD_DIM)
    q_s = qkv[:, :, :, 0].transpose(0, 3, 2, 1, 4)
    new_t = qkv[:, :, :, 1:3].transpose(0, 4, 5, 2, 3, 1).reshape(dbatch, HEADS, HEAD_DIM, 6 * dseq)
    new_t = jnp.pad(new_t, ((0, 0), (0, 0), (0, 0), (0, LANES - 6 * dseq)))
    caches = [c[0].transpose(0, 2, 3, 4, 1) for c in (cache_kv_w128, cache_kv_w512, cache_kv_w2048)]

    a_p, *kvt_p = _attn_prompt(uq_p, batch, seq)
    y_p, ssm_p, ao, nc0, nc1, nc2 = _ssd_prompt(ur_p, dt_p, cw, cb, dtb, alog, d_exp, e_mat, batch, seq,
                                                q_s, new_t, caches)
    out_p = _mix_ffn(a_p, y_p, ur_p, xp, wa, ws, wo, sn, pmn, wu, wd, pfn, qfn, 256).reshape(batch, seq, D_MODEL)
    kv_p = [t.transpose(0, 4, 1, 2, 3)[None] for t in kvt_p]
    conv_p = _tail_conv(ur_p, batch, seq)[None]

    a_s = ao.transpose(0, 2, 1, 3).reshape(dbatch * dseq, A_WIDTH)
    kv_s = [c.transpose(0, 4, 1, 2, 3)[None] for c in (nc0, nc1, nc2)]
    y_s, ssm_s = _ssd_sample(ur_s, dt_s, state_conv[0], state_ssm[0], cw, cb, dtb, alog, d_exp,
                             e_mat, dseq)
    tm_s = min(256, dbatch * dseq)
    out_s = _mix_ffn(a_s, y_s, ur_s, xs, wa, ws, wo, sn, pmn, wu, wd, pfn, qfn, tm_s).reshape(dbatch, dseq, D_MODEL)
    conv_s = _tail_conv(ur_s, dbatch, dseq)[None]

    return (out_p, out_s, kv_p[0], kv_p[1], kv_p[2], conv_p, ssm_p[None],
            kv_s[0], kv_s[1], kv_s[2], conv_s, ssm_s[None])
```

```python
import functools

import jax
import jax.numpy as jnp
from jax import lax
from jax.experimental import pallas as pl
from jax.experimental.pallas import tpu as pltpu

F32 = jnp.float32
BF16 = jnp.bfloat16

D_MODEL = 1024
ATTN_GROUPS = ((128, 1), (512, 4), (2048, 16))
BAND = 128
HEADS = 8
HEAD_DIM = 64
A_WIDTH = HEADS * HEAD_DIM
ATTN_SCALE = HEAD_DIM ** -0.5
LOG2E = 1.4426950408889634
ATTN_TILE = 2048

D_INNER = 2048
SSD_HEADS = 32
SSD_HEAD_DIM = 64
SSD_GROUPS = 4
SSD_GROUP_W = D_INNER // SSD_GROUPS
SSD_STATE = 128
CONV_W = 4
CHUNK = 128
D_FF = 2816
RMS_EPS = 1e-6
LANES = 128
HALO = 8

Z_OFF = 0
GATE_OFF = Z_OFF + D_INNER
X_OFF = GATE_OFF + 2 * D_MODEL
B_OFF = X_OFF + D_INNER
C_OFF = B_OFF + SSD_GROUP_W
N_REST = C_OFF + SSD_GROUP_W
N_QKV = 3 * len(ATTN_GROUPS) * A_WIDTH
N_MAIN = N_REST + N_QKV

V7X_VMEM_BYTES = 64 * 1024 * 1024
VMEM_LIMIT = V7X_VMEM_BYTES - 8 * 1024 * 1024
INPROJ_ROWS = 2048
INPROJ_COLS = 512
MIX_FFN_ROWS = 256


def _cparams(sem):
    return pltpu.CompilerParams(dimension_semantics=sem, vmem_limit_bytes=VMEM_LIMIT)


def _rms(x, w):
    return x * lax.rsqrt(jnp.mean(x * x, axis=-1, keepdims=True) + RMS_EPS) * w


def _silu(x):
    h = 0.5 * x
    return h + h * jnp.tanh(h)


def _sigmoid(x):
    return 0.5 + 0.5 * jnp.tanh(0.5 * x)


def _dot(a, b):
    return jnp.dot(a, b, preferred_element_type=F32)


def _dot_nt(a, b):
    return lax.dot_general(a, b, (((1,), (1,)), ((), ())), preferred_element_type=F32)


def _dot_tn(a, b):
    return lax.dot_general(a, b, (((0,), (0,)), ((), ())), preferred_element_type=F32)


def _expand01(a, e):
    hi = a.astype(BF16)
    mid = (a - hi.astype(F32)).astype(BF16)
    return _dot(hi, e) + _dot(mid, e)


def _inproj_order(j, rest_tiles, q_tiles):
    mixed = j < 2 * q_tiles
    is_q = jnp.logical_and(mixed, j % 2 == 1)
    rest_idx = jnp.where(mixed, j // 2, j - q_tiles)
    q_idx = jnp.clip((j - 1) // 2, 0, q_tiles - 1)
    return is_q, rest_idx, q_idx


def _inproj_kernel(x_ref, nw_ref, w_ref, wdt_ref, r_ref, q_ref, dt_ref, h_ref, *, rest_tiles, q_tiles):
    j = pl.program_id(1)
    is_q, _, _ = _inproj_order(j, rest_tiles, q_tiles)

    @pl.when(j == 0)
    def _():
        hb = _rms(x_ref[...], nw_ref[...]).astype(BF16)
        h_ref[...] = hb
        dt_ref[...] = _dot(hb, wdt_ref[...])

    @pl.when(jnp.logical_not(is_q))
    def _():
        r_ref[...] = _dot(h_ref[...], w_ref[...]).astype(r_ref.dtype)

    @pl.when(is_q)
    def _():
        q_ref[...] = _dot(h_ref[...], w_ref[...])


def _inproj(x, nw, w_main, w_dt, tm, tn, rest_dtype):
    t = x.shape[0]
    rest_tiles, q_tiles = N_REST // tn, N_QKV // tn
    assert rest_tiles >= q_tiles
    order = functools.partial(_inproj_order, rest_tiles=rest_tiles, q_tiles=q_tiles)

    def w_tile(i, j):
        is_q, rest_idx, q_idx = order(j)
        return 0, jnp.where(is_q, rest_tiles + q_idx, rest_idx)

    return pl.pallas_call(
        functools.partial(_inproj_kernel, rest_tiles=rest_tiles, q_tiles=q_tiles),
        grid=(t // tm, N_MAIN // tn),
        in_specs=[
            pl.BlockSpec((tm, D_MODEL), lambda i, j: (i, 0)),
            pl.BlockSpec((1, D_MODEL), lambda i, j: (0, 0)),
            pl.BlockSpec((D_MODEL, tn), w_tile),
            pl.BlockSpec((D_MODEL, LANES), lambda i, j: (0, 0)),
        ],
        out_specs=[
            pl.BlockSpec((tm, tn), lambda i, j: (i, order(j)[1])),
            pl.BlockSpec((tm, tn), lambda i, j: (i, order(j)[2])),
            pl.BlockSpec((tm, LANES), lambda i, j: (i, 0)),
        ],
        out_shape=[
            jax.ShapeDtypeStruct((t, N_REST), rest_dtype),
            jax.ShapeDtypeStruct((t, N_QKV), F32),
            jax.ShapeDtypeStruct((t, LANES), F32),
        ],
        scratch_shapes=[pltpu.VMEM((tm, D_MODEL), BF16)],
        compiler_params=_cparams(("parallel", "arbitrary")),
        name="inproj",
    )(x, nw, w_main, w_dt)


def _attn_prompt_kernel(q0, k0, v0, q1, k1, v1, q2, k2, v2, o_ref, kvo0, kvo1, kvo2,
                        kd0, vt0, kd1, vt1, kd2, vt2, lse_ref, acc_ref, s_buf, p_buf, stat_buf):
    n = pl.program_id(2)
    q_refs, k_refs, v_refs = (q0, q1, q2), (k0, k1, k2), (v0, v1, v2)
    kds, vts = (kd0, kd1, kd2), (vt0, vt1, vt2)
    krow = lax.broadcasted_iota(jnp.int32, (BAND, LANES), 0)
    qcol = lax.broadcasted_iota(jnp.int32, (BAND, LANES), 1)
    upper = krow > qcol
    diag = krow == qcol
    head0 = qcol < HEAD_DIM
    half = BAND // 2

    def group_stages(g):
        dil = ATTN_GROUPS[g][1]
        span = BAND * dil
        nsub = ATTN_TILE // span
        nblk = nsub * dil
        q_ref, k_ref, v_ref, kd, vt = q_refs[g], k_refs[g], v_refs[g], kds[g], vts[g]

        def split(i):
            if dil == 1:
                return i, 0
            j = i // dil
            return j, i - j * dil

        def rows_of(i):
            j, r = split(i)
            if dil == 1:
                return pl.ds(i * BAND if isinstance(i, int) else pl.multiple_of(i * BAND, BAND), BAND)
            return pl.ds(j * span + r, BAND, stride=dil)

        def carry():
            @pl.when(n == 0)
            def _():
                for r in range(dil):
                    kd[r * (nsub + 1)] = jnp.zeros((BAND, LANES), BF16)
                    vt[r * (nsub + 1)] = jnp.zeros((LANES, BAND), BF16)

            @pl.when(n > 0)
            def _():
                for r in range(dil):
                    kd[r * (nsub + 1)] = kd[r * (nsub + 1) + nsub]
                    vt[r * (nsub + 1)] = vt[r * (nsub + 1) + nsub]

        def deinterleave():
            for i in range(nblk):
                j, r = split(i)
                slot = r * (nsub + 1) + j + 1
                rows = rows_of(i)
                kd[slot] = k_ref[rows, :].astype(BF16)
                vt[slot] = v_ref[rows, :].T.astype(BF16)

        def qk_stage(i, sl):
            j, r = split(i)
            slot = r * (nsub + 1) + j
            q = q_ref[rows_of(i), :] * (ATTN_SCALE * LOG2E)
            q2 = jnp.concatenate([jnp.where(head0, q, 0.0), jnp.where(head0, 0.0, q)], axis=0).astype(BF16)
            kk = kd[pl.ds(slot, 2)].reshape(2 * BAND, LANES)
            s_buf[sl] = _dot_nt(kk, q2)

        def softmax_stage(sl, first):
            for hh in range(2):
                sp = s_buf[sl, 0:BAND, hh * BAND:(hh + 1) * BAND]
                sc = s_buf[sl, BAND:2 * BAND, hh * BAND:(hh + 1) * BAND]
                if first:
                    sp = jnp.where(n > 0, sp, -jnp.inf)
                comb = jnp.where(upper, sp, sc)
                dg = jnp.sum(jnp.where(diag, sp, 0.0), axis=0, keepdims=True)
                m = jnp.maximum(jnp.max(comb, axis=0, keepdims=True), dg)
                p = jnp.exp2(comb - m)
                pd = jnp.exp2(dg - m)
                den = jnp.sum(p, axis=0, keepdims=True) + pd
                p_buf[sl, hh, 0:BAND, :] = jnp.where(upper, p, jnp.where(diag, pd, 0.0)).astype(BF16)
                p_buf[sl, hh, BAND:2 * BAND, :] = jnp.where(upper, 0.0, p).astype(BF16)
                stat_buf[sl, 2 * hh:2 * hh + 1, :] = 1.0 / den
                stat_buf[sl, 2 * hh + 1:2 * hh + 2, :] = m + jnp.log2(den)

        def pv_stage(i, sl):
            j, r = split(i)
            slot = r * (nsub + 1) + j
            rows = rows_of(i)
            vprev = vt[slot]
            vcur = vt[slot + 1]
            ots, lts = [], []
            for hh in range(2):
                vsl = slice(hh * half, (hh + 1) * half)
                vtt = jnp.concatenate([vprev[vsl, :], vcur[vsl, :]], axis=1)
                ots.append(_dot(vtt, p_buf[sl, hh]) * stat_buf[sl, 2 * hh:2 * hh + 1, :])
                lts.append(jnp.broadcast_to(stat_buf[sl, 2 * hh + 1:2 * hh + 2, :], (half, BAND)))
            o_blk = jnp.concatenate(ots, axis=0).T
            lse_blk = jnp.concatenate(lts, axis=0).T
            if g == 0:
                lse_ref[rows, :] = lse_blk
                acc_ref[rows, :] = o_blk
            else:
                lse_run = lse_ref[rows, :]
                acc_run = acc_ref[rows, :]
                mx = jnp.maximum(lse_run, lse_blk)
                ea = jnp.exp2(lse_run - mx)
                eb = jnp.exp2(lse_blk - mx)
                tot = ea + eb
                inv = 1.0 / tot
                merged = (ea * inv) * acc_run + (eb * inv) * o_blk
                if g == len(ATTN_GROUPS) - 1:
                    o_ref[rows, :] = merged
                else:
                    lse_ref[rows, :] = mx + jnp.log2(tot)
                    acc_ref[rows, :] = merged

        def qk_pair(pair, par):
            qk_stage(2 * pair, 2 * par)
            qk_stage(2 * pair + 1, 2 * par + 1)

        def softmax_pair(pair, par):
            softmax_stage(2 * par, 2 * pair < dil)
            softmax_stage(2 * par + 1, 2 * pair + 1 < dil)

        def pv_pair(pair, par):
            pv_stage(2 * pair, 2 * par)
            pv_stage(2 * pair + 1, 2 * par + 1)

        return (carry, deinterleave), (qk_pair, softmax_pair, pv_pair), nblk // 2

    groups = [group_stages(g) for g in range(len(ATTN_GROUPS))]
    for (carry, _), _, _ in groups:
        carry()
    for (_, deinterleave), _, _ in groups:
        deinterleave()
    items = [(stages, pair) for _, stages, npairs in groups for pair in range(npairs)]
    for t in range(len(items) + 2):
        for depth in (2, 1, 0):
            if 0 <= t - depth < len(items):
                stages, pair = items[t - depth]
                stages[depth](pair, (t - depth) % 2)

    @pl.when(n == pl.num_programs(2) - 1)
    def _():
        for g, (win, _) in enumerate(ATTN_GROUPS):
            for which, ref in enumerate((k_refs[g], v_refs[g])):
                t = ref[ATTN_TILE - win:, :].T
                for hh in range(2):
                    (kvo0, kvo1, kvo2)[g][0, which, hh] = t[hh * HEAD_DIM:(hh + 1) * HEAD_DIM, :]


def _attn_prompt(u, batch, seq):
    nt = seq // ATTN_TILE
    hp = A_WIDTH // LANES

    def spec(g, which):
        col = (3 * g + which) * hp
        return pl.BlockSpec((ATTN_TILE, LANES), lambda b, h, n, col=col: (b * nt + n, col + h))

    in_specs = [spec(g, w) for g in range(3) for w in range(3)]
    scratch = []
    for _, dil in ATTN_GROUPS:
        nslots = dil * (ATTN_TILE // (BAND * dil) + 1)
        scratch += [pltpu.VMEM((nslots, BAND, LANES), BF16), pltpu.VMEM((nslots, LANES, BAND), BF16)]
    scratch += [pltpu.VMEM((ATTN_TILE, LANES), F32), pltpu.VMEM((ATTN_TILE, LANES), F32)]
    scratch += [pltpu.VMEM((4, 2 * BAND, 2 * BAND), F32), pltpu.VMEM((4, 2, 2 * BAND, BAND), BF16),
                pltpu.VMEM((4, HALO, BAND), F32)]
    return pl.pallas_call(
        _attn_prompt_kernel,
        grid=(batch, hp, nt),
        in_specs=in_specs,
        out_specs=[pl.BlockSpec((ATTN_TILE, LANES), lambda b, h, n: (b * nt + n, h))]
        + [pl.BlockSpec((1, 2, 2, HEAD_DIM, w), lambda b, h, n: (b, 0, h, 0, 0)) for w, _ in ATTN_GROUPS],
        out_shape=[jax.ShapeDtypeStruct((batch * seq, A_WIDTH), F32)]
        + [jax.ShapeDtypeStruct((batch, 2, HEADS, HEAD_DIM, w), F32) for w, _ in ATTN_GROUPS],
        scratch_shapes=scratch,
        compiler_params=_cparams(("parallel", "parallel", "arbitrary")),
        name="attn_prompt",
    )(*([u] * 9))


def _conv_silu(pad_ref, w_ref, b_ref, rows):
    acc = b_ref[...] + pad_ref[HALO - (CONV_W - 1):HALO - (CONV_W - 1) + rows, :] * w_ref[0:1, :]
    for i in range(1, CONV_W):
        off = HALO - (CONV_W - 1) + i
        acc = acc + pad_ref[off:off + rows, :] * w_ref[i:i + 1, :]
    return _silu(acc)


def _conv_silu_chunk(x_ref, tail_ref, w_ref, b_ref, rows, cols):
    x = x_ref[rows, cols].astype(F32)
    ext = jnp.concatenate([tail_ref[:, cols], x], axis=0)
    tail_ref[:, cols] = x[x.shape[0] - HALO:, :]
    acc = b_ref[:, cols] + x * w_ref[CONV_W - 1:CONV_W, cols]
    for k in range(1, CONV_W):
        acc = acc + pltpu.roll(ext, k, 0)[HALO:, :] * w_ref[CONV_W - 1 - k:CONV_W - k, cols]
    return _silu(acc)


def _softplus(x):
    return jnp.maximum(x, 0.0) + jnp.log1p(jnp.exp(-jnp.abs(x)))


def _ssd_prompt_kernel(x_ref, b_ref, c_ref, dt_ref, dtn_ref, cwx, cwb, cwc, cbx, cbb, cbc,
                       dtb_ref, alog_ref, dexp_ref, e_ref, q_ref, new_ref, c0, c1, c2,
                       y_ref, st_ref, ao_ref, o0, o1, o2,
                       xtail, btail, ctail, state, ex_a, ex_b, cum_a, cum_b):
    c = pl.program_id(1)
    nc = pl.num_programs(1)

    @pl.when(c == 0)
    def _():
        xtail[...] = jnp.zeros(xtail.shape, F32)
        btail[...] = jnp.zeros(btail.shape, F32)
        ctail[...] = jnp.zeros(ctail.shape, F32)
        state[...] = jnp.zeros(state.shape, F32)

    ti = lax.broadcasted_iota(jnp.int32, (CHUNK, CHUNK), 0)
    si = lax.broadcasted_iota(jnp.int32, (CHUNK, CHUNK), 1)
    causal = si <= ti
    lane = lax.broadcasted_iota(jnp.int32, (CHUNK, LANES), 1)
    head0 = lane < SSD_HEAD_DIM
    hpg = SSD_HEADS // SSD_GROUPS

    def prepare(dt_blk, ex_ref, cum_ref):
        dt = _softplus(dt_blk + dtb_ref[...])
        dta = dt * -jnp.exp(alog_ref[...])
        tril = jnp.where(causal, 1.0, 0.0).astype(F32)
        cum = jnp.dot(tril, dta, preferred_element_type=F32, precision=lax.Precision.HIGHEST)
        cum2 = cum * LOG2E
        cum_ref[0] = cum2
        cum_ref[1] = cum2.T
        expcum = jnp.exp2(cum2)
        to_end = jnp.exp2(cum2[CHUNK - 1:CHUNK, :] - cum2) * dt
        ex_ref[...] = _expand01(jnp.concatenate([expcum, to_end, dt], axis=0), e_ref[...])

    @pl.when(c == 0)
    def _():
        prepare(dt_ref[0:CHUNK, :], ex_a, cum_a)

    def chunk(rows, ex_s, cum_s, hooks):
        cum2 = cum_s[0]
        cum2_t = cum_s[1]
        dec_x = ex_s[CHUNK - 1:CHUNK, :]
        for g in range(SSD_GROUPS):
            for hook in hooks[g]:
                hook()
            gs = slice(g * SSD_GROUP_W, (g + 1) * SSD_GROUP_W)
            ns = slice(g * SSD_STATE, (g + 1) * SSD_STATE)
            bg = _conv_silu_chunk(b_ref, btail, cwb, cbb, rows, ns).astype(BF16)
            cg = _conv_silu_chunk(c_ref, ctail, cwc, cbc, rows, ns).astype(BF16)
            xg = _conv_silu_chunk(x_ref, xtail, cwx, cbx, rows, gs)
            cb = _dot_nt(cg, bg)
            xdt = xg * ex_s[2 * CHUNK:3 * CHUNK, gs]
            ys = []
            for pr in range(hpg // 2):
                ws = []
                for hh in range(2):
                    h = g * hpg + 2 * pr + hh
                    colb = jnp.broadcast_to(cum2[:, h:h + 1], (CHUNK, CHUNK))
                    rowb = jnp.broadcast_to(cum2_t[h:h + 1, :], (CHUNK, CHUNK))
                    decay = jnp.exp2(jnp.where(causal, colb - rowb, -jnp.inf))
                    ws.append((cb * decay).astype(BF16))
                w2 = jnp.concatenate(ws, axis=1)
                xp = xdt[:, pr * LANES:(pr + 1) * LANES]
                x2 = jnp.concatenate([jnp.where(head0, xp, 0.0), jnp.where(head0, 0.0, xp)], axis=0).astype(BF16)
                ys.append(_dot(w2, x2))
            st = state[g]
            y_state = _dot(cg, st.astype(BF16))
            y_ref[rows, gs] = jnp.concatenate(ys, axis=1) + y_state * ex_s[0:CHUNK, gs] + dexp_ref[:, gs] * xg
            xs = (xg * ex_s[CHUNK:2 * CHUNK, gs]).astype(BF16)
            state[g] = dec_x[:, gs] * st + _dot_tn(bg, xs)

    sample_pieces = [p for hx in range(q_ref.shape[1])
                     for p in _attn_sample_pieces(hx, q_ref, new_ref, (c0, c1, c2), ao_ref, (o0, o1, o2))]
    nslots = 2 * SSD_GROUPS
    hooks = [[] for _ in range(nslots)]
    for idx, piece in enumerate(sample_pieces):
        hooks[idx * nslots // len(sample_pieces)].append(piece)
    hooks[1].append(lambda: prepare(dt_ref[CHUNK:2 * CHUNK, :], ex_b, cum_b))
    hooks[SSD_GROUPS + 1].append(lambda: prepare(dtn_ref[...], ex_a, cum_a))
    chunk(slice(0, CHUNK), ex_a, cum_a, hooks[:SSD_GROUPS])
    chunk(slice(CHUNK, 2 * CHUNK), ex_b, cum_b, hooks[SSD_GROUPS:])

    @pl.when(c == nc - 1)
    def _():
        for g in range(SSD_GROUPS):
            st_ref[0, g * hpg:(g + 1) * hpg, :, :] = state[g].T.reshape(hpg, SSD_HEAD_DIM, SSD_STATE)


def _ssd_prompt(u, dt_raw, conv_w, conv_b, dt_bias, a_log, d_exp, e_mat, batch, seq, q_s, new_t, caches):
    nc = seq // CHUNK
    cwx, cwb, cwc = conv_w[:, :D_INNER], conv_w[:, D_INNER:D_INNER + SSD_GROUP_W], conv_w[:, D_INNER + SSD_GROUP_W:]
    cbx, cbb, cbc = conv_b[:, :D_INNER], conv_b[:, D_INNER:D_INNER + SSD_GROUP_W], conv_b[:, D_INNER + SSD_GROUP_W:]
    full = lambda shape: pl.BlockSpec(shape, lambda b, c: (0,) * len(shape))
    dbatch, _, _, n_new, _ = q_s.shape
    assert nc % 2 == 0
    nchunks, nc, rows = nc, nc // 2, 2 * CHUNK
    steps = batch * nc
    assert (dbatch * HEADS) % steps == 0, "sample heads must spread evenly over the prompt's SSD steps"
    hps = dbatch * HEADS // steps
    assert HEADS % hps == 0
    spb = HEADS // hps
    sb = lambda b, c: (b * nc + c) // spb
    hb = lambda b, c: (b * nc + c) % spb
    q_spec = pl.BlockSpec((1, hps, len(ATTN_GROUPS), n_new, HEAD_DIM), lambda b, c: (sb(b, c), hb(b, c), 0, 0, 0))
    new_spec = pl.BlockSpec((1, hps, HEAD_DIM, LANES), lambda b, c: (sb(b, c), hb(b, c), 0, 0))
    ao_spec = pl.BlockSpec((1, hps, n_new, HEAD_DIM), lambda b, c: (sb(b, c), hb(b, c), 0, 0))
    cache_specs = [pl.BlockSpec((1, 2, hps, HEAD_DIM, w), lambda b, c: (sb(b, c), 0, hb(b, c), 0, 0))
                   for w, _ in ATTN_GROUPS]
    return pl.pallas_call(
        _ssd_prompt_kernel,
        grid=(batch, nc),
        in_specs=[
            pl.BlockSpec((rows, D_INNER), lambda b, c: (b * nc + c, X_OFF // D_INNER)),
            pl.BlockSpec((rows, SSD_GROUP_W), lambda b, c: (b * nc + c, B_OFF // SSD_GROUP_W)),
            pl.BlockSpec((rows, SSD_GROUP_W), lambda b, c: (b * nc + c, C_OFF // SSD_GROUP_W)),
            pl.BlockSpec((rows, LANES), lambda b, c: (b * nc + c, 0)),
            pl.BlockSpec((CHUNK, LANES), lambda b, c: (b * nchunks + jnp.minimum(2 * c + 2, nchunks - 1), 0)),
            full((CONV_W, D_INNER)), full((CONV_W, SSD_GROUP_W)), full((CONV_W, SSD_GROUP_W)),
            full((1, D_INNER)), full((1, SSD_GROUP_W)), full((1, SSD_GROUP_W)),
            full((1, LANES)), full((1, LANES)), full((1, D_INNER)), full((LANES, D_INNER)),
            q_spec, new_spec,
        ] + cache_specs,
        out_specs=[
            pl.BlockSpec((rows, D_INNER), lambda b, c: (b * nc + c, 0)),
            pl.BlockSpec((1, SSD_HEADS, SSD_HEAD_DIM, SSD_STATE), lambda b, c: (b, 0, 0, 0)),
            ao_spec,
        ] + cache_specs,
        out_shape=[
            jax.ShapeDtypeStruct((batch * seq, D_INNER), F32),
            jax.ShapeDtypeStruct((batch, SSD_HEADS, SSD_HEAD_DIM, SSD_STATE), F32),
            jax.ShapeDtypeStruct((dbatch, HEADS, n_new, HEAD_DIM), F32),
        ] + [jax.ShapeDtypeStruct(cc.shape, F32) for cc in caches],
        scratch_shapes=[
            pltpu.VMEM((HALO, D_INNER), F32),
            pltpu.VMEM((HALO, SSD_GROUP_W), F32),
            pltpu.VMEM((HALO, SSD_GROUP_W), F32),
            pltpu.VMEM((SSD_GROUPS, SSD_STATE, SSD_GROUP_W), F32),
            pltpu.VMEM((3 * CHUNK, D_INNER), F32), pltpu.VMEM((3 * CHUNK, D_INNER), F32),
            pltpu.VMEM((2, CHUNK, LANES), F32), pltpu.VMEM((2, CHUNK, LANES), F32),
        ],
        compiler_params=_cparams(("parallel", "arbitrary")),
        name="ssd_prompt",
    )(u, u, u, dt_raw, dt_raw, cwx, cwb, cwc, cbx, cbb, cbc, dt_bias, a_log, d_exp, e_mat, q_s, new_t, *caches)


def _mix_ffn_kernel(a_ref, y_ref, z_ref, ga_ref, gb_ref, x_ref, wa_ref, ws_ref, wo_ref,
                    sn_ref, pn_ref, wu_ref, wd_ref, n1_ref, n2_ref, o_ref):
    a_proj = _dot(a_ref[...].astype(BF16), wa_ref[...])
    g = y_ref[...] * _silu(z_ref[...].astype(F32))
    parts = []
    for i in range(SSD_GROUPS):
        gi = g[:, i * SSD_GROUP_W:(i + 1) * SSD_GROUP_W]
        parts.append(gi * lax.rsqrt(jnp.mean(gi * gi, axis=-1, keepdims=True) + RMS_EPS))
    yn = (jnp.concatenate(parts, axis=1) * sn_ref[...]).astype(BF16)
    b_proj = _dot(yn, ws_ref[...])
    mixed_in = _sigmoid(ga_ref[...].astype(F32)) * a_proj + _sigmoid(gb_ref[...].astype(F32)) * b_proj
    mixed = _dot(mixed_in.astype(BF16), wo_ref[...])
    x1 = x_ref[...] + _rms(mixed, pn_ref[...])
    h = _rms(x1, n1_ref[...]).astype(BF16)
    gu = _dot(h, wu_ref[...])
    act = (_silu(gu[:, :D_FF]) * gu[:, D_FF:]).astype(BF16)
    f = _dot(act, wd_ref[...])
    o_ref[...] = x1 + _rms(f, n2_ref[...])


def _mix_ffn(a_out, y, u, x, w_attn_out, w_ssd_out, w_out, ssd_norm, post_mix_norm,
             w_up, w_down, pre_norm, post_norm, tm):
    t = x.shape[0]
    const = lambda shape: pl.BlockSpec(shape, lambda i: (0, 0), pipeline_mode=pl.Buffered(1))
    return pl.pallas_call(
        _mix_ffn_kernel,
        grid=(t // tm,),
        in_specs=[
            pl.BlockSpec((tm, A_WIDTH), lambda i: (i, 0)),
            pl.BlockSpec((tm, D_INNER), lambda i: (i, 0)),
            pl.BlockSpec((tm, D_INNER), lambda i: (i, Z_OFF // D_INNER)),
            pl.BlockSpec((tm, D_MODEL), lambda i: (i, GATE_OFF // D_MODEL)),
            pl.BlockSpec((tm, D_MODEL), lambda i: (i, GATE_OFF // D_MODEL + 1)),
            pl.BlockSpec((tm, D_MODEL), lambda i: (i, 0)),
            const((A_WIDTH, D_MODEL)), const((D_INNER, D_MODEL)), const((D_MODEL, D_MODEL)),
            const((1, D_INNER)), const((1, D_MODEL)),
            const((D_MODEL, 2 * D_FF)), const((D_FF, D_MODEL)),
            const((1, D_MODEL)), const((1, D_MODEL)),
        ],
        out_specs=pl.BlockSpec((tm, D_MODEL), lambda i: (i, 0)),
        out_shape=jax.ShapeDtypeStruct((t, D_MODEL), F32),
        compiler_params=_cparams(("parallel",)),
        name="mix_ffn",
    )(a_out, y, u, u, u, x, w_attn_out, w_ssd_out, w_out, ssd_norm, post_mix_norm,
      w_up, w_down, pre_norm, post_norm)


def _attn_sample_pieces(hx, q_ref, new_ref, caches, ao_ref, outs):
    n_new = q_ref.shape[3]
    run = {}

    def new_cols(g, which):
        return pltpu.roll(new_ref[0, hx], LANES - n_new - (2 * g + which) * n_new, 1)

    def attend(g):
        win, dil = ATTN_GROUPS[g]
        q = (q_ref[0, hx, g] * ATTN_SCALE).astype(BF16)
        k_t = caches[g][0, 0, hx]
        v_t = caches[g][0, 1, hx]
        knt, vnt = new_cols(g, 0), new_cols(g, 1)
        qi = lax.broadcasted_iota(jnp.int32, (n_new, win), 0)
        ci = lax.broadcasted_iota(jnp.int32, (n_new, win), 1)
        ok_c = jnp.logical_and(ci >= qi, ((ci - qi) & (dil - 1)) == 0)
        qn = lax.broadcasted_iota(jnp.int32, (n_new, LANES), 0)
        nn = lax.broadcasted_iota(jnp.int32, (n_new, LANES), 1) - (LANES - n_new)
        ok_n = jnp.logical_and(jnp.logical_and(nn >= 0, nn <= qn), ((qn - nn) & (dil - 1)) == 0)
        s_c = jnp.where(ok_c, _dot(q, k_t.astype(BF16)), -jnp.inf)
        s_n = jnp.where(ok_n, _dot(q, knt.astype(BF16)), -jnp.inf)
        m = jnp.maximum(jnp.max(s_c, axis=1, keepdims=True), jnp.max(s_n, axis=1, keepdims=True))
        p_c = jnp.exp(s_c - m)
        p_n = jnp.exp(s_n - m)
        den = jnp.sum(p_c, axis=1, keepdims=True) + jnp.sum(p_n, axis=1, keepdims=True)
        o = _dot_nt(p_c.astype(BF16), v_t.astype(BF16)) + _dot_nt(p_n.astype(BF16), vnt.astype(BF16))
        if not run:
            run.update(m=m, l=den, o=o)
        else:
            m_new = jnp.maximum(run["m"], m)
            ea = jnp.exp(run["m"] - m_new)
            eb = jnp.exp(m - m_new)
            run.update(m=m_new, l=ea * run["l"] + eb * den, o=ea * run["o"] + eb * o)
        if g == len(ATTN_GROUPS) - 1:
            ao_ref[0, hx] = run["o"] * (1.0 / run["l"])

    def shift(g, which):
        win = ATTN_GROUPS[g][0]
        keep = lax.broadcasted_iota(jnp.int32, (HEAD_DIM, LANES), 1) < LANES - n_new
        rolled = pltpu.roll(caches[g][0, which, hx], win - n_new, 1)
        if win > LANES:
            outs[g][0, which, hx, :, 0:win - LANES] = rolled[:, 0:win - LANES]
        outs[g][0, which, hx, :, win - LANES:win] = jnp.where(keep, rolled[:, win - LANES:win], new_cols(g, which))

    pieces = []
    for g in range(len(ATTN_GROUPS)):
        pieces += [functools.partial(attend, g), functools.partial(shift, g, 0), functools.partial(shift, g, 1)]
    return pieces


def _ssd_sample_kernel(x_ref, b_ref, c_ref, dt_ref, sx_ref, sb_ref, sc_ref, st_in,
                       cwx, cwb, cwc, cbx, cbb, cbc, dtb_ref, alog_ref, dexp_ref, e_ref,
                       y_ref, st_out, xpad, bpad, cpad):
    n = x_ref.shape[0]
    keep = CONV_W - 1
    for pad, new, old in ((xpad, x_ref, sx_ref), (bpad, b_ref, sb_ref), (cpad, c_ref, sc_ref)):
        pad[0:HALO, :] = jnp.zeros((HALO, pad.shape[1]), F32)
        pad[HALO - keep:HALO, :] = old[0]
        pad[HALO:HALO + n, :] = new[...]
    xc = _conv_silu(xpad, cwx, cbx, n)
    bm = _conv_silu(bpad, cwb, cbb, n)
    cm = _conv_silu(cpad, cwc, cbc, n)

    dt = _softplus(dt_ref[...] + dtb_ref[...])
    a = -jnp.exp(alog_ref[...])
    dta = dt * a
    ti = lax.broadcasted_iota(jnp.int32, (n, n), 0)
    si = lax.broadcasted_iota(jnp.int32, (n, n), 1)
    tril = jnp.where(si <= ti, 1.0, 0.0).astype(F32)
    cum = jnp.dot(tril, dta, preferred_element_type=F32, precision=lax.Precision.HIGHEST)
    cum_last = cum[n - 1:n, :]
    expcum = jnp.exp(cum)
    to_end = jnp.exp(cum_last - cum) * dt

    hpg = SSD_HEADS // SSD_GROUPS
    lane = lax.broadcasted_iota(jnp.int32, (n, LANES), 1)
    trow = lax.broadcasted_iota(jnp.int32, (n, LANES), 0)
    cbs = [_dot_nt(cm[:, g * SSD_STATE:(g + 1) * SSD_STATE].astype(BF16),
                   bm[:, g * SSD_STATE:(g + 1) * SSD_STATE].astype(BF16)) for g in range(SSD_GROUPS)]
    coefs = []
    for s in range(n):
        cbh = jnp.zeros((n, LANES), F32)
        for g in range(SSD_GROUPS):
            in_g = jnp.logical_and(lane >= g * hpg, lane < (g + 1) * hpg)
            cbh = jnp.where(in_g, jnp.broadcast_to(cbs[g][:, s:s + 1], (n, LANES)), cbh)
        decay = jnp.exp(jnp.where(trow >= s, cum - cum[s:s + 1, :], -jnp.inf))
        coefs.append(cbh * decay * dt[s:s + 1, :])
    ex = _expand01(jnp.concatenate(coefs + [expcum, to_end], axis=0), e_ref[...])
    y = dexp_ref[...] * xc
    for s in range(n):
        y = y + ex[s * n:(s + 1) * n, :] * xc[s:s + 1, :]
    expcum_x = ex[n * n:n * n + n, :]
    to_end_x = ex[n * n + n:n * n + 2 * n, :]
    xs = xc * to_end_x
    dec_t = jnp.broadcast_to(expcum[n - 1:n, :], (LANES, LANES)).T
    ys = []
    for g in range(SSD_GROUPS):
        gs = slice(g * SSD_GROUP_W, (g + 1) * SSD_GROUP_W)
        ns = slice(g * SSD_STATE, (g + 1) * SSD_STATE)
        st = st_in[0, g * hpg:(g + 1) * hpg].reshape(SSD_GROUP_W, SSD_STATE)
        ys.append(_dot_nt(cm[:, ns].astype(BF16), st.astype(BF16)))
        upd = _dot_tn(xs[:, gs].astype(BF16), bm[:, ns].astype(BF16))
        for hh in range(hpg):
            h = g * hpg + hh
            rows = slice(hh * SSD_HEAD_DIM, (hh + 1) * SSD_HEAD_DIM)
            st_out[0, h] = dec_t[h:h + 1, :] * st[rows, :] + upd[rows, :]
    y_ref[...] = y + jnp.concatenate(ys, axis=1) * expcum_x


def _ssd_sample(u, dt_raw, conv_state, ssm_state, conv_w, conv_b, dt_bias, a_log, d_exp, e_mat, n_new):
    bsz = ssm_state.shape[0]
    keep = CONV_W - 1
    cwx, cwb, cwc = conv_w[:, :D_INNER], conv_w[:, D_INNER:D_INNER + SSD_GROUP_W], conv_w[:, D_INNER + SSD_GROUP_W:]
    cbx, cbb, cbc = conv_b[:, :D_INNER], conv_b[:, D_INNER:D_INNER + SSD_GROUP_W], conv_b[:, D_INNER + SSD_GROUP_W:]
    full = lambda shape: pl.BlockSpec(shape, lambda b: (0,) * len(shape))
    st_spec = pl.BlockSpec((1, SSD_HEADS, SSD_HEAD_DIM, SSD_STATE), lambda b: (b, 0, 0, 0))
    return pl.pallas_call(
        _ssd_sample_kernel,
        grid=(bsz,),
        in_specs=[
            pl.BlockSpec((n_new, D_INNER), lambda b: (b, X_OFF // D_INNER)),
            pl.BlockSpec((n_new, SSD_GROUP_W), lambda b: (b, B_OFF // SSD_GROUP_W)),
            pl.BlockSpec((n_new, SSD_GROUP_W), lambda b: (b, C_OFF // SSD_GROUP_W)),
            pl.BlockSpec((n_new, LANES), lambda b: (b, 0)),
            pl.BlockSpec((1, keep, D_INNER), lambda b: (b, 0, 0)),
            pl.BlockSpec((1, keep, SSD_GROUP_W), lambda b: (b, 0, D_INNER // SSD_GROUP_W)),
            pl.BlockSpec((1, keep, SSD_GROUP_W), lambda b: (b, 0, D_INNER // SSD_GROUP_W + 1)),
            st_spec,
            full((CONV_W, D_INNER)), full((CONV_W, SSD_GROUP_W)), full((CONV_W, SSD_GROUP_W)),
            full((1, D_INNER)), full((1, SSD_GROUP_W)), full((1, SSD_GROUP_W)),
            full((1, LANES)), full((1, LANES)), full((1, D_INNER)), full((LANES, D_INNER)),
        ],
        out_specs=[pl.BlockSpec((n_new, D_INNER), lambda b: (b, 0)), st_spec],
        out_shape=[
            jax.ShapeDtypeStruct((bsz * n_new, D_INNER), F32),
            jax.ShapeDtypeStruct(ssm_state.shape, F32),
        ],
        scratch_shapes=[
            pltpu.VMEM((HALO + n_new, D_INNER), F32),
            pltpu.VMEM((HALO + n_new, SSD_GROUP_W), F32),
            pltpu.VMEM((HALO + n_new, SSD_GROUP_W), F32),
        ],
        compiler_params=_cparams(("parallel",)),
        name="ssd_sample",
    )(u, u, u, dt_raw, conv_state, conv_state, conv_state, ssm_state,
      cwx, cwb, cwc, cbx, cbb, cbc, dt_bias, a_log, d_exp, e_mat)


def _prep_weights(w_in, dt_bias, d_skip):
    q_end = 3 * len(ATTN_GROUPS) * A_WIDTH
    z_end = q_end + D_INNER
    x_end = z_end + D_INNER
    b_end = x_end + SSD_GROUP_W
    c_end = b_end + SSD_GROUP_W
    dt_end = c_end + SSD_HEADS
    w_main = jnp.concatenate(
        [w_in[:, q_end:z_end], w_in[:, dt_end:], w_in[:, z_end:c_end], w_in[:, :q_end]], axis=1).astype(BF16)
    w_dt = jnp.pad(w_in[:, c_end:dt_end], ((0, 0), (0, LANES - SSD_HEADS))).astype(BF16)
    dtb = jnp.pad(dt_bias, (0, LANES - SSD_HEADS)).reshape(1, LANES)
    d_exp = jnp.repeat(d_skip, SSD_HEAD_DIM).reshape(1, D_INNER)
    return w_main, w_dt, dtb, d_exp


def _tail_conv(u, batch, seq):
    u3 = u.reshape(batch, seq, N_REST)
    return u3[:, seq - (CONV_W - 1):, X_OFF:C_OFF + SSD_GROUP_W].astype(F32)


def kernel(x_prompt, x_sample, cache_kv_w128, cache_kv_w512, cache_kv_w2048, state_conv, state_ssm,
           pre_mix_norm, w_in, conv_w, conv_b, dt_bias, a_log, d_skip, ssd_norm, w_attn_out, w_ssd_out,
           w_out, post_mix_norm, pre_ffn_norm, w_up, w_down, post_ffn_norm):
    batch, seq, _ = x_prompt.shape
    dbatch, dseq, _ = x_sample.shape
    assert w_in.shape[0] == 1, "single-layer trunk"
    assert seq % ATTN_TILE == 0 and dseq == HALO

    w_main, w_dt, dtb, d_exp = _prep_weights(w_in[0], dt_bias[0], d_skip[0])
    alog = jnp.pad(a_log[0], (0, LANES - SSD_HEADS)).reshape(1, LANES)
    e_mat = (jnp.arange(LANES)[:, None] == (jnp.arange(D_INNER)[None, :] // SSD_HEAD_DIM)).astype(BF16)
    nw = pre_mix_norm[0].reshape(1, D_MODEL)
    cw, cb = conv_w[0], conv_b[0].reshape(1, -1)
    wa, ws, wo = w_attn_out[0].astype(BF16), w_ssd_out[0].astype(BF16), w_out[0].astype(BF16)
    wu, wd = w_up[0].astype(BF16), w_down[0].astype(BF16)
    sn = ssd_norm[0].reshape(1, D_INNER)
    pmn, pfn, qfn = (post_mix_norm[0].reshape(1, -1), pre_ffn_norm[0].reshape(1, -1),
                     post_ffn_norm[0].reshape(1, -1))

    xp = x_prompt.reshape(batch * seq, D_MODEL)
    xs = x_sample.reshape(dbatch * dseq, D_MODEL)
    ur_p, uq_p, dt_p = _inproj(xp, nw, w_main, w_dt, min(INPROJ_ROWS, batch * seq), INPROJ_COLS, BF16)
    ur_s, uq_s, dt_s = _inproj(xs, nw, w_main, w_dt, min(INPROJ_ROWS, dbatch * dseq), INPROJ_COLS, F32)
    qkv = uq_s.reshape(dbatch, dseq, 3, 3, HEADS, HEAD_DIM)
    q_s = qkv[:, :, :, 0].transpose(0, 3, 2, 1, 4)
    new_t = qkv[:, :, :, 1:3].transpose(0, 4, 5, 2, 3, 1).reshape(dbatch, HEADS, HEAD_DIM, 6 * dseq)
    new_t = jnp.pad(new_t, ((0, 0), (0, 0), (0, 0), (0, LANES - 6 * dseq)))
    caches = [c[0].transpose(0, 2, 3, 4, 1) for c in (cache_kv_w128, cache_kv_w512, cache_kv_w2048)]

    a_p, *kvt_p = _attn_prompt(uq_p, batch, seq)
    y_p, ssm_p, ao, nc0, nc1, nc2 = _ssd_prompt(ur_p, dt_p, cw, cb, dtb, alog, d_exp, e_mat, batch, seq,
                                                q_s, new_t, caches)
    out_p = _mix_ffn(a_p, y_p, ur_p, xp, wa, ws, wo, sn, pmn, wu, wd, pfn, qfn,
                     min(MIX_FFN_ROWS, batch * seq)).reshape(batch, seq, D_MODEL)
    kv_p = [t.transpose(0, 4, 1, 2, 3)[None] for t in kvt_p]
    conv_p = _tail_conv(ur_p, batch, seq)[None]

    a_s = ao.transpose(0, 2, 1, 3).reshape(dbatch * dseq, A_WIDTH)
    kv_s = [c.transpose(0, 4, 1, 2, 3)[None] for c in (nc0, nc1, nc2)]
    y_s, ssm_s = _ssd_sample(ur_s, dt_s, state_conv[0], state_ssm[0], cw, cb, dtb, alog, d_exp,
                             e_mat, dseq)
    tm_s = min(MIX_FFN_ROWS, dbatch * dseq)
    out_s = _mix_ffn(a_s, y_s, ur_s, xs, wa, ws, wo, sn, pmn, wu, wd, pfn, qfn, tm_s).reshape(dbatch, dseq, D_MODEL)
    conv_s = _tail_conv(ur_s, dbatch, dseq)[None]

    return (out_p, out_s, kv_p[0], kv_p[1], kv_p[2], conv_p, ssm_p[None],
            kv_s[0], kv_s[1], kv_s[2], conv_s, ssm_s[None])
```

```python
import functools

import jax
import jax.numpy as jnp
from jax import lax
from jax.experimental import pallas as pl
from jax.experimental.pallas import tpu as pltpu

F32 = jnp.float32
BF16 = jnp.bfloat16

D_MODEL = 1024
ATTN_GROUPS = ((128, 1), (512, 4), (2048, 16))
BAND = 128
HEADS = 8
HEAD_DIM = 64
A_WIDTH = HEADS * HEAD_DIM
ATTN_SCALE = HEAD_DIM ** -0.5
LOG2E = 1.4426950408889634
ATTN_TILE = 2048

D_INNER = 2048
SSD_HEADS = 32
SSD_HEAD_DIM = 64
SSD_GROUPS = 4
SSD_GROUP_W = D_INNER // SSD_GROUPS
SSD_STATE = 128
CONV_W = 4
CHUNK = 128
D_FF = 2816
RMS_EPS = 1e-6
LANES = 128
HALO = 8

Z_OFF = 0
GATE_OFF = Z_OFF + D_INNER
X_OFF = GATE_OFF + 2 * D_MODEL
B_OFF = X_OFF + D_INNER
C_OFF = B_OFF + SSD_GROUP_W
N_REST = C_OFF + SSD_GROUP_W
N_QKV = 3 * len(ATTN_GROUPS) * A_WIDTH
N_MAIN = N_REST + N_QKV

V7X_VMEM_BYTES = 64 * 1024 * 1024
VMEM_LIMIT = V7X_VMEM_BYTES - 8 * 1024 * 1024
INPROJ_ROWS = 2048
INPROJ_COLS = 512
MIX_FFN_ROWS = 256


def _cparams(sem):
    return pltpu.CompilerParams(dimension_semantics=sem, vmem_limit_bytes=VMEM_LIMIT)


def _rms(x, w):
    return x * lax.rsqrt(jnp.mean(x * x, axis=-1, keepdims=True) + RMS_EPS) * w


def _silu(x):
    h = 0.5 * x
    return h + h * jnp.tanh(h)


def _sigmoid(x):
    return 0.5 + 0.5 * jnp.tanh(0.5 * x)


def _dot(a, b):
    return jnp.dot(a, b, preferred_element_type=F32)


def _dot_nt(a, b):
    return lax.dot_general(a, b, (((1,), (1,)), ((), ())), preferred_element_type=F32)


def _dot_tn(a, b):
    return lax.dot_general(a, b, (((0,), (0,)), ((), ())), preferred_element_type=F32)


def _expand01(a, e):
    hi = a.astype(BF16)
    mid = (a - hi.astype(F32)).astype(BF16)
    return _dot(hi, e) + _dot(mid, e)


def _inproj_order(j, rest_tiles, q_tiles):
    mixed = j < 2 * q_tiles
    is_q = jnp.logical_and(mixed, j % 2 == 1)
    rest_idx = jnp.where(mixed, j // 2, j - q_tiles)
    q_idx = jnp.clip((j - 1) // 2, 0, q_tiles - 1)
    return is_q, rest_idx, q_idx


def _inproj_kernel(x_ref, nw_ref, w_ref, wdt_ref, r_ref, q_ref, dt_ref, h_ref, *, rest_tiles, q_tiles):
    j = pl.program_id(1)
    is_q, _, _ = _inproj_order(j, rest_tiles, q_tiles)

    @pl.when(j == 0)
    def _():
        hb = _rms(x_ref[...], nw_ref[...]).astype(BF16)
        h_ref[...] = hb
        dt_ref[...] = _dot(hb, wdt_ref[...])

    @pl.when(jnp.logical_not(is_q))
    def _():
        r_ref[...] = _dot(h_ref[...], w_ref[...]).astype(r_ref.dtype)

    @pl.when(is_q)
    def _():
        q_ref[...] = _dot(h_ref[...], w_ref[...])


def _inproj(x, nw, w_main, w_dt, tm, tn, rest_dtype):
    t = x.shape[0]
    rest_tiles, q_tiles = N_REST // tn, N_QKV // tn
    assert rest_tiles >= q_tiles
    order = functools.partial(_inproj_order, rest_tiles=rest_tiles, q_tiles=q_tiles)

    def w_tile(i, j):
        is_q, rest_idx, q_idx = order(j)
        return 0, jnp.where(is_q, rest_tiles + q_idx, rest_idx)

    return pl.pallas_call(
        functools.partial(_inproj_kernel, rest_tiles=rest_tiles, q_tiles=q_tiles),
        grid=(t // tm, N_MAIN // tn),
        in_specs=[
            pl.BlockSpec((tm, D_MODEL), lambda i, j: (i, 0)),
            pl.BlockSpec((1, D_MODEL), lambda i, j: (0, 0)),
            pl.BlockSpec((D_MODEL, tn), w_tile),
            pl.BlockSpec((D_MODEL, LANES), lambda i, j: (0, 0)),
        ],
        out_specs=[
            pl.BlockSpec((tm, tn), lambda i, j: (i, order(j)[1])),
            pl.BlockSpec((tm, tn), lambda i, j: (i, order(j)[2])),
            pl.BlockSpec((tm, LANES), lambda i, j: (i, 0)),
        ],
        out_shape=[
            jax.ShapeDtypeStruct((t, N_REST), rest_dtype),
            jax.ShapeDtypeStruct((t, N_QKV), F32),
            jax.ShapeDtypeStruct((t, LANES), F32),
        ],
        scratch_shapes=[pltpu.VMEM((tm, D_MODEL), BF16)],
        compiler_params=_cparams(("parallel", "arbitrary")),
        name="inproj",
    )(x, nw, w_main, w_dt)


def _attn_prompt_kernel(q0, k0, v0, q1, k1, v1, q2, k2, v2, o_ref, kvo0, kvo1, kvo2,
                        kd0, vt0, kd1, vt1, kd2, vt2, lse_ref, acc_ref, s_buf, p_buf, stat_buf, extra=None):
    n = pl.program_id(2)
    q_refs, k_refs, v_refs = (q0, q1, q2), (k0, k1, k2), (v0, v1, v2)
    kds, vts = (kd0, kd1, kd2), (vt0, vt1, vt2)
    krow = lax.broadcasted_iota(jnp.int32, (BAND, LANES), 0)
    qcol = lax.broadcasted_iota(jnp.int32, (BAND, LANES), 1)
    upper = krow > qcol
    diag = krow == qcol
    head0 = qcol < HEAD_DIM
    half = BAND // 2

    def group_stages(g):
        dil = ATTN_GROUPS[g][1]
        span = BAND * dil
        nsub = ATTN_TILE // span
        nblk = nsub * dil
        q_ref, k_ref, v_ref, kd, vt = q_refs[g], k_refs[g], v_refs[g], kds[g], vts[g]

        def split(i):
            if dil == 1:
                return i, 0
            j = i // dil
            return j, i - j * dil

        def rows_of(i):
            j, r = split(i)
            if dil == 1:
                return pl.ds(i * BAND if isinstance(i, int) else pl.multiple_of(i * BAND, BAND), BAND)
            return pl.ds(j * span + r, BAND, stride=dil)

        def carry():
            @pl.when(n == 0)
            def _():
                for r in range(dil):
                    kd[r * (nsub + 1)] = jnp.zeros((BAND, LANES), BF16)
                    vt[r * (nsub + 1)] = jnp.zeros((LANES, BAND), BF16)

            @pl.when(n > 0)
            def _():
                for r in range(dil):
                    kd[r * (nsub + 1)] = kd[r * (nsub + 1) + nsub]
                    vt[r * (nsub + 1)] = vt[r * (nsub + 1) + nsub]

        def deinterleave():
            for i in range(nblk):
                j, r = split(i)
                slot = r * (nsub + 1) + j + 1
                rows = rows_of(i)
                kd[slot] = k_ref[rows, :].astype(BF16)
                vt[slot] = v_ref[rows, :].T.astype(BF16)

        def qk_stage(i, sl):
            j, r = split(i)
            slot = r * (nsub + 1) + j
            q = q_ref[rows_of(i), :] * (ATTN_SCALE * LOG2E)
            q2 = jnp.concatenate([jnp.where(head0, q, 0.0), jnp.where(head0, 0.0, q)], axis=0).astype(BF16)
            kk = kd[pl.ds(slot, 2)].reshape(2 * BAND, LANES)
            s_buf[sl] = _dot_nt(kk, q2)

        def softmax_stage(sl, first):
            for hh in range(2):
                sp = s_buf[sl, 0:BAND, hh * BAND:(hh + 1) * BAND]
                sc = s_buf[sl, BAND:2 * BAND, hh * BAND:(hh + 1) * BAND]
                if first:
                    sp = jnp.where(n > 0, sp, -jnp.inf)
                comb = jnp.where(upper, sp, sc)
                dg = jnp.sum(jnp.where(diag, sp, 0.0), axis=0, keepdims=True)
                m = jnp.maximum(jnp.max(comb, axis=0, keepdims=True), dg)
                p = jnp.exp2(comb - m)
                pd = jnp.exp2(dg - m)
                den = jnp.sum(p, axis=0, keepdims=True) + pd
                p_buf[sl, hh, 0:BAND, :] = jnp.where(upper, p, jnp.where(diag, pd, 0.0)).astype(BF16)
                p_buf[sl, hh, BAND:2 * BAND, :] = jnp.where(upper, 0.0, p).astype(BF16)
                stat_buf[sl, 2 * hh:2 * hh + 1, :] = 1.0 / den
                stat_buf[sl, 2 * hh + 1:2 * hh + 2, :] = m + jnp.log2(den)

        def pv_stage(i, sl):
            j, r = split(i)
            slot = r * (nsub + 1) + j
            rows = rows_of(i)
            vprev = vt[slot]
            vcur = vt[slot + 1]
            ots, lts = [], []
            for hh in range(2):
                vsl = slice(hh * half, (hh + 1) * half)
                vtt = jnp.concatenate([vprev[vsl, :], vcur[vsl, :]], axis=1)
                ots.append(_dot(vtt, p_buf[sl, hh]) * stat_buf[sl, 2 * hh:2 * hh + 1, :])
                lts.append(jnp.broadcast_to(stat_buf[sl, 2 * hh + 1:2 * hh + 2, :], (half, BAND)))
            o_blk = jnp.concatenate(ots, axis=0).T
            lse_blk = jnp.concatenate(lts, axis=0).T
            if g == 0:
                lse_ref[rows, :] = lse_blk
                acc_ref[rows, :] = o_blk
            else:
                lse_run = lse_ref[rows, :]
                acc_run = acc_ref[rows, :]
                mx = jnp.maximum(lse_run, lse_blk)
                ea = jnp.exp2(lse_run - mx)
                eb = jnp.exp2(lse_blk - mx)
                tot = ea + eb
                inv = 1.0 / tot
                merged = (ea * inv) * acc_run + (eb * inv) * o_blk
                if g == len(ATTN_GROUPS) - 1:
                    o_ref[rows, :] = merged
                else:
                    lse_ref[rows, :] = mx + jnp.log2(tot)
                    acc_ref[rows, :] = merged

        def qk_pair(pair, par):
            qk_stage(2 * pair, 2 * par)
            qk_stage(2 * pair + 1, 2 * par + 1)

        def softmax_pair(pair, par):
            softmax_stage(2 * par, 2 * pair < dil)
            softmax_stage(2 * par + 1, 2 * pair + 1 < dil)

        def pv_pair(pair, par):
            pv_stage(2 * pair, 2 * par)
            pv_stage(2 * pair + 1, 2 * par + 1)

        return (carry, deinterleave), (qk_pair, softmax_pair, pv_pair), nblk // 2

    groups = [group_stages(g) for g in range(len(ATTN_GROUPS))]
    for (carry, _), _, _ in groups:
        carry()
    for (_, deinterleave), _, _ in groups:
        deinterleave()
    items = [(stages, pair) for _, stages, npairs in groups for pair in range(npairs)]
    for t in range(len(items) + 2):
        if extra is not None and t == len(items) // 2:
            extra()
        for depth in (2, 1, 0):
            if 0 <= t - depth < len(items):
                stages, pair = items[t - depth]
                stages[depth](pair, (t - depth) % 2)

    @pl.when(n == pl.num_programs(2) - 1)
    def _():
        for g, (win, _) in enumerate(ATTN_GROUPS):
            for which, ref in enumerate((k_refs[g], v_refs[g])):
                t = ref[ATTN_TILE - win:, :].T
                for hh in range(2):
                    (kvo0, kvo1, kvo2)[g][0, which, hh] = t[hh * HEAD_DIM:(hh + 1) * HEAD_DIM, :]


N_ATTN_IN, N_SAMPLE_IN, N_ATTN_OUT, N_ATTN_SCRATCH = 9, 18, 4, 11


def _attn_ssd_kernel(*refs):
    a, b = N_ATTN_IN, N_ATTN_IN + N_SAMPLE_IN
    attn_in, samp_in = refs[:a], refs[a:b]
    attn_out, samp_out = refs[b:b + N_ATTN_OUT], refs[b + N_ATTN_OUT:b + N_ATTN_OUT + 2]
    scratch = refs[b + N_ATTN_OUT + 2:]
    attn_scr, pads = scratch[:N_ATTN_SCRATCH], scratch[N_ATTN_SCRATCH:]
    _attn_prompt_kernel(*attn_in, *attn_out, *attn_scr,
                        extra=lambda: _ssd_sample_kernel(*samp_in, *samp_out, *pads))


def _attn_prompt(u, batch, seq, u_s, dt_s, conv_state, ssm_state, conv_w, conv_b, dt_bias, a_log, d_exp, e_mat,
                 n_new):
    nt = seq // ATTN_TILE
    hp = A_WIDTH // LANES
    dbatch = ssm_state.shape[0]
    assert dbatch == batch * hp * nt, "one sample batch element per prompt attention step"
    keep = CONV_W - 1
    cwx, cwb, cwc = conv_w[:, :D_INNER], conv_w[:, D_INNER:D_INNER + SSD_GROUP_W], conv_w[:, D_INNER + SSD_GROUP_W:]
    cbx, cbb, cbc = conv_b[:, :D_INNER], conv_b[:, D_INNER:D_INNER + SSD_GROUP_W], conv_b[:, D_INNER + SSD_GROUP_W:]
    sid = lambda b, h, n: (b * hp + h) * nt + n
    full = lambda shape: pl.BlockSpec(shape, lambda b, h, n: (0,) * len(shape))
    st_spec = pl.BlockSpec((1, SSD_HEADS, SSD_HEAD_DIM, SSD_STATE), lambda b, h, n: (sid(b, h, n), 0, 0, 0))
    sample_specs = [
        pl.BlockSpec((n_new, D_INNER), lambda b, h, n: (sid(b, h, n), X_OFF // D_INNER)),
        pl.BlockSpec((n_new, SSD_GROUP_W), lambda b, h, n: (sid(b, h, n), B_OFF // SSD_GROUP_W)),
        pl.BlockSpec((n_new, SSD_GROUP_W), lambda b, h, n: (sid(b, h, n), C_OFF // SSD_GROUP_W)),
        pl.BlockSpec((n_new, LANES), lambda b, h, n: (sid(b, h, n), 0)),
        pl.BlockSpec((1, keep, D_INNER), lambda b, h, n: (sid(b, h, n), 0, 0)),
        pl.BlockSpec((1, keep, SSD_GROUP_W), lambda b, h, n: (sid(b, h, n), 0, D_INNER // SSD_GROUP_W)),
        pl.BlockSpec((1, keep, SSD_GROUP_W), lambda b, h, n: (sid(b, h, n), 0, D_INNER // SSD_GROUP_W + 1)),
        st_spec,
        full((CONV_W, D_INNER)), full((CONV_W, SSD_GROUP_W)), full((CONV_W, SSD_GROUP_W)),
        full((1, D_INNER)), full((1, SSD_GROUP_W)), full((1, SSD_GROUP_W)),
        full((1, LANES)), full((1, LANES)), full((1, D_INNER)), full((LANES, D_INNER)),
    ]
    assert len(sample_specs) == N_SAMPLE_IN
    sample_args = (u_s, u_s, u_s, dt_s, conv_state, conv_state, conv_state, ssm_state,
                   cwx, cwb, cwc, cbx, cbb, cbc, dt_bias, a_log, d_exp, e_mat)

    def spec(g, which):
        col = (3 * g + which) * hp
        return pl.BlockSpec((ATTN_TILE, LANES), lambda b, h, n, col=col: (b * nt + n, col + h))

    in_specs = [spec(g, w) for g in range(3) for w in range(3)]
    scratch = []
    for _, dil in ATTN_GROUPS:
        nslots = dil * (ATTN_TILE // (BAND * dil) + 1)
        scratch += [pltpu.VMEM((nslots, BAND, LANES), BF16), pltpu.VMEM((nslots, LANES, BAND), BF16)]
    scratch += [pltpu.VMEM((ATTN_TILE, LANES), F32), pltpu.VMEM((ATTN_TILE, LANES), F32)]
    scratch += [pltpu.VMEM((4, 2 * BAND, 2 * BAND), F32), pltpu.VMEM((4, 2, 2 * BAND, BAND), BF16),
                pltpu.VMEM((4, HALO, BAND), F32)]
    scratch += [pltpu.VMEM((HALO + n_new, D_INNER), F32), pltpu.VMEM((HALO + n_new, SSD_GROUP_W), F32),
                pltpu.VMEM((HALO + n_new, SSD_GROUP_W), F32)]
    return pl.pallas_call(
        _attn_ssd_kernel,
        grid=(batch, hp, nt),
        in_specs=in_specs + sample_specs,
        out_specs=[pl.BlockSpec((ATTN_TILE, LANES), lambda b, h, n: (b * nt + n, h))]
        + [pl.BlockSpec((1, 2, 2, HEAD_DIM, w), lambda b, h, n: (b, 0, h, 0, 0)) for w, _ in ATTN_GROUPS]
        + [pl.BlockSpec((n_new, D_INNER), lambda b, h, n: (sid(b, h, n), 0)), st_spec],
        out_shape=[jax.ShapeDtypeStruct((batch * seq, A_WIDTH), F32)]
        + [jax.ShapeDtypeStruct((batch, 2, HEADS, HEAD_DIM, w), F32) for w, _ in ATTN_GROUPS]
        + [jax.ShapeDtypeStruct((dbatch * n_new, D_INNER), F32), jax.ShapeDtypeStruct(ssm_state.shape, F32)],
        scratch_shapes=scratch,
        compiler_params=_cparams(("parallel", "parallel", "arbitrary")),
        name="attn_prompt",
    )(*([u] * 9), *sample_args)


def _conv_silu(pad_ref, w_ref, b_ref, rows):
    acc = b_ref[...] + pad_ref[HALO - (CONV_W - 1):HALO - (CONV_W - 1) + rows, :] * w_ref[0:1, :]
    for i in range(1, CONV_W):
        off = HALO - (CONV_W - 1) + i
        acc = acc + pad_ref[off:off + rows, :] * w_ref[i:i + 1, :]
    return _silu(acc)


def _conv_silu_chunk(x_ref, tail_ref, w_ref, b_ref, rows, cols):
    x = x_ref[rows, cols].astype(F32)
    ext = jnp.concatenate([tail_ref[:, cols], x], axis=0)
    tail_ref[:, cols] = x[x.shape[0] - HALO:, :]
    acc = b_ref[:, cols] + x * w_ref[CONV_W - 1:CONV_W, cols]
    for k in range(1, CONV_W):
        acc = acc + pltpu.roll(ext, k, 0)[HALO:, :] * w_ref[CONV_W - 1 - k:CONV_W - k, cols]
    return _silu(acc)


def _softplus(x):
    return jnp.maximum(x, 0.0) + jnp.log1p(jnp.exp(-jnp.abs(x)))


def _ssd_prompt_kernel(x_ref, b_ref, c_ref, dt_ref, dtn_ref, cwx, cwb, cwc, cbx, cbb, cbc,
                       dtb_ref, alog_ref, dexp_ref, e_ref, q_ref, new_ref, c0, c1, c2,
                       y_ref, st_ref, ao_ref, o0, o1, o2,
                       xtail, btail, ctail, state, ex_a, ex_b, cum_a, cum_b):
    c = pl.program_id(1)
    nc = pl.num_programs(1)

    @pl.when(c == 0)
    def _():
        xtail[...] = jnp.zeros(xtail.shape, F32)
        btail[...] = jnp.zeros(btail.shape, F32)
        ctail[...] = jnp.zeros(ctail.shape, F32)
        state[...] = jnp.zeros(state.shape, F32)

    ti = lax.broadcasted_iota(jnp.int32, (CHUNK, CHUNK), 0)
    si = lax.broadcasted_iota(jnp.int32, (CHUNK, CHUNK), 1)
    causal = si <= ti
    lane = lax.broadcasted_iota(jnp.int32, (CHUNK, LANES), 1)
    head0 = lane < SSD_HEAD_DIM
    hpg = SSD_HEADS // SSD_GROUPS

    def prepare(dt_blk, ex_ref, cum_ref):
        dt = _softplus(dt_blk + dtb_ref[...])
        dta = dt * -jnp.exp(alog_ref[...])
        tril = jnp.where(causal, 1.0, 0.0).astype(F32)
        cum = jnp.dot(tril, dta, preferred_element_type=F32, precision=lax.Precision.HIGHEST)
        cum2 = cum * LOG2E
        cum_ref[0] = cum2
        cum_ref[1] = cum2.T
        expcum = jnp.exp2(cum2)
        to_end = jnp.exp2(cum2[CHUNK - 1:CHUNK, :] - cum2) * dt
        ex_ref[...] = _expand01(jnp.concatenate([expcum, to_end, dt], axis=0), e_ref[...])

    @pl.when(c == 0)
    def _():
        prepare(dt_ref[0:CHUNK, :], ex_a, cum_a)

    def chunk(rows, ex_s, cum_s, hooks):
        cum2 = cum_s[0]
        cum2_t = cum_s[1]
        dec_x = ex_s[CHUNK - 1:CHUNK, :]
        for g in range(SSD_GROUPS):
            for hook in hooks[g]:
                hook()
            gs = slice(g * SSD_GROUP_W, (g + 1) * SSD_GROUP_W)
            ns = slice(g * SSD_STATE, (g + 1) * SSD_STATE)
            bg = _conv_silu_chunk(b_ref, btail, cwb, cbb, rows, ns).astype(BF16)
            cg = _conv_silu_chunk(c_ref, ctail, cwc, cbc, rows, ns).astype(BF16)
            xg = _conv_silu_chunk(x_ref, xtail, cwx, cbx, rows, gs)
            cb = _dot_nt(cg, bg)
            xdt = xg * ex_s[2 * CHUNK:3 * CHUNK, gs]
            ys = []
            for pr in range(hpg // 2):
                ws = []
                for hh in range(2):
                    h = g * hpg + 2 * pr + hh
                    colb = jnp.broadcast_to(cum2[:, h:h + 1], (CHUNK, CHUNK))
                    rowb = jnp.broadcast_to(cum2_t[h:h + 1, :], (CHUNK, CHUNK))
                    decay = jnp.exp2(jnp.where(causal, colb - rowb, -jnp.inf))
                    ws.append((cb * decay).astype(BF16))
                w2 = jnp.concatenate(ws, axis=1)
                xp = xdt[:, pr * LANES:(pr + 1) * LANES]
                x2 = jnp.concatenate([jnp.where(head0, xp, 0.0), jnp.where(head0, 0.0, xp)], axis=0).astype(BF16)
                ys.append(_dot(w2, x2))
            st = state[g]
            y_state = _dot(cg, st.astype(BF16))
            y_ref[rows, gs] = jnp.concatenate(ys, axis=1) + y_state * ex_s[0:CHUNK, gs] + dexp_ref[:, gs] * xg
            xs = (xg * ex_s[CHUNK:2 * CHUNK, gs]).astype(BF16)
            state[g] = dec_x[:, gs] * st + _dot_tn(bg, xs)

    sample_pieces = [p for hx in range(q_ref.shape[1])
                     for p in _attn_sample_pieces(hx, q_ref, new_ref, (c0, c1, c2), ao_ref, (o0, o1, o2))]
    nslots = 2 * SSD_GROUPS
    hooks = [[] for _ in range(nslots)]
    for idx, piece in enumerate(sample_pieces):
        hooks[idx * nslots // len(sample_pieces)].append(piece)
    hooks[1].append(lambda: prepare(dt_ref[CHUNK:2 * CHUNK, :], ex_b, cum_b))
    hooks[SSD_GROUPS + 1].append(lambda: prepare(dtn_ref[...], ex_a, cum_a))
    chunk(slice(0, CHUNK), ex_a, cum_a, hooks[:SSD_GROUPS])
    chunk(slice(CHUNK, 2 * CHUNK), ex_b, cum_b, hooks[SSD_GROUPS:])

    @pl.when(c == nc - 1)
    def _():
        for g in range(SSD_GROUPS):
            st_ref[0, g * hpg:(g + 1) * hpg, :, :] = state[g].T.reshape(hpg, SSD_HEAD_DIM, SSD_STATE)


def _ssd_prompt(u, dt_raw, conv_w, conv_b, dt_bias, a_log, d_exp, e_mat, batch, seq, q_s, new_t, caches):
    nc = seq // CHUNK
    cwx, cwb, cwc = conv_w[:, :D_INNER], conv_w[:, D_INNER:D_INNER + SSD_GROUP_W], conv_w[:, D_INNER + SSD_GROUP_W:]
    cbx, cbb, cbc = conv_b[:, :D_INNER], conv_b[:, D_INNER:D_INNER + SSD_GROUP_W], conv_b[:, D_INNER + SSD_GROUP_W:]
    full = lambda shape: pl.BlockSpec(shape, lambda b, c: (0,) * len(shape))
    dbatch, _, _, n_new, _ = q_s.shape
    assert nc % 2 == 0
    nchunks, nc, rows = nc, nc // 2, 2 * CHUNK
    steps = batch * nc
    assert (dbatch * HEADS) % steps == 0, "sample heads must spread evenly over the prompt's SSD steps"
    hps = dbatch * HEADS // steps
    assert HEADS % hps == 0
    spb = HEADS // hps
    sb = lambda b, c: (b * nc + c) // spb
    hb = lambda b, c: (b * nc + c) % spb
    q_spec = pl.BlockSpec((1, hps, len(ATTN_GROUPS), n_new, HEAD_DIM), lambda b, c: (sb(b, c), hb(b, c), 0, 0, 0))
    new_spec = pl.BlockSpec((1, hps, HEAD_DIM, LANES), lambda b, c: (sb(b, c), hb(b, c), 0, 0))
    ao_spec = pl.BlockSpec((1, hps, n_new, HEAD_DIM), lambda b, c: (sb(b, c), hb(b, c), 0, 0))
    cache_specs = [pl.BlockSpec((1, 2, hps, HEAD_DIM, w), lambda b, c: (sb(b, c), 0, hb(b, c), 0, 0))
                   for w, _ in ATTN_GROUPS]
    return pl.pallas_call(
        _ssd_prompt_kernel,
        grid=(batch, nc),
        in_specs=[
            pl.BlockSpec((rows, D_INNER), lambda b, c: (b * nc + c, X_OFF // D_INNER)),
            pl.BlockSpec((rows, SSD_GROUP_W), lambda b, c: (b * nc + c, B_OFF // SSD_GROUP_W)),
            pl.BlockSpec((rows, SSD_GROUP_W), lambda b, c: (b * nc + c, C_OFF // SSD_GROUP_W)),
            pl.BlockSpec((rows, LANES), lambda b, c: (b * nc + c, 0)),
            pl.BlockSpec((CHUNK, LANES), lambda b, c: (b * nchunks + jnp.minimum(2 * c + 2, nchunks - 1), 0)),
            full((CONV_W, D_INNER)), full((CONV_W, SSD_GROUP_W)), full((CONV_W, SSD_GROUP_W)),
            full((1, D_INNER)), full((1, SSD_GROUP_W)), full((1, SSD_GROUP_W)),
            full((1, LANES)), full((1, LANES)), full((1, D_INNER)), full((LANES, D_INNER)),
            q_spec, new_spec,
        ] + cache_specs,
        out_specs=[
            pl.BlockSpec((rows, D_INNER), lambda b, c: (b * nc + c, 0)),
            pl.BlockSpec((1, SSD_HEADS, SSD_HEAD_DIM, SSD_STATE), lambda b, c: (b, 0, 0, 0)),
            ao_spec,
        ] + cache_specs,
        out_shape=[
            jax.ShapeDtypeStruct((batch * seq, D_INNER), F32),
            jax.ShapeDtypeStruct((batch, SSD_HEADS, SSD_HEAD_DIM, SSD_STATE), F32),
            jax.ShapeDtypeStruct((dbatch, HEADS, n_new, HEAD_DIM), F32),
        ] + [jax.ShapeDtypeStruct(cc.shape, F32) for cc in caches],
        scratch_shapes=[
            pltpu.VMEM((HALO, D_INNER), F32),
            pltpu.VMEM((HALO, SSD_GROUP_W), F32),
            pltpu.VMEM((HALO, SSD_GROUP_W), F32),
            pltpu.VMEM((SSD_GROUPS, SSD_STATE, SSD_GROUP_W), F32),
            pltpu.VMEM((3 * CHUNK, D_INNER), F32), pltpu.VMEM((3 * CHUNK, D_INNER), F32),
            pltpu.VMEM((2, CHUNK, LANES), F32), pltpu.VMEM((2, CHUNK, LANES), F32),
        ],
        compiler_params=_cparams(("parallel", "arbitrary")),
        name="ssd_prompt",
    )(u, u, u, dt_raw, dt_raw, cwx, cwb, cwc, cbx, cbb, cbc, dt_bias, a_log, d_exp, e_mat, q_s, new_t, *caches)


def _mix_ffn_kernel(a_ref, y_ref, z_ref, ga_ref, gb_ref, x_ref, wa_ref, ws_ref, wo_ref,
                    sn_ref, pn_ref, wu_ref, wd_ref, n1_ref, n2_ref, o_ref):
    a_proj = _dot(a_ref[...].astype(BF16), wa_ref[...])
    g = y_ref[...] * _silu(z_ref[...].astype(F32))
    parts = []
    for i in range(SSD_GROUPS):
        gi = g[:, i * SSD_GROUP_W:(i + 1) * SSD_GROUP_W]
        parts.append(gi * lax.rsqrt(jnp.mean(gi * gi, axis=-1, keepdims=True) + RMS_EPS))
    yn = (jnp.concatenate(parts, axis=1) * sn_ref[...]).astype(BF16)
    b_proj = _dot(yn, ws_ref[...])
    mixed_in = _sigmoid(ga_ref[...].astype(F32)) * a_proj + _sigmoid(gb_ref[...].astype(F32)) * b_proj
    mixed = _dot(mixed_in.astype(BF16), wo_ref[...])
    x1 = x_ref[...] + _rms(mixed, pn_ref[...])
    h = _rms(x1, n1_ref[...]).astype(BF16)
    gu = _dot(h, wu_ref[...])
    act = (_silu(gu[:, :D_FF]) * gu[:, D_FF:]).astype(BF16)
    f = _dot(act, wd_ref[...])
    o_ref[...] = x1 + _rms(f, n2_ref[...])


def _mix_ffn(a_out, y, u, x, w_attn_out, w_ssd_out, w_out, ssd_norm, post_mix_norm,
             w_up, w_down, pre_norm, post_norm, tm):
    t = x.shape[0]
    const = lambda shape: pl.BlockSpec(shape, lambda i: (0, 0), pipeline_mode=pl.Buffered(1))
    return pl.pallas_call(
        _mix_ffn_kernel,
        grid=(t // tm,),
        in_specs=[
            pl.BlockSpec((tm, A_WIDTH), lambda i: (i, 0)),
            pl.BlockSpec((tm, D_INNER), lambda i: (i, 0)),
            pl.BlockSpec((tm, D_INNER), lambda i: (i, Z_OFF // D_INNER)),
            pl.BlockSpec((tm, D_MODEL), lambda i: (i, GATE_OFF // D_MODEL)),
            pl.BlockSpec((tm, D_MODEL), lambda i: (i, GATE_OFF // D_MODEL + 1)),
            pl.BlockSpec((tm, D_MODEL), lambda i: (i, 0)),
            const((A_WIDTH, D_MODEL)), const((D_INNER, D_MODEL)), const((D_MODEL, D_MODEL)),
            const((1, D_INNER)), const((1, D_MODEL)),
            const((D_MODEL, 2 * D_FF)), const((D_FF, D_MODEL)),
            const((1, D_MODEL)), const((1, D_MODEL)),
        ],
        out_specs=pl.BlockSpec((tm, D_MODEL), lambda i: (i, 0)),
        out_shape=jax.ShapeDtypeStruct((t, D_MODEL), F32),
        compiler_params=_cparams(("parallel",)),
        name="mix_ffn",
    )(a_out, y, u, u, u, x, w_attn_out, w_ssd_out, w_out, ssd_norm, post_mix_norm,
      w_up, w_down, pre_norm, post_norm)


def _attn_sample_pieces(hx, q_ref, new_ref, caches, ao_ref, outs):
    n_new = q_ref.shape[3]
    run = {}

    def new_cols(g, which):
        return pltpu.roll(new_ref[0, hx], LANES - n_new - (2 * g + which) * n_new, 1)

    def attend(g):
        win, dil = ATTN_GROUPS[g]
        q = (q_ref[0, hx, g] * ATTN_SCALE).astype(BF16)
        k_t = caches[g][0, 0, hx]
        v_t = caches[g][0, 1, hx]
        knt, vnt = new_cols(g, 0), new_cols(g, 1)
        qi = lax.broadcasted_iota(jnp.int32, (n_new, win), 0)
        ci = lax.broadcasted_iota(jnp.int32, (n_new, win), 1)
        ok_c = jnp.logical_and(ci >= qi, ((ci - qi) & (dil - 1)) == 0)
        qn = lax.broadcasted_iota(jnp.int32, (n_new, LANES), 0)
        nn = lax.broadcasted_iota(jnp.int32, (n_new, LANES), 1) - (LANES - n_new)
        ok_n = jnp.logical_and(jnp.logical_and(nn >= 0, nn <= qn), ((qn - nn) & (dil - 1)) == 0)
        s_c = jnp.where(ok_c, _dot(q, k_t.astype(BF16)), -jnp.inf)
        s_n = jnp.where(ok_n, _dot(q, knt.astype(BF16)), -jnp.inf)
        m = jnp.maximum(jnp.max(s_c, axis=1, keepdims=True), jnp.max(s_n, axis=1, keepdims=True))
        p_c = jnp.exp(s_c - m)
        p_n = jnp.exp(s_n - m)
        den = jnp.sum(p_c, axis=1, keepdims=True) + jnp.sum(p_n, axis=1, keepdims=True)
        o = _dot_nt(p_c.astype(BF16), v_t.astype(BF16)) + _dot_nt(p_n.astype(BF16), vnt.astype(BF16))
        if not run:
            run.update(m=m, l=den, o=o)
        else:
            m_new = jnp.maximum(run["m"], m)
            ea = jnp.exp(run["m"] - m_new)
            eb = jnp.exp(m - m_new)
            run.update(m=m_new, l=ea * run["l"] + eb * den, o=ea * run["o"] + eb * o)
        if g == len(ATTN_GROUPS) - 1:
            ao_ref[0, hx] = run["o"] * (1.0 / run["l"])

    def shift(g, which):
        win = ATTN_GROUPS[g][0]
        keep = lax.broadcasted_iota(jnp.int32, (HEAD_DIM, LANES), 1) < LANES - n_new
        rolled = pltpu.roll(caches[g][0, which, hx], win - n_new, 1)
        if win > LANES:
            outs[g][0, which, hx, :, 0:win - LANES] = rolled[:, 0:win - LANES]
        outs[g][0, which, hx, :, win - LANES:win] = jnp.where(keep, rolled[:, win - LANES:win], new_cols(g, which))

    pieces = []
    for g in range(len(ATTN_GROUPS)):
        pieces += [functools.partial(attend, g), functools.partial(shift, g, 0), functools.partial(shift, g, 1)]
    return pieces


def _ssd_sample_kernel(x_ref, b_ref, c_ref, dt_ref, sx_ref, sb_ref, sc_ref, st_in,
                       cwx, cwb, cwc, cbx, cbb, cbc, dtb_ref, alog_ref, dexp_ref, e_ref,
                       y_ref, st_out, xpad, bpad, cpad):
    n = x_ref.shape[0]
    keep = CONV_W - 1
    for pad, new, old in ((xpad, x_ref, sx_ref), (bpad, b_ref, sb_ref), (cpad, c_ref, sc_ref)):
        pad[0:HALO, :] = jnp.zeros((HALO, pad.shape[1]), F32)
        pad[HALO - keep:HALO, :] = old[0]
        pad[HALO:HALO + n, :] = new[...]
    xc = _conv_silu(xpad, cwx, cbx, n)
    bm = _conv_silu(bpad, cwb, cbb, n)
    cm = _conv_silu(cpad, cwc, cbc, n)

    dt = _softplus(dt_ref[...] + dtb_ref[...])
    a = -jnp.exp(alog_ref[...])
    dta = dt * a
    ti = lax.broadcasted_iota(jnp.int32, (n, n), 0)
    si = lax.broadcasted_iota(jnp.int32, (n, n), 1)
    tril = jnp.where(si <= ti, 1.0, 0.0).astype(F32)
    cum = jnp.dot(tril, dta, preferred_element_type=F32, precision=lax.Precision.HIGHEST)
    cum_last = cum[n - 1:n, :]
    expcum = jnp.exp(cum)
    to_end = jnp.exp(cum_last - cum) * dt

    hpg = SSD_HEADS // SSD_GROUPS
    lane = lax.broadcasted_iota(jnp.int32, (n, LANES), 1)
    trow = lax.broadcasted_iota(jnp.int32, (n, LANES), 0)
    cbs = [_dot_nt(cm[:, g * SSD_STATE:(g + 1) * SSD_STATE].astype(BF16),
                   bm[:, g * SSD_STATE:(g + 1) * SSD_STATE].astype(BF16)) for g in range(SSD_GROUPS)]
    coefs = []
    for s in range(n):
        cbh = jnp.zeros((n, LANES), F32)
        for g in range(SSD_GROUPS):
            in_g = jnp.logical_and(lane >= g * hpg, lane < (g + 1) * hpg)
            cbh = jnp.where(in_g, jnp.broadcast_to(cbs[g][:, s:s + 1], (n, LANES)), cbh)
        decay = jnp.exp(jnp.where(trow >= s, cum - cum[s:s + 1, :], -jnp.inf))
        coefs.append(cbh * decay * dt[s:s + 1, :])
    ex = _expand01(jnp.concatenate(coefs + [expcum, to_end], axis=0), e_ref[...])
    y = dexp_ref[...] * xc
    for s in range(n):
        y = y + ex[s * n:(s + 1) * n, :] * xc[s:s + 1, :]
    expcum_x = ex[n * n:n * n + n, :]
    to_end_x = ex[n * n + n:n * n + 2 * n, :]
    xs = xc * to_end_x
    dec_t = jnp.broadcast_to(expcum[n - 1:n, :], (LANES, LANES)).T
    ys = []
    for g in range(SSD_GROUPS):
        gs = slice(g * SSD_GROUP_W, (g + 1) * SSD_GROUP_W)
        ns = slice(g * SSD_STATE, (g + 1) * SSD_STATE)
        st = st_in[0, g * hpg:(g + 1) * hpg].reshape(SSD_GROUP_W, SSD_STATE)
        ys.append(_dot_nt(cm[:, ns].astype(BF16), st.astype(BF16)))
        upd = _dot_tn(xs[:, gs].astype(BF16), bm[:, ns].astype(BF16))
        for hh in range(hpg):
            h = g * hpg + hh
            rows = slice(hh * SSD_HEAD_DIM, (hh + 1) * SSD_HEAD_DIM)
            st_out[0, h] = dec_t[h:h + 1, :] * st[rows, :] + upd[rows, :]
    y_ref[...] = y + jnp.concatenate(ys, axis=1) * expcum_x


def _prep_weights(w_in, dt_bias, d_skip):
    q_end = 3 * len(ATTN_GROUPS) * A_WIDTH
    z_end = q_end + D_INNER
    x_end = z_end + D_INNER
    b_end = x_end + SSD_GROUP_W
    c_end = b_end + SSD_GROUP_W
    dt_end = c_end + SSD_HEADS
    w_main = jnp.concatenate(
        [w_in[:, q_end:z_end], w_in[:, dt_end:], w_in[:, z_end:c_end], w_in[:, :q_end]], axis=1).astype(BF16)
    w_dt = jnp.pad(w_in[:, c_end:dt_end], ((0, 0), (0, LANES - SSD_HEADS))).astype(BF16)
    dtb = jnp.pad(dt_bias, (0, LANES - SSD_HEADS)).reshape(1, LANES)
    d_exp = jnp.repeat(d_skip, SSD_HEAD_DIM).reshape(1, D_INNER)
    return w_main, w_dt, dtb, d_exp


def _tail_conv(u, batch, seq):
    u3 = u.reshape(batch, seq, N_REST)
    return u3[:, seq - (CONV_W - 1):, X_OFF:C_OFF + SSD_GROUP_W].astype(F32)


def kernel(x_prompt, x_sample, cache_kv_w128, cache_kv_w512, cache_kv_w2048, state_conv, state_ssm,
           pre_mix_norm, w_in, conv_w, conv_b, dt_bias, a_log, d_skip, ssd_norm, w_attn_out, w_ssd_out,
           w_out, post_mix_norm, pre_ffn_norm, w_up, w_down, post_ffn_norm):
    batch, seq, _ = x_prompt.shape
    dbatch, dseq, _ = x_sample.shape
    assert w_in.shape[0] == 1, "single-layer trunk"
    assert seq % ATTN_TILE == 0 and dseq == HALO

    w_main, w_dt, dtb, d_exp = _prep_weights(w_in[0], dt_bias[0], d_skip[0])
    alog = jnp.pad(a_log[0], (0, LANES - SSD_HEADS)).reshape(1, LANES)
    e_mat = (jnp.arange(LANES)[:, None] == (jnp.arange(D_INNER)[None, :] // SSD_HEAD_DIM)).astype(BF16)
    nw = pre_mix_norm[0].reshape(1, D_MODEL)
    cw, cb = conv_w[0], conv_b[0].reshape(1, -1)
    wa, ws, wo = w_attn_out[0].astype(BF16), w_ssd_out[0].astype(BF16), w_out[0].astype(BF16)
    wu, wd = w_up[0].astype(BF16), w_down[0].astype(BF16)
    sn = ssd_norm[0].reshape(1, D_INNER)
    pmn, pfn, qfn = (post_mix_norm[0].reshape(1, -1), pre_ffn_norm[0].reshape(1, -1),
                     post_ffn_norm[0].reshape(1, -1))

    xp = x_prompt.reshape(batch * seq, D_MODEL)
    xs = x_sample.reshape(dbatch * dseq, D_MODEL)
    ur_p, uq_p, dt_p = _inproj(xp, nw, w_main, w_dt, min(INPROJ_ROWS, batch * seq), INPROJ_COLS, BF16)
    ur_s, uq_s, dt_s = _inproj(xs, nw, w_main, w_dt, min(INPROJ_ROWS, dbatch * dseq), INPROJ_COLS, F32)
    qkv = uq_s.reshape(dbatch, dseq, 3, 3, HEADS, HEAD_DIM)
    q_s = qkv[:, :, :, 0].transpose(0, 3, 2, 1, 4)
    new_t = qkv[:, :, :, 1:3].transpose(0, 4, 5, 2, 3, 1).reshape(dbatch, HEADS, HEAD_DIM, 6 * dseq)
    new_t = jnp.pad(new_t, ((0, 0), (0, 0), (0, 0), (0, LANES - 6 * dseq)))
    caches = [c[0].transpose(0, 2, 3, 4, 1) for c in (cache_kv_w128, cache_kv_w512, cache_kv_w2048)]

    a_p, *kvt_p, y_s, ssm_s = _attn_prompt(uq_p, batch, seq, ur_s, dt_s, state_conv[0], state_ssm[0], cw, cb, dtb,
                                           alog, d_exp, e_mat, dseq)
    y_p, ssm_p, ao, nc0, nc1, nc2 = _ssd_prompt(ur_p, dt_p, cw, cb, dtb, alog, d_exp, e_mat, batch, seq,
                                                q_s, new_t, caches)
    out_p = _mix_ffn(a_p, y_p, ur_p, xp, wa, ws, wo, sn, pmn, wu, wd, pfn, qfn,
                     min(MIX_FFN_ROWS, batch * seq)).reshape(batch, seq, D_MODEL)
    kv_p = [t.transpose(0, 4, 1, 2, 3)[None] for t in kvt_p]
    conv_p = _tail_conv(ur_p, batch, seq)[None]

    a_s = ao.transpose(0, 2, 1, 3).reshape(dbatch * dseq, A_WIDTH)
    kv_s = [c.transpose(0, 4, 1, 2, 3)[None] for c in (nc0, nc1, nc2)]
    tm_s = min(MIX_FFN_ROWS, dbatch * dseq)
    out_s = _mix_ffn(a_s, y_s, ur_s, xs, wa, ws, wo, sn, pmn, wu, wd, pfn, qfn, tm_s).reshape(dbatch, dseq, D_MODEL)
    conv_s = _tail_conv(ur_s, dbatch, dseq)[None]

    return (out_p, out_s, kv_p[0], kv_p[1], kv_p[2], conv_p, ssm_p[None],
            kv_s[0], kv_s[1], kv_s[2], conv_s, ssm_s[None])
```
